```python
import math
import jax, jax.numpy as jnp
from jax import lax
import numpy as np

D_MODEL = 1024
BATCH = 8
SEQ = 2048
DEPTH = 4

RET_HEADS = 4
RET_DK = 128
RET_DV = 256
RET_CHUNK = 128
RET_THETA = 10000.0
DIL_HEADS = 8
DIL_HD = 64
DIL_PAIRS = ((128, 1), (512, 4), (2048, 16))
DIL_BLOCK = 128
ROPE_THETA = 500000.0
ROPE_DIMS = DIL_HD // 4
GDN_HEADS = 8
GDN_DK = 128
GDN_DV = 128
GDN_CHUNK = 64
GDN_CONV = 4
D_FF = 2816
FFN_CONV = 3
DN_ALPHA = (2.0 * DEPTH) ** 0.25
DN_BETA = (8.0 * DEPTH) ** -0.25
EPS = 1e-5
N_EVEN = (DEPTH + 1) // 2
N_ODD = DEPTH // 2

RET_QK_W = RET_HEADS * RET_DK
RET_V_W = RET_HEADS * RET_DV
DIL_W = DIL_HEADS * DIL_HD
EV_IN = 2 * RET_QK_W + 2 * RET_V_W + 3 * DIL_W
EV_MIX = RET_V_W + DIL_W
GDN_W = GDN_HEADS * GDN_DK
OD_IN = 4 * GDN_W + 2 * GDN_HEADS

kernel_name = "hybrid_retention_dilated_gdn_deepnorm"

F32 = jnp.float32


def _layernorm(x, g, b):
    xf = x.astype(F32)
    mu = jnp.mean(xf, -1, keepdims=True)
    var = jnp.mean(jnp.square(xf - mu), -1, keepdims=True)
    return ((xf - mu) * lax.rsqrt(var + EPS) * g.astype(F32) + b.astype(F32)).astype(x.dtype)


def _rms(xf):
    return xf * lax.rsqrt(jnp.mean(xf * xf, -1, keepdims=True) + EPS)


def _l2norm(xf):
    return xf * lax.rsqrt(jnp.sum(xf * xf, -1, keepdims=True) + 1e-6)


def _heads(t, n_heads):
    b, s, w = t.shape
    return t.reshape(b, s, n_heads, w // n_heads).transpose(0, 2, 1, 3)


def _merge(t):
    b, h, s, d = t.shape
    return t.transpose(0, 2, 1, 3).reshape(b, s, h * d)


def _rotary(x, positions, n_rot, theta):
    half = n_rot // 2
    inv = jnp.power(theta, -jnp.arange(half, dtype=F32) * 2.0 / n_rot)
    ang = positions.astype(F32)[:, None, :, None] * inv
    cos, sin = jnp.cos(ang), jnp.sin(ang)
    xf = x.astype(F32)
    x1, x2 = xf[..., :half], xf[..., half:n_rot]
    out = jnp.concatenate([x1 * cos - x2 * sin, x2 * cos + x1 * sin, xf[..., n_rot:]], -1)
    return out.astype(x.dtype)


def _causal_dwconv(x, w):
    k_w = w.shape[0]
    s = x.shape[1]
    xp = jnp.pad(x, ((0, 0), (k_w - 1, 0), (0, 0)))
    return sum(xp[:, j:j + s] * w[j] for j in range(k_w))


def _retention(q, k, v):
    b, h, s, dk = q.shape
    dv = v.shape[-1]
    c = RET_CHUNK
    n = s // c
    lg = jnp.log1p(-jnp.power(2.0, -5.0 - jnp.arange(h, dtype=F32)))
    idx = jnp.arange(c, dtype=F32)
    rel = idx[:, None] - idx[None, :]
    dmat = jnp.where(rel >= 0, jnp.exp(jnp.maximum(rel, 0.0) * lg[:, None, None]), 0.0)
    zeta = jnp.exp((c - 1 - idx) * lg[:, None])
    xi = jnp.exp((idx + 1) * lg[:, None])
    g_chunk = jnp.exp(c * lg)
    qc = q.reshape(b, h, n, c, dk)
    kc = k.reshape(b, h, n, c, dk)
    vc = v.reshape(b, h, n, c, dv)
    scores = jnp.einsum('bhnid,bhnjd->bhnij', qc, kc) * dmat[None, :, None]
    o_intra = jnp.einsum('bhnij,bhnje->bhnie', scores, vc)
    kv = jnp.einsum('bhnjd,bhnje->bhnde', kc * zeta[None, :, None, :, None], vc)

    def step(r_state, kv_n):
        return r_state * g_chunk[None, :, None, None] + kv_n, r_state

    _, r_prev = lax.scan(step, jnp.zeros((b, h, dk, dv), F32), jnp.moveaxis(kv, 2, 0))
    r_prev = jnp.moveaxis(r_prev, 0, 2)
    o_inter = jnp.einsum('bhnid,bhnde->bhnie', qc, r_prev) * xi[None, :, None, :, None]
    return (o_intra + o_inter).reshape(b, h, s, dv)


def _dilated_branch(q, k, v, window, dilation):
    b, h, s, hd = q.shape
    L = s // dilation
    wd = window // dilation
    qb_sz = DIL_BLOCK
    nb = -(-L // qb_sz)
    lp = nb * qb_sz

    def sub(t):
        t = t.reshape(b, h, L, dilation, hd).transpose(0, 1, 3, 2, 4)
        return jnp.pad(t, ((0, 0), (0, 0), (0, 0), (0, lp - L), (0, 0)))

    def band(t):
        tb = t.reshape(b, h, dilation, nb, qb_sz, hd)
        prev = jnp.pad(tb, ((0, 0), (0, 0), (0, 0), (1, 0), (0, 0), (0, 0)))[:, :, :, :nb]
        return jnp.concatenate([prev, tb], axis=4)

    qs = sub(q).reshape(b, h, dilation, nb, qb_sz, hd)
    kb = band(sub(k))
    vb = band(sub(v))
    qpos = jnp.arange(nb)[:, None] * qb_sz + jnp.arange(qb_sz)[None, :]
    kpos = jnp.arange(nb)[:, None] * qb_sz - qb_sz + jnp.arange(2 * qb_sz)[None, :]
    dist = qpos[:, :, None] - kpos[:, None, :]
    mask = (dist >= 0) & (dist <= wd) & (kpos[:, None, :] >= 0)
    sc = jnp.einsum('bhrnqd,bhrnkd->bhrnqk', qs, kb).astype(F32) * (hd ** -0.5)
    sc = jnp.where(mask, sc, -jnp.inf)
    m = jnp.max(sc, -1)
    p = jnp.exp(sc - m[..., None])
    l = jnp.sum(p, -1)
    o = jnp.einsum('bhrnqk,bhrnkd->bhrnqd', p, vb.astype(F32)) / l[..., None]

    def unsub(t):
        tail = t.shape[5:]
        t = t.reshape(b, h, dilation, lp, *tail)[:, :, :, :L]
        t = jnp.moveaxis(t, 2, 3)
        return t.reshape(b, h, s, *tail)

    return unsub(o), unsub(m), unsub(l)


def _dilated_attention(q, k, v):
    outs = [_dilated_branch(q, k, v, w, r) for (w, r) in DIL_PAIRS]
    m_all = jnp.stack([m for (_, m, _) in outs])
    m_max = jnp.max(m_all, 0)
    wts = jnp.stack([l * jnp.exp(m - m_max) for (_, m, l) in outs])
    o_all = jnp.stack([o for (o, _, _) in outs])
    return jnp.sum(wts[..., None] * o_all, 0) / jnp.sum(wts, 0)[..., None]


def _even_mixer(x, positions, w_in, w_out):
    hproj = x @ w_in
    sizes = (RET_QK_W, RET_QK_W, RET_V_W, RET_V_W, DIL_W, DIL_W, DIL_W)
    offs = [0]
    for sz in sizes:
        offs.append(offs[-1] + sz)
    qa, ka, va, ga, qb, kb, vb = [hproj[..., offs[i]:offs[i + 1]] for i in range(len(sizes))]
    qa = _rotary(_heads(qa, RET_HEADS), positions, RET_DK, RET_THETA).astype(F32)
    ka = _rotary(_heads(ka, RET_HEADS), positions, RET_DK, RET_THETA).astype(F32) * (RET_DK ** -0.5)
    ya = _rms(_retention(qa, ka, _heads(va, RET_HEADS).astype(F32)))
    ya = (_merge(ya) * jax.nn.silu(ga.astype(F32))).astype(x.dtype)
    qb = _rotary(_heads(qb, DIL_HEADS), positions, ROPE_DIMS, ROPE_THETA)
    kb = _rotary(_heads(kb, DIL_HEADS), positions, ROPE_DIMS, ROPE_THETA)
    yb = _merge(_dilated_attention(qb, kb, _heads(vb, DIL_HEADS))).astype(x.dtype)
    return jnp.concatenate([ya, yb], -1) @ w_out


def _gated_delta_rule(q, k, v, beta, g):
    b, h, s, dk = q.shape
    dv = v.shape[-1]
    c = GDN_CHUNK
    n = s // c
    ch = lambda t: t.reshape(b, h, n, c, *t.shape[3:])
    q, k, v, beta, g = ch(q), ch(k), ch(v), ch(beta), ch(g)
    gc = jnp.cumsum(g, -1)
    tri = jnp.tril(jnp.ones((c, c), bool))
    strict = jnp.tril(jnp.ones((c, c), bool), -1)
    diff = gc[..., :, None] - gc[..., None, :]
    decay = jnp.where(tri, jnp.exp(jnp.where(tri, diff, 0.0)), 0.0)
    kb = k * beta[..., None]
    a_mat = jnp.eye(c, dtype=F32) + jnp.where(strict, jnp.einsum('bhnid,bhnjd->bhnij', kb, k) * decay, 0.0)
    rhs = jnp.concatenate([v * beta[..., None], kb * jnp.exp(gc)[..., None]], -1)
    sol = lax.linalg.triangular_solve(a_mat, rhs, left_side=True, lower=True, unit_diagonal=True)
    u, w = sol[..., :dv], sol[..., dv:]
    attn = jnp.where(tri, jnp.einsum('bhnid,bhnjd->bhnij', q, k) * decay, 0.0)
    glast = gc[..., -1]
    k_dec = k * jnp.exp(glast[..., None] - gc)[..., None]
    q_dec = q * jnp.exp(gc)[..., None]

    def step(state, inp):
        qd_n, u_n, w_n, attn_n, kd_n, gl_n = inp
        v_new = u_n - jnp.einsum('bhcd,bhde->bhce', w_n, state)
        o_n = jnp.einsum('bhcd,bhde->bhce', qd_n, state) + jnp.einsum('bhij,bhje->bhie', attn_n, v_new)
        state = state * jnp.exp(gl_n)[..., None, None] + jnp.einsum('bhcd,bhce->bhde', kd_n, v_new)
        return state, o_n

    xs = tuple(jnp.moveaxis(t, 2, 0) for t in (q_dec, u, w, attn, k_dec, glast))
    _, o = lax.scan(step, jnp.zeros((b, h, dk, dv), F32), xs)
    return jnp.moveaxis(o, 0, 2).reshape(b, h, s, dv)


def _odd_mixer(x, w_in, conv_w, a_log, dt_bias, norm_w, w_out):
    hproj = x @ w_in
    qkv = jax.nn.silu(_causal_dwconv(hproj[..., :3 * GDN_W], conv_w))
    gate = hproj[..., 3 * GDN_W:4 * GDN_W]
    b_raw = hproj[..., 4 * GDN_W:4 * GDN_W + GDN_HEADS].astype(F32)
    a_raw = hproj[..., 4 * GDN_W + GDN_HEADS:].astype(F32)
    q = _l2norm(_heads(qkv[..., :GDN_W], GDN_HEADS).astype(F32)) * (GDN_DK ** -0.5)
    k = _l2norm(_heads(qkv[..., GDN_W:2 * GDN_W], GDN_HEADS).astype(F32))
    v = _heads(qkv[..., 2 * GDN_W:], GDN_HEADS).astype(F32)
    beta = jax.nn.sigmoid(b_raw).transpose(0, 2, 1)
    g = (-jnp.exp(a_log.astype(F32)) * jax.nn.softplus(a_raw + dt_bias.astype(F32))).transpose(0, 2, 1)
    o = _rms(_gated_delta_rule(q, k, v, beta, g)) * norm_w.astype(F32)
    o = (_merge(o) * jax.nn.silu(gate.astype(F32))).astype(x.dtype)
    return o @ w_out


def _conv_ffn(x, w_up, conv_w, conv_b, w_down):
    hproj = _causal_dwconv(x @ w_up, conv_w) + conv_b
    gate, val = hproj[..., :D_FF], hproj[..., D_FF:]
    return (jax.nn.silu(gate) * val) @ w_down


def setup_inputs(seed: int = 0) -> dict:
    key = jax.random.key(seed)
    ks = jax.random.split(key, 20)
    nrm = lambda kk, shape, scale: jax.random.normal(kk, shape, F32) * scale
    x = nrm(ks[0], (BATCH, SEQ, D_MODEL), 1.0)
    start = jax.random.randint(ks[1], (BATCH, 1), 0, 4096, dtype=jnp.int32)
    positions = start + jnp.arange(SEQ, dtype=jnp.int32)[None, :]
    ev_w_in = nrm(ks[2], (N_EVEN, D_MODEL, EV_IN), D_MODEL ** -0.5)
    ev_w_out = nrm(ks[3], (N_EVEN, EV_MIX, D_MODEL), EV_MIX ** -0.5 * DN_BETA)
    od_w_in = nrm(ks[4], (N_ODD, D_MODEL, OD_IN), D_MODEL ** -0.5)
    od_conv_w = nrm(ks[5], (N_ODD, GDN_CONV, 3 * GDN_W), GDN_CONV ** -0.5)
    od_a_log = jnp.log(jax.random.uniform(ks[6], (N_ODD, GDN_HEADS), F32, 1.0, 16.0))
    dt = jnp.exp(jax.random.uniform(ks[7], (N_ODD, GDN_HEADS), F32, math.log(1e-3), math.log(1e-1)))
    od_dt_bias = dt + jnp.log(-jnp.expm1(-dt))
    od_norm_w = 1.0 + nrm(ks[8], (N_ODD, GDN_DV), 0.02)
    od_w_out = nrm(ks[9], (N_ODD, GDN_W, D_MODEL), GDN_W ** -0.5 * DN_BETA)
    ffn_w_up = nrm(ks[10], (DEPTH, D_MODEL, 2 * D_FF), D_MODEL ** -0.5)
    ffn_conv_w = nrm(ks[11], (DEPTH, FFN_CONV, 2 * D_FF), FFN_CONV ** -0.5)
    ffn_conv_b = nrm(ks[12], (DEPTH, 2 * D_FF), 0.01)
    ffn_w_down = nrm(ks[13], (DEPTH, D_FF, D_MODEL), D_FF ** -0.5 * DN_BETA)
    ln1_g = 1.0 + nrm(ks[14], (DEPTH, D_MODEL), 0.02)
    ln1_b = nrm(ks[15], (DEPTH, D_MODEL), 0.01)
    ln2_g = 1.0 + nrm(ks[16], (DEPTH, D_MODEL), 0.02)
    ln2_b = nrm(ks[17], (DEPTH, D_MODEL), 0.01)
    return {"x": x, "positions": positions, "ev_w_in": ev_w_in, "ev_w_out": ev_w_out,
            "od_w_in": od_w_in, "od_conv_w": od_conv_w, "od_a_log": od_a_log,
            "od_dt_bias": od_dt_bias, "od_norm_w": od_norm_w, "od_w_out": od_w_out,
            "ffn_w_up": ffn_w_up, "ffn_conv_w": ffn_conv_w, "ffn_conv_b": ffn_conv_b,
            "ffn_w_down": ffn_w_down, "ln1_g": ln1_g, "ln1_b": ln1_b,
            "ln2_g": ln2_g, "ln2_b": ln2_b}


def reference(x, positions, ev_w_in, ev_w_out, od_w_in, od_conv_w, od_a_log, od_dt_bias,
              od_norm_w, od_w_out, ffn_w_up, ffn_conv_w, ffn_conv_b, ffn_w_down,
              ln1_g, ln1_b, ln2_g, ln2_b):
    for layer in range(DEPTH):
        j = layer // 2
        if layer % 2 == 0:
            mix = _even_mixer(x, positions, ev_w_in[j], ev_w_out[j])
        else:
            mix = _odd_mixer(x, od_w_in[j], od_conv_w[j], od_a_log[j], od_dt_bias[j],
                             od_norm_w[j], od_w_out[j])
        x = _layernorm(DN_ALPHA * x + mix, ln1_g[layer], ln1_b[layer])
        ffn = _conv_ffn(x, ffn_w_up[layer], ffn_conv_w[layer], ffn_conv_b[layer], ffn_w_down[layer])
        x = _layernorm(DN_ALPHA * x + ffn, ln2_g[layer], ln2_b[layer])
    return x
```

```python
import functools
import math

import jax
import jax.numpy as jnp
import numpy as np
from jax import lax
from jax.experimental import pallas as pl
from jax.experimental.pallas import tpu as pltpu

F32 = jnp.float32
BF16 = jnp.bfloat16

D_MODEL = 1024
DEPTH = 4
RET_HEADS, RET_DK, RET_DV, RET_CHUNK, RET_THETA = 4, 128, 256, 128, 10000.0
DIL_HEADS, DIL_HD, DIL_BLOCK = 8, 64, 128
DIL_DILATIONS = (1, 4, 16)
ROPE_THETA, ROPE_DIMS = 500000.0, DIL_HD // 4
GDN_HEADS, GDN_DK, GDN_DV, GDN_CHUNK, GDN_CONV = 8, 128, 128, 64, 4
D_FF, FFN_CONV = 2816, 3
DN_ALPHA = (2.0 * DEPTH) ** 0.25
EPS = 1e-5

RET_QK_W = RET_HEADS * RET_DK
RET_V_W = RET_HEADS * RET_DV
DIL_W = DIL_HEADS * DIL_HD
EV_IN = 2 * RET_QK_W + 2 * RET_V_W + 3 * DIL_W
GDN_W = GDN_HEADS * GDN_DK

LANES = 128
HALO = 16
VMEM_LIMIT = 56 * 1024 * 1024
HI = lax.Precision.HIGHEST


def _cparams(n_axes, vmem=VMEM_LIMIT):
    return pltpu.CompilerParams(dimension_semantics=("arbitrary",) * n_axes, vmem_limit_bytes=vmem)


def _dot(a, b):
    return jnp.dot(a, b, preferred_element_type=F32)


def _dot_nt(a, b):
    return lax.dot_general(a, b, (((1,), (1,)), ((), ())), preferred_element_type=F32)


def _dot_tn(a, b):
    return lax.dot_general(a, b, (((0,), (0,)), ((), ())), preferred_element_type=F32)


def _sigmoid(x):
    return 1.0 / (1.0 + jnp.exp(-x))


def _layernorm_rows(z, g, b):
    mu = jnp.mean(z, -1, keepdims=True)
    zc = z - mu
    var = jnp.mean(zc * zc, -1, keepdims=True)
    return zc * lax.rsqrt(var + EPS) * g + b


def _proj_kernel(x_ref, w_ref, o_ref, xb_ref):
    @pl.when(pl.program_id(1) == 0)
    def _():
        xb_ref[...] = x_ref[...].astype(BF16)

    o_ref[...] = _dot(xb_ref[...], w_ref[...]).astype(o_ref.dtype)


def _proj(x2d, w_bf16, tm, tn):
    t, k = x2d.shape
    n = w_bf16.shape[1]
    return pl.pallas_call(
        _proj_kernel,
        grid=(t // tm, n // tn),
        in_specs=[pl.BlockSpec((tm, k), lambda i, j: (i, 0)),
                  pl.BlockSpec((k, tn), lambda i, j: (0, j))],
        out_specs=pl.BlockSpec((tm, tn), lambda i, j: (i, j)),
        out_shape=jax.ShapeDtypeStruct((t, n), BF16),
        scratch_shapes=[pltpu.VMEM((tm, k), BF16)],
        compiler_params=_cparams(2),
        name="ev_in_proj",
    )(x2d, w_bf16)


def _trig_kernel(pos_ref, f_ref, o_ref, *, n_tab, cos_mask):
    rc = 256

    def body(i, carry):
        rows = pl.ds(pl.multiple_of(i * rc, rc), rc)
        pos = pos_ref[0, rows, :].astype(F32)
        for t in range(n_tab):
            ang = pos * f_ref[t:t + 1, :]
            o_ref[0, t, rows, :] = jnp.cos(ang) if cos_mask[t] else jnp.sin(ang)
        return carry

    lax.fori_loop(0, pos_ref.shape[1] // rc, body, 0)


def _trig_tables(pos3, freqs, cos_mask):
    b, s, _ = pos3.shape
    n_tab = len(cos_mask)
    fpad = jnp.zeros((8, LANES), F32).at[:n_tab].set(freqs)
    return pl.pallas_call(
        functools.partial(_trig_kernel, n_tab=n_tab, cos_mask=cos_mask),
        grid=(b,),
        in_specs=[pl.BlockSpec((1, s, 1), lambda i: (i, 0, 0)),
                  pl.BlockSpec((8, LANES), lambda i: (0, 0))],
        out_specs=pl.BlockSpec((1, n_tab, s, LANES), lambda i: (i, 0, 0, 0)),
        out_shape=jax.ShapeDtypeStruct((b, n_tab, s, LANES), F32),
        compiler_params=_cparams(1),
        name="rope_tables",
    )(pos3, fpad)


def _rope_freqs():
    half = RET_DK // 2
    inv = jnp.power(RET_THETA, -jnp.arange(half, dtype=F32) * 2.0 / RET_DK)
    ret = jnp.stack([jnp.concatenate([inv, inv]), jnp.concatenate([-inv, inv])])
    hh = ROPE_DIMS // 2
    invd = jnp.power(ROPE_THETA, -jnp.arange(hh, dtype=F32) * 2.0 / ROPE_DIMS)
    z = jnp.zeros((DIL_HD - ROPE_DIMS,), F32)
    zh = jnp.zeros((hh,), F32)
    f_cos = jnp.concatenate([invd, invd, z])
    f_a = jnp.concatenate([-invd, zh, z])
    f_b = jnp.concatenate([zh, invd, z])
    dil = jnp.stack([jnp.tile(f_cos, 2), jnp.tile(f_a, 2), jnp.tile(f_b, 2)])
    return ret, dil


def _ret_kernel(gch_ref, q_ref, k_ref, v_ref, g_ref, tab_ref, dm_ref, zt_ref, xi_ref, o_ref, r_ref, *, n_chunks):
    h = pl.program_id(1)
    gch = gch_ref[h]
    r_ref[...] = jnp.zeros_like(r_ref)
    c = RET_CHUNK

    def body(ci, carry):
        rows = pl.ds(pl.multiple_of(ci * c, c), c)
        cosr = tab_ref[0, 0, rows, :]
        sinr = tab_ref[0, 1, rows, :]
        q = q_ref[0, rows, :].astype(F32)
        k = k_ref[0, rows, :].astype(F32)
        qr = q * cosr + pltpu.roll(q, RET_DK // 2, 1) * sinr
        kr = (k * cosr + pltpu.roll(k, RET_DK // 2, 1) * sinr) * (RET_DK ** -0.5)
        v = v_ref[0, rows, :]
        r_prev = r_ref[...]
        scores = _dot_nt(qr.astype(BF16), kr.astype(BF16)) * dm_ref[0]
        o = _dot(scores.astype(BF16), v) + _dot((qr * xi_ref[0]).astype(BF16), r_prev.astype(BF16))
        r_ref[...] = r_prev * gch + _dot_tn((kr * zt_ref[0]).astype(BF16), v)
        y = o * lax.rsqrt(jnp.mean(o * o, -1, keepdims=True) + EPS)
        g = g_ref[0, rows, :].astype(F32)
        o_ref[0, rows, :] = (y * (g * _sigmoid(g))).astype(o_ref.dtype)
        return carry

    lax.fori_loop(0, n_chunks, body, 0)


def _ret_consts():
    hh = np.arange(RET_HEADS, dtype=np.float64)
    lg = np.log1p(-np.power(2.0, -5.0 - hh))
    idx = np.arange(RET_CHUNK, dtype=np.float64)
    rel = idx[:, None] - idx[None, :]
    dmat = np.where(rel >= 0, np.exp(np.maximum(rel, 0.0) * lg[:, None, None]), 0.0)
    zeta = np.exp((RET_CHUNK - 1 - idx) * lg[:, None])
    xi = np.exp((idx + 1) * lg[:, None])
    gch = np.exp(RET_CHUNK * lg)
    bc = lambda a: np.broadcast_to(a[:, :, None], (RET_HEADS, RET_CHUNK, RET_DK))
    return (jnp.asarray(gch, F32), jnp.asarray(dmat, F32), jnp.asarray(bc(zeta), F32), jnp.asarray(bc(xi), F32))


def _retention(hproj, tab_ret, consts):
    b, s, _ = hproj.shape
    gch, dmat, zeta, xi = consts
    qk_blocks = RET_QK_W // RET_DK
    v_blocks = RET_V_W // RET_DV
    v_off = 2 * RET_QK_W // RET_DV
    per_head = pl.BlockSpec((1, RET_CHUNK, RET_DK), lambda i, j: (j, 0, 0))
    return pl.pallas_call(
        functools.partial(_ret_kernel, n_chunks=s // RET_CHUNK),
        grid=(b, RET_HEADS),
        in_specs=[pl.BlockSpec(memory_space=pltpu.SMEM),
                  pl.BlockSpec((1, s, RET_DK), lambda i, j: (i, 0, j)),
                  pl.BlockSpec((1, s, RET_DK), lambda i, j: (i, 0, qk_blocks + j)),
                  pl.BlockSpec((1, s, RET_DV), lambda i, j: (i, 0, v_off + j)),
                  pl.BlockSpec((1, s, RET_DV), lambda i, j: (i, 0, v_off + v_blocks + j)),
                  pl.BlockSpec((1, 2, s, LANES), lambda i, j: (i, 0, 0, 0)),
                  per_head, per_head, per_head],
        out_specs=pl.BlockSpec((1, s, RET_DV), lambda i, j: (i, 0, j)),
        out_shape=jax.ShapeDtypeStruct((b, s, RET_V_W), BF16),
        scratch_shapes=[pltpu.VMEM((RET_DK, RET_DV), F32)],
        compiler_params=_cparams(2),
        name="retention",
    )(gch, hproj, hproj, hproj, hproj, tab_ret, dmat, zeta, xi)


def _dil_kernel(q_ref, k_ref, v_ref, tab_ref, o_ref, qs, ks, vs, acc, ms, ls, *, seq):
    rc = 256
    lane = lax.broadcasted_iota(jnp.int32, (DIL_BLOCK, LANES), 1)
    first_head = lane < DIL_HD

    def rot_body(i, carry):
        rows = pl.ds(pl.multiple_of(i * rc, rc), rc)
        cosd = tab_ref[0, 0, rows, :]
        sina = tab_ref[0, 1, rows, :]
        sinb = tab_ref[0, 2, rows, :]
        hh = ROPE_DIMS // 2

        def rot(x):
            return x * cosd + pltpu.roll(x, LANES - hh, 1) * sina + pltpu.roll(x, hh, 1) * sinb

        qs[rows, :] = rot(q_ref[0, rows, :].astype(F32)) * (DIL_HD ** -0.5)
        ks[rows, :] = rot(k_ref[0, rows, :].astype(F32))
        vs[rows, :] = v_ref[0, rows, :].astype(F32)
        return carry

    lax.fori_loop(0, seq // rc, rot_body, 0)

    def block(g, d, qstart, kstart, kb_rows, qsub, ksub):
        if d == 1:
            qsl = pl.ds(qstart, DIL_BLOCK)
            ksl = pl.ds(kstart, kb_rows)
        else:
            qsl = pl.ds(qstart, DIL_BLOCK, stride=d)
            ksl = pl.ds(kstart, kb_rows, stride=d)
        qb = qs[qsl, :]
        kb = ks[ksl, :].astype(BF16)
        vb = vs[ksl, :].astype(BF16)
        qi = lax.broadcasted_iota(jnp.int32, (DIL_BLOCK, kb_rows), 0) + qsub
        kj = lax.broadcasted_iota(jnp.int32, (DIL_BLOCK, kb_rows), 1) + ksub
        dist = qi - kj
        valid = (dist >= 0) & (dist <= DIL_BLOCK)
        outs = []
        for hh in range(2):
            sel = first_head if hh == 0 else jnp.logical_not(first_head)
            qh = jnp.where(sel, qb, 0.0).astype(BF16)
            sc = jnp.where(valid, _dot_nt(qh, kb), -1e30)
            m = jnp.max(sc, -1, keepdims=True)
            p = jnp.exp(sc - m)
            l = jnp.sum(p, -1, keepdims=True)
            outs.append((_dot(p.astype(BF16), vb), m, l))
        acc[g, qsl, :] = jnp.where(first_head, outs[0][0], outs[1][0])
        ms[g, qsl, :] = jnp.where(first_head, outs[0][1], outs[1][1])
        ls[g, qsl, :] = jnp.where(first_head, outs[0][2], outs[1][2])

    for g, d in enumerate(DIL_DILATIONS):
        sub_len = seq // d
        nb = sub_len // DIL_BLOCK
        if nb == 1:
            def body(r, carry, g=g, d=d):
                block(g, d, r, r, DIL_BLOCK, 0, 0)
                return carry
            lax.fori_loop(0, d, body, 0)
        else:
            def body(i, carry, g=g, d=d, nb=nb):
                r = i & (d - 1)
                n = i >> int(math.log2(d))
                kn = jnp.maximum(n - 1, 0)
                block(g, d, r + n * (DIL_BLOCK * d), r + kn * (DIL_BLOCK * d), 2 * DIL_BLOCK,
                      n * DIL_BLOCK, kn * DIL_BLOCK)
                return carry
            lax.fori_loop(0, d * nb, body, 0)

    def comb_body(i, carry):
        rows = pl.ds(pl.multiple_of(i * rc, rc), rc)
        m0, m1, m2 = ms[0, rows, :], ms[1, rows, :], ms[2, rows, :]
        mmax = jnp.maximum(jnp.maximum(m0, m1), m2)
        e0, e1, e2 = jnp.exp(m0 - mmax), jnp.exp(m1 - mmax), jnp.exp(m2 - mmax)
        num = e0 * acc[0, rows, :] + e1 * acc[1, rows, :] + e2 * acc[2, rows, :]
        den = e0 * ls[0, rows, :] + e1 * ls[1, rows, :] + e2 * ls[2, rows, :]
        o_ref[0, rows, :] = (num / den).astype(o_ref.dtype)
        return carry

    lax.fori_loop(0, seq // rc, comb_body, 0)


def _dilated(hproj, tab_dil):
    b, s, _ = hproj.shape
    base = (2 * RET_QK_W + 2 * RET_V_W) // LANES
    pairs = DIL_W // LANES
    blk = lambda off: pl.BlockSpec((1, s, LANES), lambda i, j: (i, 0, base + off + j))
    n_g = len(DIL_DILATIONS)
    return pl.pallas_call(
        functools.partial(_dil_kernel, seq=s),
        grid=(b, pairs),
        in_specs=[blk(0), blk(pairs), blk(2 * pairs),
                  pl.BlockSpec((1, 3, s, LANES), lambda i, j: (i, 0, 0, 0))],
        out_specs=pl.BlockSpec((1, s, LANES), lambda i, j: (i, 0, j)),
        out_shape=jax.ShapeDtypeStruct((b, s, DIL_W), BF16),
        scratch_shapes=[pltpu.VMEM((s, LANES), F32)] * 3 + [pltpu.VMEM((n_g, s, LANES), F32)] * 3,
        compiler_params=_cparams(2),
        name="dilated_attention",
    )(hproj, hproj, hproj, tab_dil)


def _out_ln_kernel(*refs, n_in):
    x_ref = refs[0]
    ys = refs[1:1 + n_in]
    ws = refs[1 + n_in:1 + 2 * n_in]
    g_ref, b_ref, o_ref = refs[1 + 2 * n_in:]
    z = DN_ALPHA * x_ref[...]
    for y_ref, w_ref in zip(ys, ws):
        z = z + _dot(y_ref[...], w_ref[...])
    o_ref[...] = _layernorm_rows(z, g_ref[...], b_ref[...])


def _out_ln(x2d, ys, ws, g, b, tm):
    t, d = x2d.shape
    n_in = len(ys)
    row = lambda w: pl.BlockSpec((tm, w), lambda i: (i, 0))
    full = lambda a: pl.BlockSpec(a.shape, lambda i: (0, 0))
    return pl.pallas_call(
        functools.partial(_out_ln_kernel, n_in=n_in),
        grid=(t // tm,),
        in_specs=[row(d)] + [row(y.shape[1]) for y in ys] + [full(w) for w in ws] + [full(g), full(b)],
        out_specs=row(d),
        out_shape=jax.ShapeDtypeStruct((t, d), F32),
        compiler_params=_cparams(1),
        name="out_proj_ln",
    )(x2d, *ys, *ws, g, b)


def _conv_taps(ub, taps):
    k_w = len(taps)
    out = ub[8:] * taps[k_w - 1]
    for back in range(1, k_w):
        out = out + pltpu.roll(ub, back, 0)[8:] * taps[k_w - 1 - back]
    return out


def _fill_xb(xb, x_ref, xh_ref, i):
    xb[HALO:, :] = x_ref[0].astype(BF16)
    xb[0:HALO, :] = jnp.where(i > 0, xh_ref[0], 0.0).astype(BF16)


def _ffn_kernel(x_ref, xh_ref, wup_ref, cw_ref, wdn_ref, g_ref, b_ref, o_ref, xb, us, hs, acc, *, tm, fc, nc, rb):
    i = pl.program_id(1)
    c = pl.program_id(2)

    @pl.when(c == 0)
    def _():
        _fill_xb(xb, x_ref, xh_ref, i)

    us[...] = _dot(xb[...], wup_ref[0])

    def blk(r, carry):
        r0 = pl.multiple_of(r * rb, rb)
        rows_in = pl.ds(r0 + HALO - 8, rb + 8)
        for p in range(fc // LANES):
            cg = slice(p * LANES, (p + 1) * LANES)
            cv = slice(fc + p * LANES, fc + (p + 1) * LANES)
            tg = [cw_ref[0, j:j + 1, cg] for j in range(FFN_CONV)]
            tv = [cw_ref[0, j:j + 1, cv] for j in range(FFN_CONV)]
            gate = _conv_taps(us[rows_in, cg], tg) + cw_ref[0, FFN_CONV:FFN_CONV + 1, cg]
            val = _conv_taps(us[rows_in, cv], tv) + cw_ref[0, FFN_CONV:FFN_CONV + 1, cv]
            hs[pl.ds(r0, rb), cg] = (gate * _sigmoid(gate) * val).astype(BF16)
        return carry

    lax.fori_loop(0, tm // rb, blk, 0)
    d = _dot(hs[...], wdn_ref[0])

    @pl.when(c == 0)
    def _():
        acc[...] = d

    @pl.when(c > 0)
    def _():
        acc[...] += d

    @pl.when(c == nc - 1)
    def _():
        o_ref[0] = _layernorm_rows(DN_ALPHA * x_ref[0] + acc[...], g_ref[...], b_ref[...])


def _ffn(x3, wup_r, cw_r, wdn_r, g, b, tm, fc):
    bsz, s, d = x3.shape
    nc = D_FF // fc
    hb = tm // HALO
    return pl.pallas_call(
        functools.partial(_ffn_kernel, tm=tm, fc=fc, nc=nc, rb=128),
        grid=(bsz, s // tm, nc),
        in_specs=[pl.BlockSpec((1, tm, d), lambda bi, i, c: (bi, i, 0)),
                  pl.BlockSpec((1, HALO, d), lambda bi, i, c: (bi, jnp.maximum(i * hb - 1, 0), 0)),
                  pl.BlockSpec((1, d, 2 * fc), lambda bi, i, c: (c, 0, 0)),
                  pl.BlockSpec((1, 8, 2 * fc), lambda bi, i, c: (c, 0, 0)),
                  pl.BlockSpec((1, fc, d), lambda bi, i, c: (c, 0, 0)),
                  pl.BlockSpec((1, d), lambda bi, i, c: (0, 0)),
                  pl.BlockSpec((1, d), lambda bi, i, c: (0, 0))],
        out_specs=pl.BlockSpec((1, tm, d), lambda bi, i, c: (bi, i, 0)),
        out_shape=jax.ShapeDtypeStruct((bsz, s, d), F32),
        scratch_shapes=[pltpu.VMEM((HALO + tm, d), BF16),
                        pltpu.VMEM((HALO + tm, 2 * fc), F32),
                        pltpu.VMEM((tm, fc), BF16),
                        pltpu.VMEM((tm, d), F32)],
        compiler_params=_cparams(3),
        name="conv_ffn_ln",
    )(x3, x3, wup_r, cw_r, wdn_r, g, b)


def _ffn_weights(w_up, conv_w, conv_b, w_down, fc):
    nc = D_FF // fc
    d = w_up.shape[0]
    arr = lambda a: a.reshape(a.shape[0], 2, nc, fc).transpose(2, 0, 1, 3).reshape(nc, a.shape[0], 2 * fc)
    wup_r = arr(w_up).astype(BF16)
    cw = jnp.concatenate([conv_w, conv_b[None, :], jnp.zeros((8 - FFN_CONV - 1, 2 * D_FF), F32)], 0)
    cw_r = arr(cw)
    wdn_r = w_down.reshape(nc, fc, d).astype(BF16)
    return wup_r, cw_r, wdn_r


def _odd_in_kernel(x_ref, xh_ref, w_ref, cw_ref, ws_ref, o_ref, os_ref, xb, us, *, tm, tn, n_conv, rb):
    i = pl.program_id(1)
    c = pl.program_id(2)

    @pl.when(c == 0)
    def _():
        _fill_xb(xb, x_ref, xh_ref, i)
        os_ref[0] = _dot(xb[HALO:, :], ws_ref[...])

    us[...] = _dot(xb[...], w_ref[...])

    @pl.when(c < n_conv)
    def _():
        def blk(r, carry):
            r0 = pl.multiple_of(r * rb, rb)
            rows_in = pl.ds(r0 + HALO - 8, rb + 8)
            for p in range(tn // LANES):
                cs = slice(p * LANES, (p + 1) * LANES)
                taps = [cw_ref[0, j:j + 1, cs] for j in range(GDN_CONV)]
                y = _conv_taps(us[rows_in, cs], taps)
                o_ref[0, pl.ds(r0, rb), cs] = (y * _sigmoid(y)).astype(o_ref.dtype)
            return carry

        lax.fori_loop(0, tm // rb, blk, 0)

    @pl.when(c >= n_conv)
    def _():
        o_ref[0] = us[HALO:, :].astype(o_ref.dtype)


def _odd_in(x3, w_main, cw_r, w_small, tm, tn):
    bsz, s, d = x3.shape
    n = w_main.shape[1]
    n_conv = cw_r.shape[0]
    hb = tm // HALO
    return pl.pallas_call(
        functools.partial(_odd_in_kernel, tm=tm, tn=tn, n_conv=n_conv, rb=128),
        grid=(bsz, s // tm, n // tn),
        in_specs=[pl.BlockSpec((1, tm, d), lambda bi, i, c: (bi, i, 0)),
                  pl.BlockSpec((1, HALO, d), lambda bi, i, c: (bi, jnp.maximum(i * hb - 1, 0), 0)),
                  pl.BlockSpec((d, tn), lambda bi, i, c: (0, c)),
                  pl.BlockSpec((1, 8, tn), lambda bi, i, c: (jnp.minimum(c, n_conv - 1), 0, 0)),
                  pl.BlockSpec((d, LANES), lambda bi, i, c: (0, 0))],
        out_specs=[pl.BlockSpec((1, tm, tn), lambda bi, i, c: (bi, i, c)),
                   pl.BlockSpec((1, tm, LANES), lambda bi, i, c: (bi, i, 0))],
        out_shape=[jax.ShapeDtypeStruct((bsz, s, n), BF16),
                   jax.ShapeDtypeStruct((bsz, s, LANES), F32)],
        scratch_shapes=[pltpu.VMEM((HALO + tm, d), BF16),
                        pltpu.VMEM((HALO + tm, tn), F32)],
        compiler_params=_cparams(3),
        name="od_in_proj_conv",
    )(x3, x3, w_main, cw_r, w_small)


def _gdn_kernel(alog_ref, dtb_ref, q_ref, k_ref, v_ref, gt_ref, sm_ref, nw_ref, o_ref,
                u_s, w_s, qd_s, kd_s, at_s, eg_s, st_s, *, n_chunks):
    h = pl.program_id(1)
    c = GDN_CHUNK
    a_neg = -jnp.exp(alog_ref[h])
    dtb = dtb_ref[h]
    ri = lax.broadcasted_iota(jnp.int32, (c, c), 0)
    ci = lax.broadcasted_iota(jnp.int32, (c, c), 1)
    tri = ri >= ci
    strict = ri > ci
    ltri = jnp.where(tri, 1.0, 0.0).astype(F32)
    eye = jnp.where(ri == ci, 1.0, 0.0).astype(F32)
    lane = lax.broadcasted_iota(jnp.int32, (c, LANES), 1)

    def prep(i, carry):
        rows = pl.ds(pl.multiple_of(i * c, c), c)
        q = q_ref[0, rows, :].astype(F32)
        k = k_ref[0, rows, :].astype(F32)
        v = v_ref[0, rows, :].astype(F32)
        qn = q * lax.rsqrt(jnp.sum(q * q, -1, keepdims=True) + 1e-6) * (GDN_DK ** -0.5)
        kn = k * lax.rsqrt(jnp.sum(k * k, -1, keepdims=True) + 1e-6)
        sm = sm_ref[0, rows, :]
        b_raw = jnp.sum(jnp.where(lane == h, sm, 0.0), -1, keepdims=True)
        a_raw = jnp.sum(jnp.where(lane == GDN_HEADS + h, sm, 0.0), -1, keepdims=True)
        beta = _sigmoid(b_raw)
        sp_in = a_raw + dtb
        g = a_neg * (jnp.maximum(sp_in, 0.0) + jnp.log1p(jnp.exp(-jnp.abs(sp_in))))
        gb = jnp.broadcast_to(g, (c, LANES))
        gc = jnp.dot(ltri, gb, precision=HI, preferred_element_type=F32)
        gc_row = lax.dot_general(gb[:, :c], jnp.where(ri <= ci, 1.0, 0.0).astype(F32),
                                 (((0,), (0,)), ((), ())), precision=HI, preferred_element_type=F32)
        diff = gc[:, :c] - gc_row
        decay = jnp.where(tri, jnp.exp(jnp.where(tri, diff, 0.0)), 0.0)
        kb = kn * beta
        a_mat = jnp.where(strict, _dot_nt(kb.astype(BF16), kn.astype(BF16)) * decay, 0.0)
        pw = -a_mat
        t_inv = eye + pw
        for _ in range(int(math.log2(c)) - 1):
            pw = jnp.dot(pw, pw, precision=HI, preferred_element_type=F32)
            t_inv = t_inv + jnp.dot(t_inv, pw, precision=HI, preferred_element_type=F32)
        gc1 = gc[:, 0:1]
        egc = jnp.exp(gc1)
        u_s[rows, :] = jnp.dot(t_inv, v * beta, precision=HI, preferred_element_type=F32)
        w_s[rows, :] = jnp.dot(t_inv, kb * egc, precision=HI, preferred_element_type=F32)
        at_s[rows, :] = jnp.where(tri, _dot_nt(qn.astype(BF16), kn.astype(BF16)) * decay, 0.0)
        gl = gc[c - 1:c, :]
        kd_s[rows, :] = kn * jnp.exp(gl - gc)
        qd_s[rows, :] = qn * egc
        eg_s[rows, :] = jnp.broadcast_to(jnp.exp(gl), (c, LANES))
        return carry

    lax.fori_loop(0, n_chunks, prep, 0)
    st_s[...] = jnp.zeros_like(st_s)

    def scan(i, carry):
        rows = pl.ds(pl.multiple_of(i * c, c), c)
        state = st_s[...]
        sb = state.astype(BF16)
        v_new = u_s[rows, :] - _dot(w_s[rows, :].astype(BF16), sb)
        vb = v_new.astype(BF16)
        o = _dot(qd_s[rows, :].astype(BF16), sb) + _dot(at_s[rows, :].astype(BF16), vb)
        st_s[...] = state * eg_s[pl.ds(pl.multiple_of(i * c, c), 1), :] + _dot_tn(kd_s[rows, :].astype(BF16), vb)
        y = o * lax.rsqrt(jnp.mean(o * o, -1, keepdims=True) + EPS) * nw_ref[...]
        gt = gt_ref[0, rows, :].astype(F32)
        o_ref[0, rows, :] = (y * (gt * _sigmoid(gt))).astype(o_ref.dtype)
        return carry

    lax.fori_loop(0, n_chunks, scan, 0)


def _gdn(qkvg, small, a_log, dt_bias, norm_w):
    b, s, _ = qkvg.shape
    blk = lambda off: pl.BlockSpec((1, s, GDN_DK), lambda i, j: (i, 0, off + j))
    smem = pl.BlockSpec(memory_space=pltpu.SMEM)
    return pl.pallas_call(
        functools.partial(_gdn_kernel, n_chunks=s // GDN_CHUNK),
        grid=(b, GDN_HEADS),
        in_specs=[smem, smem, blk(0), blk(GDN_HEADS), blk(2 * GDN_HEADS), blk(3 * GDN_HEADS),
                  pl.BlockSpec((1, s, LANES), lambda i, j: (i, 0, 0)),
                  pl.BlockSpec((1, GDN_DV), lambda i, j: (0, 0))],
        out_specs=pl.BlockSpec((1, s, GDN_DV), lambda i, j: (i, 0, j)),
        out_shape=jax.ShapeDtypeStruct((b, s, GDN_W), BF16),
        scratch_shapes=[pltpu.VMEM((s, LANES), F32)] * 4 + [pltpu.VMEM((s, GDN_CHUNK), F32),
                                                             pltpu.VMEM((s, LANES), F32),
                                                             pltpu.VMEM((GDN_DK, GDN_DV), F32)],
        compiler_params=_cparams(2),
        name="gated_delta_rule",
    )(a_log, dt_bias, qkvg, qkvg, qkvg, qkvg, small, norm_w.reshape(1, GDN_DV))


def kernel(x, positions, ev_w_in, ev_w_out, od_w_in, od_conv_w, od_a_log, od_dt_bias, od_norm_w, od_w_out,
           ffn_w_up, ffn_conv_w, ffn_conv_b, ffn_w_down, ln1_g, ln1_b, ln2_g, ln2_b):
    b, s, d = x.shape
    t = b * s
    ret_f, dil_f = _rope_freqs()
    pos3 = positions.reshape(b, s, 1)
    tab_ret = _trig_tables(pos3, ret_f, (True, False))
    tab_dil = _trig_tables(pos3, dil_f, (True, False, False))
    ret_consts = _ret_consts()
    fc = D_FF // 2
    row = lambda a: a.reshape(1, d)

    for layer in range(DEPTH):
        j = layer // 2
        x2 = x.reshape(t, d)
        if layer % 2 == 0:
            hproj = _proj(x2, ev_w_in[j].astype(BF16), 1024, EV_IN // 4).reshape(b, s, EV_IN)
            ya = _retention(hproj, tab_ret, ret_consts)
            yb = _dilated(hproj, tab_dil)
            w_out = ev_w_out[j].astype(BF16)
            x2 = _out_ln(x2, [ya.reshape(t, RET_V_W), yb.reshape(t, DIL_W)],
                         [w_out[:RET_V_W], w_out[RET_V_W:]], row(ln1_g[layer]), row(ln1_b[layer]), 512)
        else:
            w_in = od_w_in[j]
            w_main = w_in[:, :4 * GDN_W].astype(BF16)
            w_small = jnp.pad(w_in[:, 4 * GDN_W:], ((0, 0), (0, LANES - 2 * GDN_HEADS))).astype(BF16)
            cw = jnp.pad(od_conv_w[j], ((0, 8 - GDN_CONV), (0, 0)))
            cw_r = cw.reshape(8, 3, GDN_W).transpose(1, 0, 2)
            qkvg, small = _odd_in(x, w_main, cw_r, w_small, 512, GDN_W)
            yc = _gdn(qkvg, small, od_a_log[j], od_dt_bias[j], od_norm_w[j])
            x2 = _out_ln(x2, [yc.reshape(t, GDN_W)], [od_w_out[j].astype(BF16)],
                         row(ln1_g[layer]), row(ln1_b[layer]), 512)
        x = x2.reshape(b, s, d)
        wup_r, cw_r, wdn_r = _ffn_weights(ffn_w_up[layer], ffn_conv_w[layer], ffn_conv_b[layer],
                                          ffn_w_down[layer], fc)
        x = _ffn(x, wup_r, cw_r, wdn_r, row(ln2_g[layer]), row(ln2_b[layer]), 512, fc)
    return x
```

```python
import functools
import math

import jax
import jax.numpy as jnp
import numpy as np
from jax import lax
from jax.experimental import pallas as pl
from jax.experimental.pallas import tpu as pltpu

F32 = jnp.float32
BF16 = jnp.bfloat16

D_MODEL = 1024
DEPTH = 4
RET_HEADS, RET_DK, RET_DV, RET_CHUNK, RET_THETA = 4, 128, 256, 128, 10000.0
DIL_HEADS, DIL_HD, DIL_BLOCK = 8, 64, 128
DIL_DILATIONS = (1, 4, 16)
ROPE_THETA, ROPE_DIMS = 500000.0, DIL_HD // 4
GDN_HEADS, GDN_DK, GDN_DV, GDN_CHUNK, GDN_CONV = 8, 128, 128, 64, 4
D_FF, FFN_CONV = 2816, 3
DN_ALPHA = (2.0 * DEPTH) ** 0.25
EPS = 1e-5

RET_QK_W = RET_HEADS * RET_DK
RET_V_W = RET_HEADS * RET_DV
DIL_W = DIL_HEADS * DIL_HD
EV_IN = 2 * RET_QK_W + 2 * RET_V_W + 3 * DIL_W
GDN_W = GDN_HEADS * GDN_DK

LANES = 128
HALO = 16
VMEM_LIMIT = 56 * 1024 * 1024
HI = lax.Precision.HIGHEST


def _cparams(n_axes, vmem=VMEM_LIMIT):
    return pltpu.CompilerParams(dimension_semantics=("arbitrary",) * n_axes, vmem_limit_bytes=vmem)


def _dot(a, b):
    return jnp.dot(a, b, preferred_element_type=F32)


def _dot_nt(a, b):
    return lax.dot_general(a, b, (((1,), (1,)), ((), ())), preferred_element_type=F32)


def _dot_tn(a, b):
    return lax.dot_general(a, b, (((0,), (0,)), ((), ())), preferred_element_type=F32)


def _sigmoid(x):
    return 1.0 / (1.0 + jnp.exp(-x))


def _layernorm_rows(z, g, b):
    mu = jnp.mean(z, -1, keepdims=True)
    zc = z - mu
    var = jnp.mean(zc * zc, -1, keepdims=True)
    return zc * lax.rsqrt(var + EPS) * g + b


def _proj_kernel(x_ref, w_ref, o_ref, xb_ref):
    @pl.when(pl.program_id(1) == 0)
    def _():
        xb_ref[...] = x_ref[...].astype(BF16)

    o_ref[...] = _dot(xb_ref[...], w_ref[...]).astype(o_ref.dtype)


def _proj(x2d, w_bf16, tm, tn):
    t, k = x2d.shape
    n = w_bf16.shape[1]
    return pl.pallas_call(
        _proj_kernel,
        grid=(t // tm, n // tn),
        in_specs=[pl.BlockSpec((tm, k), lambda i, j: (i, 0)),
                  pl.BlockSpec((k, tn), lambda i, j: (0, j))],
        out_specs=pl.BlockSpec((tm, tn), lambda i, j: (i, j)),
        out_shape=jax.ShapeDtypeStruct((t, n), BF16),
        scratch_shapes=[pltpu.VMEM((tm, k), BF16)],
        compiler_params=_cparams(2),
        name="ev_in_proj",
    )(x2d, w_bf16)


def _trig_kernel(pos_ref, f_ref, o_ref, *, n_tab, cos_mask):
    rc = 256

    def body(i, carry):
        rows = pl.ds(pl.multiple_of(i * rc, rc), rc)
        pos = pos_ref[0, rows, :].astype(F32)
        for t in range(n_tab):
            ang = pos * f_ref[t:t + 1, :]
            o_ref[0, t, rows, :] = jnp.cos(ang) if cos_mask[t] else jnp.sin(ang)
        return carry

    lax.fori_loop(0, pos_ref.shape[1] // rc, body, 0)


def _trig_tables(pos3, freqs, cos_mask):
    b, s, _ = pos3.shape
    n_tab = len(cos_mask)
    fpad = jnp.zeros((8, LANES), F32).at[:n_tab].set(freqs)
    return pl.pallas_call(
        functools.partial(_trig_kernel, n_tab=n_tab, cos_mask=cos_mask),
        grid=(b,),
        in_specs=[pl.BlockSpec((1, s, 1), lambda i: (i, 0, 0)),
                  pl.BlockSpec((8, LANES), lambda i: (0, 0))],
        out_specs=pl.BlockSpec((1, n_tab, s, LANES), lambda i: (i, 0, 0, 0)),
        out_shape=jax.ShapeDtypeStruct((b, n_tab, s, LANES), F32),
        compiler_params=_cparams(1),
        name="rope_tables",
    )(pos3, fpad)


def _rope_freqs():
    half = RET_DK // 2
    inv = jnp.power(RET_THETA, -jnp.arange(half, dtype=F32) * 2.0 / RET_DK)
    ret = jnp.stack([jnp.concatenate([inv, inv]), jnp.concatenate([-inv, inv])])
    hh = ROPE_DIMS // 2
    invd = jnp.power(ROPE_THETA, -jnp.arange(hh, dtype=F32) * 2.0 / ROPE_DIMS)
    z = jnp.zeros((DIL_HD - ROPE_DIMS,), F32)
    zh = jnp.zeros((hh,), F32)
    f_cos = jnp.concatenate([invd, invd, z])
    f_a = jnp.concatenate([-invd, zh, z])
    f_b = jnp.concatenate([zh, invd, z])
    dil = jnp.stack([jnp.tile(f_cos, 2), jnp.tile(f_a, 2), jnp.tile(f_b, 2)])
    return ret, dil


def _ret_kernel(gch_ref, q_ref, k_ref, v_ref, g_ref, tab_ref, dm_ref, zt_ref, xi_ref, o_ref, r_ref, *, n_chunks):
    h = pl.program_id(1)
    gch = gch_ref[h]
    r_ref[...] = jnp.zeros_like(r_ref)
    c = RET_CHUNK

    def body(ci, carry):
        rows = pl.ds(pl.multiple_of(ci * c, c), c)
        cosr = tab_ref[0, 0, rows, :]
        sinr = tab_ref[0, 1, rows, :]
        q = q_ref[0, rows, :].astype(F32)
        k = k_ref[0, rows, :].astype(F32)
        qr = q * cosr + pltpu.roll(q, RET_DK // 2, 1) * sinr
        kr = (k * cosr + pltpu.roll(k, RET_DK // 2, 1) * sinr) * (RET_DK ** -0.5)
        v = v_ref[0, rows, :]
        r_prev = r_ref[...]
        scores = _dot_nt(qr.astype(BF16), kr.astype(BF16)) * dm_ref[0]
        o = _dot(scores.astype(BF16), v) + _dot((qr * xi_ref[0]).astype(BF16), r_prev.astype(BF16))
        r_ref[...] = r_prev * gch + _dot_tn((kr * zt_ref[0]).astype(BF16), v)
        y = o * lax.rsqrt(jnp.mean(o * o, -1, keepdims=True) + EPS)
        g = g_ref[0, rows, :].astype(F32)
        o_ref[0, rows, :] = (y * (g * _sigmoid(g))).astype(o_ref.dtype)
        return carry

    lax.fori_loop(0, n_chunks, body, 0)


def _ret_consts():
    hh = np.arange(RET_HEADS, dtype=np.float64)
    lg = np.log1p(-np.power(2.0, -5.0 - hh))
    idx = np.arange(RET_CHUNK, dtype=np.float64)
    rel = idx[:, None] - idx[None, :]
    dmat = np.where(rel >= 0, np.exp(np.maximum(rel, 0.0) * lg[:, None, None]), 0.0)
    zeta = np.exp((RET_CHUNK - 1 - idx) * lg[:, None])
    xi = np.exp((idx + 1) * lg[:, None])
    gch = np.exp(RET_CHUNK * lg)
    bc = lambda a: np.broadcast_to(a[:, :, None], (RET_HEADS, RET_CHUNK, RET_DK))
    return (jnp.asarray(gch, F32), jnp.asarray(dmat, F32), jnp.asarray(bc(zeta), F32), jnp.asarray(bc(xi), F32))


def _retention(hproj, tab_ret, consts):
    b, s, _ = hproj.shape
    gch, dmat, zeta, xi = consts
    qk_blocks = RET_QK_W // RET_DK
    v_blocks = RET_V_W // RET_DV
    v_off = 2 * RET_QK_W // RET_DV
    per_head = pl.BlockSpec((1, RET_CHUNK, RET_DK), lambda i, j: (j, 0, 0))
    return pl.pallas_call(
        functools.partial(_ret_kernel, n_chunks=s // RET_CHUNK),
        grid=(b, RET_HEADS),
        in_specs=[pl.BlockSpec(memory_space=pltpu.SMEM),
                  pl.BlockSpec((1, s, RET_DK), lambda i, j: (i, 0, j)),
                  pl.BlockSpec((1, s, RET_DK), lambda i, j: (i, 0, qk_blocks + j)),
                  pl.BlockSpec((1, s, RET_DV), lambda i, j: (i, 0, v_off + j)),
                  pl.BlockSpec((1, s, RET_DV), lambda i, j: (i, 0, v_off + v_blocks + j)),
                  pl.BlockSpec((1, 2, s, LANES), lambda i, j: (i, 0, 0, 0)),
                  per_head, per_head, per_head],
        out_specs=pl.BlockSpec((1, s, RET_DV), lambda i, j: (i, 0, j)),
        out_shape=jax.ShapeDtypeStruct((b, s, RET_V_W), BF16),
        scratch_shapes=[pltpu.VMEM((RET_DK, RET_DV), F32)],
        compiler_params=_cparams(2),
        name="retention",
    )(gch, hproj, hproj, hproj, hproj, tab_ret, dmat, zeta, xi)


def _dil_kernel(q_ref, k_ref, v_ref, tab_ref, o_ref, qs, ks, vs, acc, ms, ls, *, seq):
    rc = 256
    lane = lax.broadcasted_iota(jnp.int32, (DIL_BLOCK, LANES), 1)
    first_head = lane < DIL_HD

    def rot_body(i, carry):
        rows = pl.ds(pl.multiple_of(i * rc, rc), rc)
        cosd = tab_ref[0, 0, rows, :]
        sina = tab_ref[0, 1, rows, :]
        sinb = tab_ref[0, 2, rows, :]
        hh = ROPE_DIMS // 2

        def rot(x):
            return x * cosd + pltpu.roll(x, LANES - hh, 1) * sina + pltpu.roll(x, hh, 1) * sinb

        qs[rows, :] = rot(q_ref[0, rows, :].astype(F32)) * (DIL_HD ** -0.5)
        ks[rows, :] = rot(k_ref[0, rows, :].astype(F32))
        vs[rows, :] = v_ref[0, rows, :].astype(F32)
        return carry

    lax.fori_loop(0, seq // rc, rot_body, 0)

    def block(g, d, qstart, kstart, kb_rows, qsub, ksub):
        if d == 1:
            qsl = pl.ds(qstart, DIL_BLOCK)
            ksl = pl.ds(kstart, kb_rows)
        else:
            qsl = pl.ds(qstart, DIL_BLOCK, stride=d)
            ksl = pl.ds(kstart, kb_rows, stride=d)
        qb = qs[qsl, :]
        kb = ks[ksl, :].astype(BF16)
        vb = vs[ksl, :].astype(BF16)
        qi = lax.broadcasted_iota(jnp.int32, (DIL_BLOCK, kb_rows), 0) + qsub
        kj = lax.broadcasted_iota(jnp.int32, (DIL_BLOCK, kb_rows), 1) + ksub
        dist = qi - kj
        valid = (dist >= 0) & (dist <= DIL_BLOCK)
        outs = []
        for hh in range(2):
            sel = first_head if hh == 0 else jnp.logical_not(first_head)
            qh = jnp.where(sel, qb, 0.0).astype(BF16)
            sc = jnp.where(valid, _dot_nt(qh, kb), -1e30)
            m = jnp.max(sc, -1, keepdims=True)
            p = jnp.exp(sc - m)
            l = jnp.sum(p, -1, keepdims=True)
            outs.append((_dot(p.astype(BF16), vb), m, l))
        acc[g, qsl, :] = jnp.where(first_head, outs[0][0], outs[1][0])
        ms[g, qsl, :] = jnp.where(first_head, outs[0][1], outs[1][1])
        ls[g, qsl, :] = jnp.where(first_head, outs[0][2], outs[1][2])

    for g, d in enumerate(DIL_DILATIONS):
        sub_len = seq // d
        nb = sub_len // DIL_BLOCK
        if nb == 1:
            def body(r, carry, g=g, d=d):
                block(g, d, r, r, DIL_BLOCK, 0, 0)
                return carry
            lax.fori_loop(0, d, body, 0)
        else:
            def body(i, carry, g=g, d=d, nb=nb):
                r = i & (d - 1)
                n = i >> int(math.log2(d))
                kn = jnp.maximum(n - 1, 0)
                block(g, d, r + n * (DIL_BLOCK * d), r + kn * (DIL_BLOCK * d), 2 * DIL_BLOCK,
                      n * DIL_BLOCK, kn * DIL_BLOCK)
                return carry
            lax.fori_loop(0, d * nb, body, 0)

    def comb_body(i, carry):
        rows = pl.ds(pl.multiple_of(i * rc, rc), rc)
        m0, m1, m2 = ms[0, rows, :], ms[1, rows, :], ms[2, rows, :]
        mmax = jnp.maximum(jnp.maximum(m0, m1), m2)
        e0, e1, e2 = jnp.exp(m0 - mmax), jnp.exp(m1 - mmax), jnp.exp(m2 - mmax)
        num = e0 * acc[0, rows, :] + e1 * acc[1, rows, :] + e2 * acc[2, rows, :]
        den = e0 * ls[0, rows, :] + e1 * ls[1, rows, :] + e2 * ls[2, rows, :]
        o_ref[0, rows, :] = (num / den).astype(o_ref.dtype)
        return carry

    lax.fori_loop(0, seq // rc, comb_body, 0)


def _dilated(hproj, tab_dil):
    b, s, _ = hproj.shape
    base = (2 * RET_QK_W + 2 * RET_V_W) // LANES
    pairs = DIL_W // LANES
    blk = lambda off: pl.BlockSpec((1, s, LANES), lambda i, j: (i, 0, base + off + j))
    n_g = len(DIL_DILATIONS)
    return pl.pallas_call(
        functools.partial(_dil_kernel, seq=s),
        grid=(b, pairs),
        in_specs=[blk(0), blk(pairs), blk(2 * pairs),
                  pl.BlockSpec((1, 3, s, LANES), lambda i, j: (i, 0, 0, 0))],
        out_specs=pl.BlockSpec((1, s, LANES), lambda i, j: (i, 0, j)),
        out_shape=jax.ShapeDtypeStruct((b, s, DIL_W), BF16),
        scratch_shapes=[pltpu.VMEM((s, LANES), F32)] * 3 + [pltpu.VMEM((n_g, s, LANES), F32)] * 3,
        compiler_params=_cparams(2),
        name="dilated_attention",
    )(hproj, hproj, hproj, tab_dil)


def _out_ln_kernel(*refs, n_in):
    x_ref = refs[0]
    ys = refs[1:1 + n_in]
    ws = refs[1 + n_in:1 + 2 * n_in]
    g_ref, b_ref, o_ref = refs[1 + 2 * n_in:]
    z = DN_ALPHA * x_ref[...]
    for y_ref, w_ref in zip(ys, ws):
        z = z + _dot(y_ref[...], w_ref[...])
    o_ref[...] = _layernorm_rows(z, g_ref[...], b_ref[...])


def _out_ln(x2d, ys, ws, g, b, tm):
    t, d = x2d.shape
    n_in = len(ys)
    row = lambda w: pl.BlockSpec((tm, w), lambda i: (i, 0))
    full = lambda a: pl.BlockSpec(a.shape, lambda i: (0, 0))
    return pl.pallas_call(
        functools.partial(_out_ln_kernel, n_in=n_in),
        grid=(t // tm,),
        in_specs=[row(d)] + [row(y.shape[1]) for y in ys] + [full(w) for w in ws] + [full(g), full(b)],
        out_specs=row(d),
        out_shape=jax.ShapeDtypeStruct((t, d), F32),
        compiler_params=_cparams(1),
        name="out_proj_ln",
    )(x2d, *ys, *ws, g, b)


def _conv_taps(ub, taps):
    k_w = len(taps)
    out = ub[8:] * taps[k_w - 1]
    for back in range(1, k_w):
        out = out + pltpu.roll(ub, back, 0)[8:] * taps[k_w - 1 - back]
    return out


def _fill_xb(xb, x_ref, xh_ref, i):
    xb[HALO:, :] = x_ref[0].astype(BF16)
    xb[0:HALO, :] = jnp.where(i > 0, xh_ref[0], 0.0).astype(BF16)


def _ffn_kernel(x_ref, xh_ref, wup_ref, cw_ref, wdn_ref, g_ref, b_ref, o_ref, xb, us, hs, acc, *, tm, fc, nc, rb):
    i = pl.program_id(1)
    c = pl.program_id(2)

    @pl.when(c == 0)
    def _():
        _fill_xb(xb, x_ref, xh_ref, i)

    us[...] = _dot(xb[...], wup_ref[0])

    def blk(r, carry):
        r0 = pl.multiple_of(r * rb, rb)
        rows_in = pl.ds(r0 + HALO - 8, rb + 8)
        for p in range(fc // LANES):
            cg = slice(p * LANES, (p + 1) * LANES)
            cv = slice(fc + p * LANES, fc + (p + 1) * LANES)
            tg = [cw_ref[0, j:j + 1, cg] for j in range(FFN_CONV)]
            tv = [cw_ref[0, j:j + 1, cv] for j in range(FFN_CONV)]
            gate = _conv_taps(us[rows_in, cg], tg) + cw_ref[0, FFN_CONV:FFN_CONV + 1, cg]
            val = _conv_taps(us[rows_in, cv], tv) + cw_ref[0, FFN_CONV:FFN_CONV + 1, cv]
            hs[pl.ds(r0, rb), cg] = (gate * _sigmoid(gate) * val).astype(BF16)
        return carry

    lax.fori_loop(0, tm // rb, blk, 0)
    d = _dot(hs[...], wdn_ref[0])

    @pl.when(c == 0)
    def _():
        acc[...] = d

    @pl.when(c > 0)
    def _():
        acc[...] += d

    @pl.when(c == nc - 1)
    def _():
        o_ref[0] = _layernorm_rows(DN_ALPHA * x_ref[0] + acc[...], g_ref[...], b_ref[...])


def _ffn(x3, wup_r, cw_r, wdn_r, g, b, tm, fc):
    bsz, s, d = x3.shape
    nc = D_FF // fc
    hb = tm // HALO
    return pl.pallas_call(
        functools.partial(_ffn_kernel, tm=tm, fc=fc, nc=nc, rb=128),
        grid=(bsz, s // tm, nc),
        in_specs=[pl.BlockSpec((1, tm, d), lambda bi, i, c: (bi, i, 0)),
                  pl.BlockSpec((1, HALO, d), lambda bi, i, c: (bi, jnp.maximum(i * hb - 1, 0), 0)),
                  pl.BlockSpec((1, d, 2 * fc), lambda bi, i, c: (c, 0, 0)),
                  pl.BlockSpec((1, 8, 2 * fc), lambda bi, i, c: (c, 0, 0)),
                  pl.BlockSpec((1, fc, d), lambda bi, i, c: (c, 0, 0)),
                  pl.BlockSpec((1, d), lambda bi, i, c: (0, 0)),
                  pl.BlockSpec((1, d), lambda bi, i, c: (0, 0))],
        out_specs=pl.BlockSpec((1, tm, d), lambda bi, i, c: (bi, i, 0)),
        out_shape=jax.ShapeDtypeStruct((bsz, s, d), F32),
        scratch_shapes=[pltpu.VMEM((HALO + tm, d), BF16),
                        pltpu.VMEM((HALO + tm, 2 * fc), F32),
                        pltpu.VMEM((tm, fc), BF16),
                        pltpu.VMEM((tm, d), F32)],
        compiler_params=_cparams(3),
        name="conv_ffn_ln",
    )(x3, x3, wup_r, cw_r, wdn_r, g, b)


def _ffn_weights(w_up, conv_w, conv_b, w_down, fc):
    nc = D_FF // fc
    d = w_up.shape[0]
    arr = lambda a: a.reshape(a.shape[0], 2, nc, fc).transpose(2, 0, 1, 3).reshape(nc, a.shape[0], 2 * fc)
    wup_r = arr(w_up).astype(BF16)
    cw = jnp.concatenate([conv_w, conv_b[None, :], jnp.zeros((8 - FFN_CONV - 1, 2 * D_FF), F32)], 0)
    cw_r = arr(cw)
    wdn_r = w_down.reshape(nc, fc, d).astype(BF16)
    return wup_r, cw_r, wdn_r


def _odd_in_kernel(x_ref, xh_ref, w_ref, cw_ref, ws_ref, o_ref, os_ref, xb, us, *, tm, tn, n_conv, rb):
    i = pl.program_id(1)
    c = pl.program_id(2)

    @pl.when(c == 0)
    def _():
        _fill_xb(xb, x_ref, xh_ref, i)
        os_ref[0] = _dot(xb[HALO:, :], ws_ref[...])

    us[...] = _dot(xb[...], w_ref[...])

    @pl.when(c < n_conv)
    def _():
        def blk(r, carry):
            r0 = pl.multiple_of(r * rb, rb)
            rows_in = pl.ds(r0 + HALO - 8, rb + 8)
            for p in range(tn // LANES):
                cs = slice(p * LANES, (p + 1) * LANES)
                taps = [cw_ref[0, j:j + 1, cs] for j in range(GDN_CONV)]
                y = _conv_taps(us[rows_in, cs], taps)
                o_ref[0, pl.ds(r0, rb), cs] = (y * _sigmoid(y)).astype(o_ref.dtype)
            return carry

        lax.fori_loop(0, tm // rb, blk, 0)

    @pl.when(c >= n_conv)
    def _():
        o_ref[0] = us[HALO:, :].astype(o_ref.dtype)


def _odd_in(x3, w_main, cw_r, w_small, tm, tn):
    bsz, s, d = x3.shape
    n = w_main.shape[1]
    n_conv = cw_r.shape[0]
    hb = tm // HALO
    return pl.pallas_call(
        functools.partial(_odd_in_kernel, tm=tm, tn=tn, n_conv=n_conv, rb=128),
        grid=(bsz, s // tm, n // tn),
        in_specs=[pl.BlockSpec((1, tm, d), lambda bi, i, c: (bi, i, 0)),
                  pl.BlockSpec((1, HALO, d), lambda bi, i, c: (bi, jnp.maximum(i * hb - 1, 0), 0)),
                  pl.BlockSpec((d, tn), lambda bi, i, c: (0, c)),
                  pl.BlockSpec((1, 8, tn), lambda bi, i, c: (jnp.minimum(c, n_conv - 1), 0, 0)),
                  pl.BlockSpec((d, LANES), lambda bi, i, c: (0, 0))],
        out_specs=[pl.BlockSpec((1, tm, tn), lambda bi, i, c: (bi, i, c)),
                   pl.BlockSpec((1, tm, LANES), lambda bi, i, c: (bi, i, 0))],
        out_shape=[jax.ShapeDtypeStruct((bsz, s, n), BF16),
                   jax.ShapeDtypeStruct((bsz, s, LANES), F32)],
        scratch_shapes=[pltpu.VMEM((HALO + tm, d), BF16),
                        pltpu.VMEM((HALO + tm, tn), F32)],
        compiler_params=_cparams(3),
        name="od_in_proj_conv",
    )(x3, x3, w_main, cw_r, w_small)


GDN_WIDE = 4
GDN_BLK = GDN_CHUNK * GDN_WIDE
GDN_HG = 4


def _split_bf16(a):
    hi = a.astype(BF16)
    return hi, (a - hi.astype(F32)).astype(BF16)


def _gdn_kernel(prm_ref, q_ref, k_ref, v_ref, gt_ref, sm_ref, nw_ref, o_ref,
                mq_s, nn_s, op_s, eg_s, st_s, rhs_s, kd_s, qd_s, at_s, *, seq):
    hg = pl.program_id(1)
    c, wd, blk = GDN_CHUNK, GDN_WIDE, GDN_BLK
    ww = c * wd
    n_blocks = seq // blk
    n_chunks = seq // c
    shift = int(math.log2(c))

    r_w = lax.broadcasted_iota(jnp.int32, (c, ww), 0)
    l_w = lax.broadcasted_iota(jnp.int32, (c, ww), 1)
    j_w = l_w & (c - 1)
    chunk_of_lane = l_w >> shift
    tri_w = r_w >= j_w
    strict_w = r_w > j_w
    upper_w = jnp.where(r_w <= j_w, 1.0, 0.0).astype(F32)
    bmask = (lax.broadcasted_iota(jnp.int32, (ww, ww), 0) >> shift) == (lax.broadcasted_iota(jnp.int32, (ww, ww), 1) >> shift)
    ltri = jnp.where(lax.broadcasted_iota(jnp.int32, (c, c), 0) >= lax.broadcasted_iota(jnp.int32, (c, c), 1),
                     1.0, 0.0).astype(BF16)
    ones_cc = jnp.ones((c, c), BF16)
    lane_blk = lax.broadcasted_iota(jnp.int32, (blk, LANES), 1)
    first_half = lax.broadcasted_iota(jnp.int32, (c, LANES), 1) < c
    a_neg = -jnp.exp(prm_ref[0:1, :])
    dtb = prm_ref[1:2, :]

    def bdiag(xw):
        return jnp.where(bmask, jnp.concatenate([xw] * wd, 0), 0.0).astype(BF16)

    def prep(i, carry):
        rows = pl.ds(pl.multiple_of(i * blk, blk), blk)
        sm = sm_ref[0, rows, :]
        beta_all = _sigmoid(sm)
        sp_in = sm + dtb
        g_all = a_neg * (jnp.maximum(sp_in, 0.0) + jnp.log1p(jnp.exp(-jnp.abs(sp_in))))
        pws = []
        for hh in range(GDN_HG):
            head = hg * GDN_HG + hh
            cols = slice(hh * GDN_DK, (hh + 1) * GDN_DK)
            q = q_ref[0, rows, cols].astype(F32)
            k = k_ref[0, rows, cols].astype(F32)
            v = v_ref[0, rows, cols].astype(F32)
            qn = q * lax.rsqrt(jnp.sum(q * q, -1, keepdims=True) + 1e-6) * (GDN_DK ** -0.5)
            kn = k * lax.rsqrt(jnp.sum(k * k, -1, keepdims=True) + 1e-6)
            beta = jnp.sum(jnp.where(lane_blk == head, beta_all, 0.0), -1, keepdims=True)
            g = jnp.sum(jnp.where(lane_blk == GDN_HEADS + head, g_all, 0.0), -1, keepdims=True)
            kb = kn * beta

            g_w = jnp.broadcast_to(g[0:c], (c, ww))
            for ch in range(1, wd):
                g_w = jnp.where(chunk_of_lane == ch, jnp.broadcast_to(g[ch * c:(ch + 1) * c], (c, ww)), g_w)
            g_hi, g_lo = _split_bf16(g_w)
            gc_col = _dot(ltri, g_hi) + _dot(ltri, g_lo)
            gu_hi, gu_lo = _split_bf16(g_w * upper_w)
            gc_row = _dot(ones_cc, gu_hi) + _dot(ones_cc, gu_lo)
            decay = jnp.where(tri_w, jnp.exp(jnp.where(tri_w, gc_col - gc_row, 0.0)), 0.0)

            a_parts, qk_parts = [], []
            for p in range(wd // 2):
                pr = slice(2 * p * c, (2 * p + 2) * c)
                gram = _dot_nt(jnp.concatenate([kb[pr], qn[pr]], 0).astype(BF16), kn[pr].astype(BF16))
                a_parts.append(jnp.where(first_half, gram[0:c], gram[c:2 * c]))
                qk_parts.append(jnp.where(first_half, gram[2 * c:3 * c], gram[3 * c:4 * c]))
            pws.append(-jnp.where(strict_w, jnp.concatenate(a_parts, 1) * decay, 0.0))
            at_s[hh] = jnp.where(tri_w, jnp.concatenate(qk_parts, 1) * decay, 0.0)

            gc_parts, gl_parts = [], []
            for p in range(wd // 2):
                x = gc_col[:, 2 * p * c:(2 * p + 2) * c]
                xr = pltpu.roll(x, c, 1)
                for part in (jnp.where(first_half, x, xr), jnp.where(first_half, xr, x)):
                    gc_parts.append(part)
                    gl_parts.append(jnp.broadcast_to(part[c - 1:c, :], (c, LANES)))
            gc_t = jnp.concatenate(gc_parts, 0)
            gl_t = jnp.concatenate(gl_parts, 0)
            egc = jnp.exp(gc_t)
            rhs_s[hh, :, 0:GDN_DK] = kb * egc
            rhs_s[hh, :, GDN_DK:] = v * beta
            kd_s[hh] = (kn * jnp.exp(gl_t - gc_t)).astype(BF16)
            qd_s[hh] = qn * egc
            for ch in range(wd):
                eg_s[hh, pl.ds(pl.multiple_of((i * wd + ch) * 8, 8), 8), :] = jnp.broadcast_to(
                    jnp.exp(gl_t[ch * c:ch * c + 1, :]), (8, LANES))

        ews = list(pws)
        for _ in range(shift - 1):
            pws = [_dot(pw.astype(BF16), bdiag(pw)) for pw in pws]
            ews = [e_w + pw + _dot(e_w.astype(BF16), bdiag(pw)) for e_w, pw in zip(ews, pws)]

        for hh in range(GDN_HG):
            rhs = rhs_s[hh]
            sol = rhs + _dot(bdiag(ews[hh]), rhs.astype(BF16))
            solb = sol.astype(BF16)
            asol = _dot(bdiag(at_s[hh]), solb)
            qp = (qd_s[hh] - asol[:, :GDN_DK]).astype(BF16)
            op_s[hh, rows, :] = asol[:, GDN_DK:]
            kd = kd_s[hh]
            for ch in range(wd):
                cr = slice(ch * c, (ch + 1) * c)
                ci = i * wd + ch
                mn = _dot_tn(kd[cr], solb[cr])
                mq_s[hh, ci, 0:GDN_DK, :] = mn[:, :GDN_DK].astype(BF16)
                mq_s[hh, ci, GDN_DK:GDN_DK + c, :] = qp[cr]
                nn_s[hh, ci] = mn[:, GDN_DK:]
        return carry

    lax.fori_loop(0, n_blocks, prep, 0)
    st_s[...] = jnp.zeros_like(st_s)

    def scan(i, carry):
        rows = pl.ds(pl.multiple_of(i * c, c), c)
        for hh in range(GDN_HG):
            cols = slice(hh * GDN_DK, (hh + 1) * GDN_DK)
            state = st_s[hh]
            x = _dot(mq_s[hh, i], state.astype(BF16))
            st_s[hh] = state * eg_s[hh, pl.ds(pl.multiple_of(i * 8, 8), 1), :] - x[:GDN_DK] + nn_s[hh, i]
            o = x[GDN_DK:] + op_s[hh, rows, :]
            y = o * lax.rsqrt(jnp.mean(o * o, -1, keepdims=True) + EPS) * nw_ref[...]
            gt = gt_ref[0, rows, cols].astype(F32)
            o_ref[0, rows, cols] = (y * (gt * _sigmoid(gt))).astype(o_ref.dtype)
        return carry

    lax.fori_loop(0, n_chunks, scan, 0)


def _gdn(qkvg, small, a_log, dt_bias, norm_w):
    b, s, _ = qkvg.shape
    gw = GDN_HG * GDN_DK
    groups = GDN_HEADS // GDN_HG
    n_chunks = s // GDN_CHUNK
    blk = lambda off: pl.BlockSpec((1, s, gw), lambda i, j: (i, 0, off + j))
    prm = jnp.zeros((8, LANES), F32)
    prm = prm.at[0, GDN_HEADS:2 * GDN_HEADS].set(a_log).at[1, GDN_HEADS:2 * GDN_HEADS].set(dt_bias)
    return pl.pallas_call(
        functools.partial(_gdn_kernel, seq=s),
        grid=(b, groups),
        in_specs=[pl.BlockSpec((8, LANES), lambda i, j: (0, 0)),
                  blk(0), blk(groups), blk(2 * groups), blk(3 * groups),
                  pl.BlockSpec((1, s, LANES), lambda i, j: (i, 0, 0)),
                  pl.BlockSpec((1, GDN_DV), lambda i, j: (0, 0))],
        out_specs=pl.BlockSpec((1, s, gw), lambda i, j: (i, 0, j)),
        out_shape=jax.ShapeDtypeStruct((b, s, GDN_W), BF16),
        scratch_shapes=[pltpu.VMEM((GDN_HG, n_chunks, GDN_DK + GDN_CHUNK, GDN_DK), BF16),
                        pltpu.VMEM((GDN_HG, n_chunks, GDN_DK, GDN_DV), F32),
                        pltpu.VMEM((GDN_HG, s, GDN_DV), F32),
                        pltpu.VMEM((GDN_HG, n_chunks * 8, LANES), F32),
                        pltpu.VMEM((GDN_HG, GDN_DK, GDN_DV), F32),
                        pltpu.VMEM((GDN_HG, GDN_BLK, GDN_DK + GDN_DV), F32),
                        pltpu.VMEM((GDN_HG, GDN_BLK, GDN_DK), BF16),
                        pltpu.VMEM((GDN_HG, GDN_BLK, GDN_DK), F32),
                        pltpu.VMEM((GDN_HG, GDN_CHUNK, GDN_BLK), F32)],
        compiler_params=_cparams(2),
        name="gated_delta_rule",
    )(prm, qkvg, qkvg, qkvg, qkvg, small, norm_w.reshape(1, GDN_DV))


def kernel(x, positions, ev_w_in, ev_w_out, od_w_in, od_conv_w, od_a_log, od_dt_bias, od_norm_w, od_w_out,
           ffn_w_up, ffn_conv_w, ffn_conv_b, ffn_w_down, ln1_g, ln1_b, ln2_g, ln2_b):
    b, s, d = x.shape
    t = b * s
    ret_f, dil_f = _rope_freqs()
    pos3 = positions.reshape(b, s, 1)
    tab_ret = _trig_tables(pos3, ret_f, (True, False))
    tab_dil = _trig_tables(pos3, dil_f, (True, False, False))
    ret_consts = _ret_consts()
    fc = D_FF // 2
    row = lambda a: a.reshape(1, d)

    for layer in range(DEPTH):
        j = layer // 2
        x2 = x.reshape(t, d)
        if layer % 2 == 0:
            hproj = _proj(x2, ev_w_in[j].astype(BF16), 1024, EV_IN // 4).reshape(b, s, EV_IN)
            ya = _retention(hproj, tab_ret, ret_consts)
            yb = _dilated(hproj, tab_dil)
            w_out = ev_w_out[j].astype(BF16)
            x2 = _out_ln(x2, [ya.reshape(t, RET_V_W), yb.reshape(t, DIL_W)],
                         [w_out[:RET_V_W], w_out[RET_V_W:]], row(ln1_g[layer]), row(ln1_b[layer]), 512)
        else:
            w_in = od_w_in[j]
            w_main = w_in[:, :4 * GDN_W].astype(BF16)
            w_small = jnp.pad(w_in[:, 4 * GDN_W:], ((0, 0), (0, LANES - 2 * GDN_HEADS))).astype(BF16)
            cw = jnp.pad(od_conv_w[j], ((0, 8 - GDN_CONV), (0, 0)))
            cw_r = cw.reshape(8, 3, GDN_W).transpose(1, 0, 2)
            qkvg, small = _odd_in(x, w_main, cw_r, w_small, 512, GDN_W)
            yc = _gdn(qkvg, small, od_a_log[j], od_dt_bias[j], od_norm_w[j])
            x2 = _out_ln(x2, [yc.reshape(t, GDN_W)], [od_w_out[j].astype(BF16)],
                         row(ln1_g[layer]), row(ln1_b[layer]), 512)
        x = x2.reshape(b, s, d)
        wup_r, cw_r, wdn_r = _ffn_weights(ffn_w_up[layer], ffn_conv_w[layer], ffn_conv_b[layer],
                                          ffn_w_down[layer], fc)
        x = _ffn(x, wup_r, cw_r, wdn_r, row(ln2_g[layer]), row(ln2_b[layer]), 512, fc)
    return x
```

```python
import functools
import math

import jax
import jax.numpy as jnp
import numpy as np
from jax import lax
from jax.experimental import pallas as pl
from jax.experimental.pallas import tpu as pltpu

F32 = jnp.float32
BF16 = jnp.bfloat16

D_MODEL = 1024
DEPTH = 4
RET_HEADS, RET_DK, RET_DV, RET_CHUNK, RET_THETA = 4, 128, 256, 128, 10000.0
DIL_HEADS, DIL_HD, DIL_BLOCK = 8, 64, 128
DIL_DILATIONS = (1, 4, 16)
DIL_INTERLEAVE = 4
ROPE_THETA, ROPE_DIMS = 500000.0, DIL_HD // 4
GDN_HEADS, GDN_DK, GDN_DV, GDN_CHUNK, GDN_CONV = 8, 128, 128, 64, 4
D_FF, FFN_CONV = 2816, 3
DN_ALPHA = (2.0 * DEPTH) ** 0.25
EPS = 1e-5

RET_QK_W = RET_HEADS * RET_DK
RET_V_W = RET_HEADS * RET_DV
DIL_W = DIL_HEADS * DIL_HD
EV_IN = 2 * RET_QK_W + 2 * RET_V_W + 3 * DIL_W
GDN_W = GDN_HEADS * GDN_DK

LANES = 128
HALO = 16
VMEM_LIMIT = 56 * 1024 * 1024
HI = lax.Precision.HIGHEST


def _cparams(n_axes, vmem=VMEM_LIMIT):
    return pltpu.CompilerParams(dimension_semantics=("arbitrary",) * n_axes, vmem_limit_bytes=vmem)


def _dot(a, b):
    return jnp.dot(a, b, preferred_element_type=F32)


def _dot_nt(a, b):
    return lax.dot_general(a, b, (((1,), (1,)), ((), ())), preferred_element_type=F32)


def _dot_tn(a, b):
    return lax.dot_general(a, b, (((0,), (0,)), ((), ())), preferred_element_type=F32)


def _sigmoid(x):
    return 1.0 / (1.0 + jnp.exp(-x))


def _layernorm_rows(z, g, b):
    mu = jnp.mean(z, -1, keepdims=True)
    zc = z - mu
    var = jnp.mean(zc * zc, -1, keepdims=True)
    return zc * lax.rsqrt(var + EPS) * g + b


def _proj_kernel(x_ref, w_ref, o_ref, xb_ref):
    @pl.when(pl.program_id(2) == 0)
    def _():
        xb_ref[...] = x_ref[0].astype(BF16)

    o_ref[0] = _dot(xb_ref[...], w_ref[...]).astype(o_ref.dtype)


def _proj(x3, w_bf16, tm, tn):
    bsz, s, k = x3.shape
    n = w_bf16.shape[1]
    return pl.pallas_call(
        _proj_kernel,
        grid=(bsz, s // tm, n // tn),
        in_specs=[pl.BlockSpec((1, tm, k), lambda bi, i, j: (bi, i, 0)),
                  pl.BlockSpec((k, tn), lambda bi, i, j: (0, j))],
        out_specs=pl.BlockSpec((1, tm, tn), lambda bi, i, j: (bi, i, j)),
        out_shape=jax.ShapeDtypeStruct((bsz, s, n), BF16),
        scratch_shapes=[pltpu.VMEM((tm, k), BF16)],
        compiler_params=_cparams(3),
        name="ev_in_proj",
    )(x3, w_bf16)


def _trig_kernel(pos_ref, f_ref, o_ref, *, n_tab, cos_mask):
    rc = 256

    def body(i, carry):
        rows = pl.ds(pl.multiple_of(i * rc, rc), rc)
        pos = pos_ref[0, rows, :].astype(F32)
        for t in range(n_tab):
            ang = pos * f_ref[t:t + 1, :]
            o_ref[0, t, rows, :] = jnp.cos(ang) if cos_mask[t] else jnp.sin(ang)
        return carry

    lax.fori_loop(0, pos_ref.shape[1] // rc, body, 0)


def _trig_tables(pos3, freqs, cos_mask):
    b, s, _ = pos3.shape
    n_tab = len(cos_mask)
    fpad = jnp.zeros((8, LANES), F32).at[:n_tab].set(freqs)
    return pl.pallas_call(
        functools.partial(_trig_kernel, n_tab=n_tab, cos_mask=cos_mask),
        grid=(b,),
        in_specs=[pl.BlockSpec((1, s, 1), lambda i: (i, 0, 0)),
                  pl.BlockSpec((8, LANES), lambda i: (0, 0))],
        out_specs=pl.BlockSpec((1, n_tab, s, LANES), lambda i: (i, 0, 0, 0)),
        out_shape=jax.ShapeDtypeStruct((b, n_tab, s, LANES), F32),
        compiler_params=_cparams(1),
        name="rope_tables",
    )(pos3, fpad)


def _rope_freqs():
    half = RET_DK // 2
    inv = jnp.power(RET_THETA, -jnp.arange(half, dtype=F32) * 2.0 / RET_DK)
    ret = jnp.stack([jnp.concatenate([inv, inv]), jnp.concatenate([-inv, inv])])
    hh = ROPE_DIMS // 2
    invd = jnp.power(ROPE_THETA, -jnp.arange(hh, dtype=F32) * 2.0 / ROPE_DIMS)
    z = jnp.zeros((DIL_HD - ROPE_DIMS,), F32)
    zh = jnp.zeros((hh,), F32)
    f_cos = jnp.concatenate([invd, invd, z])
    f_a = jnp.concatenate([-invd, zh, z])
    f_b = jnp.concatenate([zh, invd, z])
    dil = jnp.stack([jnp.tile(f_cos, 2), jnp.tile(f_a, 2), jnp.tile(f_b, 2)])
    return ret, dil


def _ret_kernel(gch_ref, q_ref, k_ref, v_ref, g_ref, tab_ref, dm_ref, zt_ref, xi_ref, o_ref, r_ref, *, n_chunks):
    r_ref[...] = jnp.zeros_like(r_ref)
    c = RET_CHUNK
    heads = range(RET_HEADS)

    def body(ci, carry):
        rows = pl.ds(pl.multiple_of(ci * c, c), c)
        cosr = tab_ref[0, 0, rows, :]
        sinr = tab_ref[0, 1, rows, :]
        qk = lambda h: slice(h * RET_DK, (h + 1) * RET_DK)
        vv = lambda h: slice(h * RET_DV, (h + 1) * RET_DV)
        qs = [q_ref[0, rows, qk(h)].astype(F32) for h in heads]
        ks = [k_ref[0, rows, qk(h)].astype(F32) for h in heads]
        qr = [q * cosr + pltpu.roll(q, RET_DK // 2, 1) * sinr for q in qs]
        kr = [(k * cosr + pltpu.roll(k, RET_DK // 2, 1) * sinr) * (RET_DK ** -0.5) for k in ks]
        vs = [v_ref[0, rows, vv(h)] for h in heads]
        r_prev = [r_ref[h] for h in heads]
        scores = [_dot_nt(qr[h].astype(BF16), kr[h].astype(BF16)) * dm_ref[h] for h in heads]
        inter = [_dot((qr[h] * xi_ref[h]).astype(BF16), r_prev[h].astype(BF16)) for h in heads]
        kv = [_dot_tn((kr[h] * zt_ref[h]).astype(BF16), vs[h]) for h in heads]
        os_ = [_dot(scores[h].astype(BF16), vs[h]) + inter[h] for h in heads]
        for h in heads:
            r_ref[h] = r_prev[h] * gch_ref[h] + kv[h]
            o = os_[h]
            y = o * lax.rsqrt(jnp.mean(o * o, -1, keepdims=True) + EPS)
            g = g_ref[0, rows, vv(h)].astype(F32)
            o_ref[0, rows, vv(h)] = (y * (g * _sigmoid(g))).astype(o_ref.dtype)
        return carry

    lax.fori_loop(0, n_chunks, body, 0)


def _ret_consts():
    hh = np.arange(RET_HEADS, dtype=np.float64)
    lg = np.log1p(-np.power(2.0, -5.0 - hh))
    idx = np.arange(RET_CHUNK, dtype=np.float64)
    rel = idx[:, None] - idx[None, :]
    dmat = np.where(rel >= 0, np.exp(np.maximum(rel, 0.0) * lg[:, None, None]), 0.0)
    zeta = np.exp((RET_CHUNK - 1 - idx) * lg[:, None])
    xi = np.exp((idx + 1) * lg[:, None])
    gch = np.exp(RET_CHUNK * lg)
    bc = lambda a: np.broadcast_to(a[:, :, None], (RET_HEADS, RET_CHUNK, RET_DK))
    return (jnp.asarray(gch, F32), jnp.asarray(dmat, F32), jnp.asarray(bc(zeta), F32), jnp.asarray(bc(xi), F32))


def _retention(hproj, tab_ret, consts):
    b, s, _ = hproj.shape
    gch, dmat, zeta, xi = consts
    v_off = 2 * RET_QK_W // RET_V_W
    all_heads = pl.BlockSpec((RET_HEADS, RET_CHUNK, RET_DK), lambda i: (0, 0, 0))
    return pl.pallas_call(
        functools.partial(_ret_kernel, n_chunks=s // RET_CHUNK),
        grid=(b,),
        in_specs=[pl.BlockSpec(memory_space=pltpu.SMEM),
                  pl.BlockSpec((1, s, RET_QK_W), lambda i: (i, 0, 0)),
                  pl.BlockSpec((1, s, RET_QK_W), lambda i: (i, 0, 1)),
                  pl.BlockSpec((1, s, RET_V_W), lambda i: (i, 0, v_off)),
                  pl.BlockSpec((1, s, RET_V_W), lambda i: (i, 0, v_off + 1)),
                  pl.BlockSpec((1, 2, s, LANES), lambda i: (i, 0, 0, 0)),
                  all_heads, all_heads, all_heads],
        out_specs=pl.BlockSpec((1, s, RET_V_W), lambda i: (i, 0, 0)),
        out_shape=jax.ShapeDtypeStruct((b, s, RET_V_W), BF16),
        scratch_shapes=[pltpu.VMEM((RET_HEADS, RET_DK, RET_DV), F32)],
        compiler_params=_cparams(1),
        name="retention",
    )(gch, hproj, hproj, hproj, hproj, tab_ret, dmat, zeta, xi)


def _dil_kernel(q_ref, k_ref, v_ref, tab_ref, o_ref, qs, ks, vs, acc, ms, ls, *, seq):
    rc = 256
    lane = lax.broadcasted_iota(jnp.int32, (DIL_BLOCK, LANES), 1)
    first_head = lane < DIL_HD

    def rot_body(i, carry):
        rows = pl.ds(pl.multiple_of(i * rc, rc), rc)
        cosd = tab_ref[0, 0, rows, :]
        sina = tab_ref[0, 1, rows, :]
        sinb = tab_ref[0, 2, rows, :]
        hh = ROPE_DIMS // 2

        def rot(x):
            return x * cosd + pltpu.roll(x, LANES - hh, 1) * sina + pltpu.roll(x, hh, 1) * sinb

        qs[rows, :] = rot(q_ref[0, rows, :].astype(F32)) * (DIL_HD ** -0.5)
        ks[rows, :] = rot(k_ref[0, rows, :].astype(F32))
        vs[rows, :] = v_ref[0, rows, :].astype(F32)
        return carry

    lax.fori_loop(0, seq // rc, rot_body, 0)

    def blocks(g, d, kb_rows, specs):
        def sl(start, n):
            return pl.ds(start, n) if d == 1 else pl.ds(start, n, stride=d)

        rel = (lax.broadcasted_iota(jnp.int32, (DIL_BLOCK, kb_rows), 0)
               - lax.broadcasted_iota(jnp.int32, (DIL_BLOCK, kb_rows), 1))
        qsl = [sl(q0, DIL_BLOCK) for q0, _, _ in specs]
        ksl = [sl(k0, kb_rows) for _, k0, _ in specs]
        qbs = [qs[s_, :] for s_ in qsl]
        kbs = [ks[s_, :].astype(BF16) for s_ in ksl]
        vbs = [vs[s_, :].astype(BF16) for s_ in ksl]
        valid = []
        for _, _, off in specs:
            dist = rel + off
            valid.append((dist >= 0) & (dist <= DIL_BLOCK))
        heads = (first_head, jnp.logical_not(first_head))
        sc = [[jnp.where(valid[b_], _dot_nt(jnp.where(sel, qbs[b_], 0.0).astype(BF16), kbs[b_]), -1e30)
               for sel in heads] for b_ in range(len(specs))]
        mx = [[jnp.max(s_, -1, keepdims=True) for s_ in row_] for row_ in sc]
        pr = [[jnp.exp(s_ - m_) for s_, m_ in zip(rs, rm)] for rs, rm in zip(sc, mx)]
        sm = [[jnp.sum(p_, -1, keepdims=True) for p_ in row_] for row_ in pr]
        pv = [[_dot(p_.astype(BF16), vbs[b_]) for p_ in pr[b_]] for b_ in range(len(specs))]
        for b_ in range(len(specs)):
            acc[g, qsl[b_], :] = jnp.where(first_head, pv[b_][0], pv[b_][1])
            ms[g, qsl[b_], :] = jnp.where(first_head, mx[b_][0], mx[b_][1])
            ls[g, qsl[b_], :] = jnp.where(first_head, sm[b_][0], sm[b_][1])

    nbi = DIL_INTERLEAVE
    for g, d in enumerate(DIL_DILATIONS):
        nb = seq // d // DIL_BLOCK
        span = DIL_BLOCK * d
        if nb == 1:
            def body(i, carry, g=g, d=d):
                blocks(g, d, DIL_BLOCK, [(i * nbi + u, i * nbi + u, 0) for u in range(nbi)])
                return carry
            lax.fori_loop(0, d // nbi, body, 0)
        elif d >= nbi:
            def body(i, carry, g=g, d=d, span=span):
                n = i // (d // nbi)
                r0 = (i % (d // nbi)) * nbi
                kn = jnp.maximum(n - 1, 0)
                off = (n - kn) * DIL_BLOCK
                blocks(g, d, 2 * DIL_BLOCK, [(r0 + u + n * span, r0 + u + kn * span, off) for u in range(nbi)])
                return carry
            lax.fori_loop(0, nb * d // nbi, body, 0)
        else:
            def body(i, carry, g=g, d=d, span=span):
                specs = []
                for u in range(nbi):
                    n = i * nbi + u
                    kn = jnp.maximum(n - 1, 0)
                    specs.append((n * span, kn * span, (n - kn) * DIL_BLOCK))
                blocks(g, d, 2 * DIL_BLOCK, specs)
                return carry
            lax.fori_loop(0, nb // nbi, body, 0)

    def comb_body(i, carry):
        rows = pl.ds(pl.multiple_of(i * rc, rc), rc)
        m0, m1, m2 = ms[0, rows, :], ms[1, rows, :], ms[2, rows, :]
        mmax = jnp.maximum(jnp.maximum(m0, m1), m2)
        e0, e1, e2 = jnp.exp(m0 - mmax), jnp.exp(m1 - mmax), jnp.exp(m2 - mmax)
        num = e0 * acc[0, rows, :] + e1 * acc[1, rows, :] + e2 * acc[2, rows, :]
        den = e0 * ls[0, rows, :] + e1 * ls[1, rows, :] + e2 * ls[2, rows, :]
        o_ref[0, rows, :] = (num / den).astype(o_ref.dtype)
        return carry

    lax.fori_loop(0, seq // rc, comb_body, 0)


def _dilated(hproj, tab_dil):
    b, s, _ = hproj.shape
    base = (2 * RET_QK_W + 2 * RET_V_W) // LANES
    pairs = DIL_W // LANES
    blk = lambda off: pl.BlockSpec((1, s, LANES), lambda i, j: (i, 0, base + off + j))
    n_g = len(DIL_DILATIONS)
    return pl.pallas_call(
        functools.partial(_dil_kernel, seq=s),
        grid=(b, pairs),
        in_specs=[blk(0), blk(pairs), blk(2 * pairs),
                  pl.BlockSpec((1, 3, s, LANES), lambda i, j: (i, 0, 0, 0))],
        out_specs=pl.BlockSpec((1, s, LANES), lambda i, j: (i, 0, j)),
        out_shape=jax.ShapeDtypeStruct((b, s, DIL_W), BF16),
        scratch_shapes=[pltpu.VMEM((s, LANES), F32)] * 3 + [pltpu.VMEM((n_g, s, LANES), F32)] * 3,
        compiler_params=_cparams(2),
        name="dilated_attention",
    )(hproj, hproj, hproj, tab_dil)


def _out_ln_kernel(*refs, n_in):
    x_ref = refs[0]
    ys = refs[1:1 + n_in]
    ws = refs[1 + n_in:1 + 2 * n_in]
    g_ref, b_ref, o_ref = refs[1 + 2 * n_in:]
    z = DN_ALPHA * x_ref[0]
    for y_ref, w_ref in zip(ys, ws):
        z = z + _dot(y_ref[0], w_ref[...])
    o_ref[0] = _layernorm_rows(z, g_ref[...], b_ref[...])


def _out_ln(x3, ys, ws, g, b, tm):
    bsz, s, d = x3.shape
    n_in = len(ys)
    row = lambda w: pl.BlockSpec((1, tm, w), lambda bi, i: (bi, i, 0))
    full = lambda a: pl.BlockSpec(a.shape, lambda bi, i: (0, 0))
    return pl.pallas_call(
        functools.partial(_out_ln_kernel, n_in=n_in),
        grid=(bsz, s // tm),
        in_specs=[row(d)] + [row(y.shape[2]) for y in ys] + [full(w) for w in ws] + [full(g), full(b)],
        out_specs=row(d),
        out_shape=jax.ShapeDtypeStruct((bsz, s, d), F32),
        compiler_params=_cparams(2),
        name="out_proj_ln",
    )(x3, *ys, *ws, g, b)


def _conv_taps(ub, taps):
    k_w = len(taps)
    out = ub[8:] * taps[k_w - 1]
    for back in range(1, k_w):
        out = out + pltpu.roll(ub, back, 0)[8:] * taps[k_w - 1 - back]
    return out


def _fill_xb(xb, x_ref, xh_ref, i):
    xb[HALO:, :] = x_ref[0].astype(BF16)
    xb[0:HALO, :] = jnp.where(i > 0, xh_ref[0], 0.0).astype(BF16)


FFN_PIECE = 256


def _ffn_kernel(x_ref, xh_ref, wup_ref, cw_ref, wdn_ref, g_ref, b_ref, o_ref, xb, us, hs, *, tm, rb):
    i = pl.program_id(1)
    _fill_xb(xb, x_ref, xh_ref, i)
    pc = FFN_PIECE
    n_pieces = D_FF // pc

    def up(p):
        us[p % 2] = _dot(xb[...], wup_ref[:, 2 * pc * p:2 * pc * (p + 1)])

    def gate_piece(p):
        u = us.at[p % 2]
        for r in range(tm // rb):
            rows_in = pl.ds(r * rb + HALO - 8, rb + 8)
            for h in range(pc // LANES):
                cg = slice(h * LANES, (h + 1) * LANES)
                cv = slice(pc + h * LANES, pc + (h + 1) * LANES)
                wg = slice(2 * pc * p + h * LANES, 2 * pc * p + (h + 1) * LANES)
                wv = slice(2 * pc * p + pc + h * LANES, 2 * pc * p + pc + (h + 1) * LANES)
                tg = [cw_ref[j:j + 1, wg] for j in range(FFN_CONV)]
                tv = [cw_ref[j:j + 1, wv] for j in range(FFN_CONV)]
                gate = _conv_taps(u[rows_in, cg], tg) + cw_ref[FFN_CONV:FFN_CONV + 1, wg]
                val = _conv_taps(u[rows_in, cv], tv) + cw_ref[FFN_CONV:FFN_CONV + 1, wv]
                hs[r * rb:(r + 1) * rb, pc * p + h * LANES:pc * p + (h + 1) * LANES] = (
                    gate * _sigmoid(gate) * val).astype(BF16)

    up(0)
    for p in range(1, n_pieces):
        up(p)
        gate_piece(p - 1)
    gate_piece(n_pieces - 1)
    ffn = _dot(hs[...], wdn_ref[...])
    o_ref[0] = _layernorm_rows(DN_ALPHA * x_ref[0] + ffn, g_ref[...], b_ref[...])


def _ffn(x3, wup_r, cw_r, wdn_r, g, b, tm):
    bsz, s, d = x3.shape
    hb = tm // HALO
    const = lambda a: pl.BlockSpec(a.shape, lambda bi, i: (0, 0), pipeline_mode=pl.Buffered(1))
    return pl.pallas_call(
        functools.partial(_ffn_kernel, tm=tm, rb=128),
        grid=(bsz, s // tm),
        in_specs=[pl.BlockSpec((1, tm, d), lambda bi, i: (bi, i, 0)),
                  pl.BlockSpec((1, HALO, d), lambda bi, i: (bi, jnp.maximum(i * hb - 1, 0), 0)),
                  const(wup_r), const(cw_r), const(wdn_r), const(g), const(b)],
        out_specs=pl.BlockSpec((1, tm, d), lambda bi, i: (bi, i, 0)),
        out_shape=jax.ShapeDtypeStruct((bsz, s, d), F32),
        scratch_shapes=[pltpu.VMEM((HALO + tm, d), BF16),
                        pltpu.VMEM((2, HALO + tm, 2 * FFN_PIECE), F32),
                        pltpu.VMEM((tm, D_FF), BF16)],
        compiler_params=_cparams(2),
        name="conv_ffn_ln",
    )(x3, x3, wup_r, cw_r, wdn_r, g, b)


def _ffn_weights(w_up, conv_w, conv_b, w_down):
    n_pieces = D_FF // FFN_PIECE
    arr = lambda a: a.reshape(a.shape[0], 2, n_pieces, FFN_PIECE).transpose(0, 2, 1, 3).reshape(a.shape[0], 2 * D_FF)
    wup_r = arr(w_up).astype(BF16)
    cw = jnp.concatenate([conv_w, conv_b[None, :], jnp.zeros((8 - FFN_CONV - 1, 2 * D_FF), F32)], 0)
    return wup_r, arr(cw), w_down.astype(BF16)


ODD_PIECE = 512


def _odd_in_kernel(x_ref, xh_ref, w_ref, cw_ref, ws_ref, o_ref, os_ref, xb, us, *, tm, conv_cols, rb):
    i = pl.program_id(1)
    _fill_xb(xb, x_ref, xh_ref, i)
    pc = ODD_PIECE
    n_pieces = w_ref.shape[1] // pc

    def up(p):
        us[p % 2] = _dot(xb[...], w_ref[:, pc * p:pc * (p + 1)])

    def finish(p):
        u = us.at[p % 2]
        if pc * p >= conv_cols:
            o_ref[0, :, pc * p:pc * (p + 1)] = u[HALO:, :].astype(o_ref.dtype)
            return
        for r in range(tm // rb):
            rows_in = pl.ds(r * rb + HALO - 8, rb + 8)
            for h in range(pc // LANES):
                cs = slice(h * LANES, (h + 1) * LANES)
                ws = slice(pc * p + h * LANES, pc * p + (h + 1) * LANES)
                taps = [cw_ref[j:j + 1, ws] for j in range(GDN_CONV)]
                y = _conv_taps(u[rows_in, cs], taps)
                o_ref[0, r * rb:(r + 1) * rb, ws] = (y * _sigmoid(y)).astype(o_ref.dtype)

    up(0)
    os_ref[0] = _dot(xb[HALO:, :], ws_ref[...])
    for p in range(1, n_pieces):
        up(p)
        finish(p - 1)
    finish(n_pieces - 1)


def _odd_in(x3, w_main, cw, w_small, tm):
    bsz, s, d = x3.shape
    n = w_main.shape[1]
    hb = tm // HALO
    const = lambda a: pl.BlockSpec(a.shape, lambda bi, i: (0, 0), pipeline_mode=pl.Buffered(1))
    return pl.pallas_call(
        functools.partial(_odd_in_kernel, tm=tm, conv_cols=cw.shape[1], rb=128),
        grid=(bsz, s // tm),
        in_specs=[pl.BlockSpec((1, tm, d), lambda bi, i: (bi, i, 0)),
                  pl.BlockSpec((1, HALO, d), lambda bi, i: (bi, jnp.maximum(i * hb - 1, 0), 0)),
                  const(w_main), const(cw), const(w_small)],
        out_specs=[pl.BlockSpec((1, tm, n), lambda bi, i: (bi, i, 0)),
                   pl.BlockSpec((1, tm, LANES), lambda bi, i: (bi, i, 0))],
        out_shape=[jax.ShapeDtypeStruct((bsz, s, n), BF16),
                   jax.ShapeDtypeStruct((bsz, s, LANES), F32)],
        scratch_shapes=[pltpu.VMEM((HALO + tm, d), BF16),
                        pltpu.VMEM((2, HALO + tm, ODD_PIECE), F32)],
        compiler_params=_cparams(2),
        name="od_in_proj_conv",
    )(x3, x3, w_main, cw, w_small)


GDN_WIDE = 4
GDN_BLK = GDN_CHUNK * GDN_WIDE
GDN_HG = 4


def _split_bf16(a):
    hi = a.astype(BF16)
    return hi, (a - hi.astype(F32)).astype(BF16)


def _gdn_kernel(prm_ref, q_ref, k_ref, v_ref, gt_ref, sm_ref, nw_ref, o_ref,
                mq_s, nn_s, op_s, eg_s, st_s, rhs_s, kd_s, qd_s, at_s, *, seq):
    hg = pl.program_id(1)
    c, wd, blk = GDN_CHUNK, GDN_WIDE, GDN_BLK
    ww = c * wd
    n_blocks = seq // blk
    n_chunks = seq // c
    shift = int(math.log2(c))

    r_w = lax.broadcasted_iota(jnp.int32, (c, ww), 0)
    l_w = lax.broadcasted_iota(jnp.int32, (c, ww), 1)
    j_w = l_w & (c - 1)
    chunk_of_lane = l_w >> shift
    tri_w = r_w >= j_w
    strict_w = r_w > j_w
    upper_w = jnp.where(r_w <= j_w, 1.0, 0.0).astype(F32)
    bmask = (lax.broadcasted_iota(jnp.int32, (ww, ww), 0) >> shift) == (lax.broadcasted_iota(jnp.int32, (ww, ww), 1) >> shift)
    ltri = jnp.where(lax.broadcasted_iota(jnp.int32, (c, c), 0) >= lax.broadcasted_iota(jnp.int32, (c, c), 1),
                     1.0, 0.0).astype(BF16)
    ones_cc = jnp.ones((c, c), BF16)
    lane_blk = lax.broadcasted_iota(jnp.int32, (blk, LANES), 1)
    first_half = lax.broadcasted_iota(jnp.int32, (c, LANES), 1) < c
    a_neg = -jnp.exp(prm_ref[0:1, :])
    dtb = prm_ref[1:2, :]

    def bdiag(xw):
        return jnp.where(bmask, jnp.concatenate([xw] * wd, 0), 0.0).astype(BF16)

    def prep(i, carry):
        rows = pl.ds(pl.multiple_of(i * blk, blk), blk)
        sm = sm_ref[0, rows, :]
        beta_all = _sigmoid(sm)
        sp_in = sm + dtb
        g_all = a_neg * (jnp.maximum(sp_in, 0.0) + jnp.log1p(jnp.exp(-jnp.abs(sp_in))))
        pws = []
        for hh in range(GDN_HG):
            head = hg * GDN_HG + hh
            cols = slice(hh * GDN_DK, (hh + 1) * GDN_DK)
            q = q_ref[0, rows, cols].astype(F32)
            k = k_ref[0, rows, cols].astype(F32)
            v = v_ref[0, rows, cols].astype(F32)
            qn = q * lax.rsqrt(jnp.sum(q * q, -1, keepdims=True) + 1e-6) * (GDN_DK ** -0.5)
            kn = k * lax.rsqrt(jnp.sum(k * k, -1, keepdims=True) + 1e-6)
            beta = jnp.sum(jnp.where(lane_blk == head, beta_all, 0.0), -1, keepdims=True)
            g = jnp.sum(jnp.where(lane_blk == GDN_HEADS + head, g_all, 0.0), -1, keepdims=True)
            kb = kn * beta

            g_w = jnp.broadcast_to(g[0:c], (c, ww))
            for ch in range(1, wd):
                g_w = jnp.where(chunk_of_lane == ch, jnp.broadcast_to(g[ch * c:(ch + 1) * c], (c, ww)), g_w)
            g_hi, g_lo = _split_bf16(g_w)
            gc_col = _dot(ltri, g_hi) + _dot(ltri, g_lo)
            gu_hi, gu_lo = _split_bf16(g_w * upper_w)
            gc_row = _dot(ones_cc, gu_hi) + _dot(ones_cc, gu_lo)
            decay = jnp.where(tri_w, jnp.exp(jnp.where(tri_w, gc_col - gc_row, 0.0)), 0.0)

            a_parts, qk_parts = [], []
            for p in range(wd // 2):
                pr = slice(2 * p * c, (2 * p + 2) * c)
                gram = _dot_nt(jnp.concatenate([kb[pr], qn[pr]], 0).astype(BF16), kn[pr].astype(BF16))
                a_parts.append(jnp.where(first_half, gram[0:c], gram[c:2 * c]))
                qk_parts.append(jnp.where(first_half, gram[2 * c:3 * c], gram[3 * c:4 * c]))
            pws.append(-jnp.where(strict_w, jnp.concatenate(a_parts, 1) * decay, 0.0))
            at_s[hh] = jnp.where(tri_w, jnp.concatenate(qk_parts, 1) * decay, 0.0)

            gc_parts, gl_parts = [], []
            for p in range(wd // 2):
                x = gc_col[:, 2 * p * c:(2 * p + 2) * c]
                xr = pltpu.roll(x, c, 1)
                for part in (jnp.where(first_half, x, xr), jnp.where(first_half, xr, x)):
                    gc_parts.append(part)
                    gl_parts.append(jnp.broadcast_to(part[c - 1:c, :], (c, LANES)))
            gc_t = jnp.concatenate(gc_parts, 0)
            gl_t = jnp.concatenate(gl_parts, 0)
            egc = jnp.exp(gc_t)
            rhs_s[hh, :, 0:GDN_DK] = kb * egc
            rhs_s[hh, :, GDN_DK:] = v * beta
            kd_s[hh] = (kn * jnp.exp(gl_t - gc_t)).astype(BF16)
            qd_s[hh] = qn * egc
            for ch in range(wd):
                eg_s[hh, pl.ds(pl.multiple_of((i * wd + ch) * 8, 8), 8), :] = jnp.broadcast_to(
                    jnp.exp(gl_t[ch * c:ch * c + 1, :]), (8, LANES))

        ews = list(pws)
        for _ in range(shift - 1):
            pws = [_dot(pw.astype(BF16), bdiag(pw)) for pw in pws]
            ews = [e_w + pw + _dot(e_w.astype(BF16), bdiag(pw)) for e_w, pw in zip(ews, pws)]

        for hh in range(GDN_HG):
            rhs = rhs_s[hh]
            sol = rhs + _dot(bdiag(ews[hh]), rhs.astype(BF16))
            solb = sol.astype(BF16)
            asol = _dot(bdiag(at_s[hh]), solb)
            qp = (qd_s[hh] - asol[:, :GDN_DK]).astype(BF16)
            op_s[hh, rows, :] = asol[:, GDN_DK:]
            kd = kd_s[hh]
            for ch in range(wd):
                cr = slice(ch * c, (ch + 1) * c)
                ci = i * wd + ch
                mn = _dot_tn(kd[cr], solb[cr])
                mq_s[hh, ci, 0:GDN_DK, :] = mn[:, :GDN_DK].astype(BF16)
                mq_s[hh, ci, GDN_DK:GDN_DK + c, :] = qp[cr]
                nn_s[hh, ci] = mn[:, GDN_DK:]
        return carry

    lax.fori_loop(0, n_blocks, prep, 0)
    st_s[...] = jnp.zeros_like(st_s)

    def scan(i, carry):
        rows = pl.ds(pl.multiple_of(i * c, c), c)
        for hh in range(GDN_HG):
            cols = slice(hh * GDN_DK, (hh + 1) * GDN_DK)
            state = st_s[hh]
            x = _dot(mq_s[hh, i], state.astype(BF16))
            st_s[hh] = state * eg_s[hh, pl.ds(pl.multiple_of(i * 8, 8), 1), :] - x[:GDN_DK] + nn_s[hh, i]
            o = x[GDN_DK:] + op_s[hh, rows, :]
            y = o * lax.rsqrt(jnp.mean(o * o, -1, keepdims=True) + EPS) * nw_ref[...]
            gt = gt_ref[0, rows, cols].astype(F32)
            o_ref[0, rows, cols] = (y * (gt * _sigmoid(gt))).astype(o_ref.dtype)
        return carry

    lax.fori_loop(0, n_chunks, scan, 0)


def _gdn(qkvg, small, a_log, dt_bias, norm_w):
    b, s, _ = qkvg.shape
    gw = GDN_HG * GDN_DK
    groups = GDN_HEADS // GDN_HG
    n_chunks = s // GDN_CHUNK
    blk = lambda off: pl.BlockSpec((1, s, gw), lambda i, j: (i, 0, off + j))
    prm = jnp.zeros((8, LANES), F32)
    prm = prm.at[0, GDN_HEADS:2 * GDN_HEADS].set(a_log).at[1, GDN_HEADS:2 * GDN_HEADS].set(dt_bias)
    return pl.pallas_call(
        functools.partial(_gdn_kernel, seq=s),
        grid=(b, groups),
        in_specs=[pl.BlockSpec((8, LANES), lambda i, j: (0, 0)),
                  blk(0), blk(groups), blk(2 * groups), blk(3 * groups),
                  pl.BlockSpec((1, s, LANES), lambda i, j: (i, 0, 0)),
                  pl.BlockSpec((1, GDN_DV), lambda i, j: (0, 0))],
        out_specs=pl.BlockSpec((1, s, gw), lambda i, j: (i, 0, j)),
        out_shape=jax.ShapeDtypeStruct((b, s, GDN_W), BF16),
        scratch_shapes=[pltpu.VMEM((GDN_HG, n_chunks, GDN_DK + GDN_CHUNK, GDN_DK), BF16),
                        pltpu.VMEM((GDN_HG, n_chunks, GDN_DK, GDN_DV), F32),
                        pltpu.VMEM((GDN_HG, s, GDN_DV), F32),
                        pltpu.VMEM((GDN_HG, n_chunks * 8, LANES), F32),
                        pltpu.VMEM((GDN_HG, GDN_DK, GDN_DV), F32),
                        pltpu.VMEM((GDN_HG, GDN_BLK, GDN_DK + GDN_DV), F32),
                        pltpu.VMEM((GDN_HG, GDN_BLK, GDN_DK), BF16),
                        pltpu.VMEM((GDN_HG, GDN_BLK, GDN_DK), F32),
                        pltpu.VMEM((GDN_HG, GDN_CHUNK, GDN_BLK), F32)],
        compiler_params=_cparams(2),
        name="gated_delta_rule",
    )(prm, qkvg, qkvg, qkvg, qkvg, small, norm_w.reshape(1, GDN_DV))


def kernel(x, positions, ev_w_in, ev_w_out, od_w_in, od_conv_w, od_a_log, od_dt_bias, od_norm_w, od_w_out,
           ffn_w_up, ffn_conv_w, ffn_conv_b, ffn_w_down, ln1_g, ln1_b, ln2_g, ln2_b):
    b, s, d = x.shape
    ret_f, dil_f = _rope_freqs()
    pos3 = positions.reshape(b, s, 1)
    tab_ret = _trig_tables(pos3, ret_f, (True, False))
    tab_dil = _trig_tables(pos3, dil_f, (True, False, False))
    ret_consts = _ret_consts()
    row = lambda a: a.reshape(1, d)

    for layer in range(DEPTH):
        j = layer // 2
        if layer % 2 == 0:
            hproj = _proj(x, ev_w_in[j].astype(BF16), 1024, EV_IN // 4)
            ya = _retention(hproj, tab_ret, ret_consts)
            yb = _dilated(hproj, tab_dil)
            w_out = ev_w_out[j].astype(BF16)
            x = _out_ln(x, [ya, yb], [w_out[:RET_V_W], w_out[RET_V_W:]], row(ln1_g[layer]), row(ln1_b[layer]), 512)
        else:
            w_in = od_w_in[j]
            w_main = w_in[:, :4 * GDN_W].astype(BF16)
            w_small = jnp.pad(w_in[:, 4 * GDN_W:], ((0, 0), (0, LANES - 2 * GDN_HEADS))).astype(BF16)
            cw = jnp.pad(od_conv_w[j], ((0, 8 - GDN_CONV), (0, 0)))
            qkvg, small = _odd_in(x, w_main, cw, w_small, 512)
            yc = _gdn(qkvg, small, od_a_log[j], od_dt_bias[j], od_norm_w[j])
            x = _out_ln(x, [yc], [od_w_out[j].astype(BF16)], row(ln1_g[layer]), row(ln1_b[layer]), 512)
        wup_r, cw_r, wdn_r = _ffn_weights(ffn_w_up[layer], ffn_conv_w[layer], ffn_conv_b[layer], ffn_w_down[layer])
        x = _ffn(x, wup_r, cw_r, wdn_r, row(ln2_g[layer]), row(ln2_b[layer]), 512)
    return x
```

```python
import functools
import math

import jax
import jax.numpy as jnp
import numpy as np
from jax import lax
from jax.experimental import pallas as pl
from jax.experimental.pallas import tpu as pltpu

F32 = jnp.float32
BF16 = jnp.bfloat16

D_MODEL = 1024
DEPTH = 4
RET_HEADS, RET_DK, RET_DV, RET_CHUNK, RET_THETA = 4, 128, 256, 128, 10000.0
DIL_HEADS, DIL_HD, DIL_BLOCK = 8, 64, 128
DIL_DILATIONS = (1, 4, 16)
DIL_INTERLEAVE = 4
ROPE_THETA, ROPE_DIMS = 500000.0, DIL_HD // 4
GDN_HEADS, GDN_DK, GDN_DV, GDN_CHUNK, GDN_CONV = 8, 128, 128, 64, 4
D_FF, FFN_CONV = 2816, 3
DN_ALPHA = (2.0 * DEPTH) ** 0.25
EPS = 1e-5

RET_QK_W = RET_HEADS * RET_DK
RET_V_W = RET_HEADS * RET_DV
DIL_W = DIL_HEADS * DIL_HD
EV_IN = 2 * RET_QK_W + 2 * RET_V_W + 3 * DIL_W
GDN_W = GDN_HEADS * GDN_DK

LANES = 128
HALO = 16
ROW_TILE = 1024
VMEM_LIMIT = 56 * 1024 * 1024
HI = lax.Precision.HIGHEST


def _cparams(n_axes, vmem=VMEM_LIMIT):
    return pltpu.CompilerParams(dimension_semantics=("arbitrary",) * n_axes, vmem_limit_bytes=vmem)


def _dot(a, b):
    return jnp.dot(a, b, preferred_element_type=F32)


def _dot_nt(a, b):
    return lax.dot_general(a, b, (((1,), (1,)), ((), ())), preferred_element_type=F32)


def _dot_tn(a, b):
    return lax.dot_general(a, b, (((0,), (0,)), ((), ())), preferred_element_type=F32)


def _sigmoid(x):
    return 1.0 / (1.0 + jnp.exp(-x))


def _layernorm_rows(z, g, b):
    mu = jnp.mean(z, -1, keepdims=True)
    zc = z - mu
    var = jnp.mean(zc * zc, -1, keepdims=True)
    return zc * lax.rsqrt(var + EPS) * g + b


def _proj_kernel(x_ref, w_ref, o_ref, *, tn):
    xb = x_ref[0].astype(BF16)
    for j in range(w_ref.shape[1] // tn):
        cols = slice(j * tn, (j + 1) * tn)
        o_ref[0, :, cols] = _dot(xb, w_ref[:, cols]).astype(o_ref.dtype)


def _proj(x3, w_bf16, tm, tn):
    bsz, s, k = x3.shape
    n = w_bf16.shape[1]
    return pl.pallas_call(
        functools.partial(_proj_kernel, tn=tn),
        grid=(bsz, s // tm),
        in_specs=[pl.BlockSpec((1, tm, k), lambda bi, i: (bi, i, 0)),
                  pl.BlockSpec((k, n), lambda bi, i: (0, 0), pipeline_mode=pl.Buffered(1))],
        out_specs=pl.BlockSpec((1, tm, n), lambda bi, i: (bi, i, 0)),
        out_shape=jax.ShapeDtypeStruct((bsz, s, n), BF16),
        compiler_params=_cparams(2),
        name="ev_in_proj",
    )(x3, w_bf16)


def _trig_kernel(pos_ref, f_ref, o_ref, *, n_tab, cos_mask):
    rc = 256

    def body(i, carry):
        rows = pl.ds(pl.multiple_of(i * rc, rc), rc)
        pos = pos_ref[0, rows, :].astype(F32)
        for t in range(n_tab):
            ang = pos * f_ref[t:t + 1, :]
            o_ref[0, t, rows, :] = jnp.cos(ang) if cos_mask[t] else jnp.sin(ang)
        return carry

    lax.fori_loop(0, pos_ref.shape[1] // rc, body, 0)


def _trig_tables(pos3, freqs, cos_mask):
    b, s, _ = pos3.shape
    n_tab = len(cos_mask)
    fpad = jnp.zeros((8, LANES), F32).at[:n_tab].set(freqs)
    return pl.pallas_call(
        functools.partial(_trig_kernel, n_tab=n_tab, cos_mask=cos_mask),
        grid=(b,),
        in_specs=[pl.BlockSpec((1, s, 1), lambda i: (i, 0, 0)),
                  pl.BlockSpec((8, LANES), lambda i: (0, 0))],
        out_specs=pl.BlockSpec((1, n_tab, s, LANES), lambda i: (i, 0, 0, 0)),
        out_shape=jax.ShapeDtypeStruct((b, n_tab, s, LANES), F32),
        compiler_params=_cparams(1),
        name="rope_tables",
    )(pos3, fpad)


def _rope_freqs():
    half = RET_DK // 2
    inv = jnp.power(RET_THETA, -jnp.arange(half, dtype=F32) * 2.0 / RET_DK)
    ret = jnp.stack([jnp.concatenate([inv, inv]), jnp.concatenate([-inv, inv])])
    hh = ROPE_DIMS // 2
    invd = jnp.power(ROPE_THETA, -jnp.arange(hh, dtype=F32) * 2.0 / ROPE_DIMS)
    z = jnp.zeros((DIL_HD - ROPE_DIMS,), F32)
    zh = jnp.zeros((hh,), F32)
    f_cos = jnp.concatenate([invd, invd, z])
    f_sin = jnp.concatenate([-invd, invd, z])
    dil = jnp.stack([jnp.tile(f_cos, 2), jnp.tile(f_sin, 2)])
    return ret, dil


def _ret_kernel(gch_ref, q_ref, k_ref, v_ref, g_ref, tab_ref, dm_ref, zt_ref, xi_ref, o_ref, r_ref, *, n_chunks):
    r_ref[...] = jnp.zeros_like(r_ref)
    c = RET_CHUNK
    heads = range(RET_HEADS)

    def body(ci, carry):
        rows = pl.ds(pl.multiple_of(ci * c, c), c)
        cosr = tab_ref[0, 0, rows, :]
        sinr = tab_ref[0, 1, rows, :]
        qk = lambda h: slice(h * RET_DK, (h + 1) * RET_DK)
        vv = lambda h: slice(h * RET_DV, (h + 1) * RET_DV)
        qs = [q_ref[0, rows, qk(h)].astype(F32) for h in heads]
        ks = [k_ref[0, rows, qk(h)].astype(F32) for h in heads]
        qr = [q * cosr + pltpu.roll(q, RET_DK // 2, 1) * sinr for q in qs]
        kr = [(k * cosr + pltpu.roll(k, RET_DK // 2, 1) * sinr) * (RET_DK ** -0.5) for k in ks]
        vs = [v_ref[0, rows, vv(h)] for h in heads]
        r_prev = [r_ref[h] for h in heads]
        scores = [_dot_nt(qr[h].astype(BF16), kr[h].astype(BF16)) * dm_ref[h] for h in heads]
        inter = [_dot((qr[h] * xi_ref[h]).astype(BF16), r_prev[h].astype(BF16)) for h in heads]
        kv = [_dot_tn((kr[h] * zt_ref[h]).astype(BF16), vs[h]) for h in heads]
        os_ = [_dot(scores[h].astype(BF16), vs[h]) + inter[h] for h in heads]
        for h in heads:
            r_ref[h] = r_prev[h] * gch_ref[h] + kv[h]
            o = os_[h]
            y = o * lax.rsqrt(jnp.mean(o * o, -1, keepdims=True) + EPS)
            g = g_ref[0, rows, vv(h)].astype(F32)
            o_ref[0, rows, vv(h)] = (y * (g * _sigmoid(g))).astype(o_ref.dtype)
        return carry

    lax.fori_loop(0, n_chunks, body, 0)


def _ret_consts():
    hh = np.arange(RET_HEADS, dtype=np.float64)
    lg = np.log1p(-np.power(2.0, -5.0 - hh))
    idx = np.arange(RET_CHUNK, dtype=np.float64)
    rel = idx[:, None] - idx[None, :]
    dmat = np.where(rel >= 0, np.exp(np.maximum(rel, 0.0) * lg[:, None, None]), 0.0)
    zeta = np.exp((RET_CHUNK - 1 - idx) * lg[:, None])
    xi = np.exp((idx + 1) * lg[:, None])
    gch = np.exp(RET_CHUNK * lg)
    bc = lambda a: np.broadcast_to(a[:, :, None], (RET_HEADS, RET_CHUNK, RET_DK))
    return (jnp.asarray(gch, F32), jnp.asarray(dmat, F32), jnp.asarray(bc(zeta), F32), jnp.asarray(bc(xi), F32))


def _retention(hproj, tab_ret, consts):
    b, s, _ = hproj.shape
    gch, dmat, zeta, xi = consts
    v_off = 2 * RET_QK_W // RET_V_W
    all_heads = pl.BlockSpec((RET_HEADS, RET_CHUNK, RET_DK), lambda i: (0, 0, 0))
    return pl.pallas_call(
        functools.partial(_ret_kernel, n_chunks=s // RET_CHUNK),
        grid=(b,),
        in_specs=[pl.BlockSpec(memory_space=pltpu.SMEM),
                  pl.BlockSpec((1, s, RET_QK_W), lambda i: (i, 0, 0)),
                  pl.BlockSpec((1, s, RET_QK_W), lambda i: (i, 0, 1)),
                  pl.BlockSpec((1, s, RET_V_W), lambda i: (i, 0, v_off)),
                  pl.BlockSpec((1, s, RET_V_W), lambda i: (i, 0, v_off + 1)),
                  pl.BlockSpec((1, 2, s, LANES), lambda i: (i, 0, 0, 0)),
                  all_heads, all_heads, all_heads],
        out_specs=pl.BlockSpec((1, s, RET_V_W), lambda i: (i, 0, 0)),
        out_shape=jax.ShapeDtypeStruct((b, s, RET_V_W), BF16),
        scratch_shapes=[pltpu.VMEM((RET_HEADS, RET_DK, RET_DV), F32)],
        compiler_params=_cparams(1),
        name="retention",
    )(gch, hproj, hproj, hproj, hproj, tab_ret, dmat, zeta, xi)


def _dil_kernel(q_ref, k_ref, v_ref, tab_ref, o_ref, qs, ks, vs, acc, ms, ls, *, seq):
    rc = 256
    lane = lax.broadcasted_iota(jnp.int32, (DIL_BLOCK, LANES), 1)
    first_head = lane < DIL_HD

    def rot_body(i, carry):
        rows = pl.ds(pl.multiple_of(i * rc, rc), rc)
        cosd = tab_ref[0, 0, rows, :]
        sind = tab_ref[0, 1, rows, :]
        hh = ROPE_DIMS // 2
        low_half = (lax.broadcasted_iota(jnp.int32, (rc, LANES), 1) & (DIL_HD - 1)) < hh

        def rot(x):
            return x * cosd + jnp.where(low_half, pltpu.roll(x, LANES - hh, 1), pltpu.roll(x, hh, 1)) * sind

        qs[rows, :] = rot(q_ref[0, rows, :].astype(F32)) * (DIL_HD ** -0.5)
        ks[rows, :] = rot(k_ref[0, rows, :].astype(F32))
        vs[rows, :] = v_ref[0, rows, :].astype(F32)
        return carry

    lax.fori_loop(0, seq // rc, rot_body, 0)

    def blocks(g, d, kb_rows, specs):
        def sl(start, n):
            return pl.ds(start, n) if d == 1 else pl.ds(start, n, stride=d)

        rel = (lax.broadcasted_iota(jnp.int32, (DIL_BLOCK, kb_rows), 0)
               - lax.broadcasted_iota(jnp.int32, (DIL_BLOCK, kb_rows), 1))
        qsl = [sl(q0, DIL_BLOCK) for q0, _, _ in specs]
        ksl = [sl(k0, kb_rows) for _, k0, _ in specs]
        qbs = [qs[s_, :] for s_ in qsl]
        kbs = [ks[s_, :].astype(BF16) for s_ in ksl]
        vbs = [vs[s_, :].astype(BF16) for s_ in ksl]
        valid = []
        for _, _, off in specs:
            dist = rel + off
            valid.append((dist >= 0) & (dist <= DIL_BLOCK))
        heads = (first_head, jnp.logical_not(first_head))
        sc = [[jnp.where(valid[b_], _dot_nt(jnp.where(sel, qbs[b_], 0.0).astype(BF16), kbs[b_]), -1e30)
               for sel in heads] for b_ in range(len(specs))]
        mx = [[jnp.max(s_, -1, keepdims=True) for s_ in row_] for row_ in sc]
        pr = [[jnp.exp(s_ - m_) for s_, m_ in zip(rs, rm)] for rs, rm in zip(sc, mx)]
        sm = [[jnp.sum(p_, -1, keepdims=True) for p_ in row_] for row_ in pr]
        pv = [[_dot(p_.astype(BF16), vbs[b_]) for p_ in pr[b_]] for b_ in range(len(specs))]
        for b_ in range(len(specs)):
            acc[g, qsl[b_], :] = jnp.where(first_head, pv[b_][0], pv[b_][1])
            ms[g, qsl[b_], :] = jnp.where(first_head, mx[b_][0], mx[b_][1])
            ls[g, qsl[b_], :] = jnp.where(first_head, sm[b_][0], sm[b_][1])

    nbi = DIL_INTERLEAVE
    for g, d in enumerate(DIL_DILATIONS):
        nb = seq // d // DIL_BLOCK
        span = DIL_BLOCK * d
        if nb == 1:
            def body(i, carry, g=g, d=d):
                blocks(g, d, DIL_BLOCK, [(i * nbi + u, i * nbi + u, 0) for u in range(nbi)])
                return carry
            lax.fori_loop(0, d // nbi, body, 0)
        elif d >= nbi:
            def body(i, carry, g=g, d=d, span=span):
                n = i // (d // nbi)
                r0 = (i % (d // nbi)) * nbi
                kn = jnp.maximum(n - 1, 0)
                off = (n - kn) * DIL_BLOCK
                blocks(g, d, 2 * DIL_BLOCK, [(r0 + u + n * span, r0 + u + kn * span, off) for u in range(nbi)])
                return carry
            lax.fori_loop(0, nb * d // nbi, body, 0)
        else:
            def body(i, carry, g=g, d=d, span=span):
                specs = []
                for u in range(nbi):
                    n = i * nbi + u
                    kn = jnp.maximum(n - 1, 0)
                    specs.append((n * span, kn * span, (n - kn) * DIL_BLOCK))
                blocks(g, d, 2 * DIL_BLOCK, specs)
                return carry
            lax.fori_loop(0, nb // nbi, body, 0)

    def comb_body(i, carry):
        rows = pl.ds(pl.multiple_of(i * rc, rc), rc)
        m0, m1, m2 = ms[0, rows, :], ms[1, rows, :], ms[2, rows, :]
        mmax = jnp.maximum(jnp.maximum(m0, m1), m2)
        e0, e1, e2 = jnp.exp(m0 - mmax), jnp.exp(m1 - mmax), jnp.exp(m2 - mmax)
        num = e0 * acc[0, rows, :] + e1 * acc[1, rows, :] + e2 * acc[2, rows, :]
        den = e0 * ls[0, rows, :] + e1 * ls[1, rows, :] + e2 * ls[2, rows, :]
        o_ref[0, rows, :] = (num / den).astype(o_ref.dtype)
        return carry

    lax.fori_loop(0, seq // rc, comb_body, 0)


def _dilated(hproj, tab_dil):
    b, s, _ = hproj.shape
    base = (2 * RET_QK_W + 2 * RET_V_W) // LANES
    pairs = DIL_W // LANES
    blk = lambda off: pl.BlockSpec((1, s, LANES), lambda i, j: (i, 0, base + off + j))
    n_g = len(DIL_DILATIONS)
    return pl.pallas_call(
        functools.partial(_dil_kernel, seq=s),
        grid=(b, pairs),
        in_specs=[blk(0), blk(pairs), blk(2 * pairs),
                  pl.BlockSpec((1, 2, s, LANES), lambda i, j: (i, 0, 0, 0))],
        out_specs=pl.BlockSpec((1, s, LANES), lambda i, j: (i, 0, j)),
        out_shape=jax.ShapeDtypeStruct((b, s, DIL_W), BF16),
        scratch_shapes=[pltpu.VMEM((s, LANES), F32)] * 3 + [pltpu.VMEM((n_g, s, LANES), F32)] * 3,
        compiler_params=_cparams(2),
        name="dilated_attention",
    )(hproj, hproj, hproj, tab_dil)


def _out_ln_kernel(*refs, n_in):
    x_ref = refs[0]
    ys = refs[1:1 + n_in]
    ws = refs[1 + n_in:1 + 2 * n_in]
    g_ref, b_ref, o_ref = refs[1 + 2 * n_in:]
    z = DN_ALPHA * x_ref[0]
    for y_ref, w_ref in zip(ys, ws):
        z = z + _dot(y_ref[0], w_ref[...])
    o_ref[0] = _layernorm_rows(z, g_ref[...], b_ref[...])


def _out_ln(x3, ys, ws, g, b, tm):
    bsz, s, d = x3.shape
    n_in = len(ys)
    row = lambda w: pl.BlockSpec((1, tm, w), lambda bi, i: (bi, i, 0))
    full = lambda a: pl.BlockSpec(a.shape, lambda bi, i: (0, 0))
    return pl.pallas_call(
        functools.partial(_out_ln_kernel, n_in=n_in),
        grid=(bsz, s // tm),
        in_specs=[row(d)] + [row(y.shape[2]) for y in ys] + [full(w) for w in ws] + [full(g), full(b)],
        out_specs=row(d),
        out_shape=jax.ShapeDtypeStruct((bsz, s, d), F32),
        compiler_params=_cparams(2),
        name="out_proj_ln",
    )(x3, *ys, *ws, g, b)


def _conv_taps(ub, taps):
    k_w = len(taps)
    out = ub[8:] * taps[k_w - 1]
    for back in range(1, k_w):
        out = out + pltpu.roll(ub, back, 0)[8:] * taps[k_w - 1 - back]
    return out


def _fill_xb(xb, x_ref, xh_ref, i):
    xb[HALO:, :] = x_ref[0].astype(BF16)
    xb[0:HALO, :] = jnp.where(i > 0, xh_ref[0], 0.0).astype(BF16)


FFN_PIECE = 256


def _ffn_kernel(x_ref, xh_ref, wup_ref, cw_ref, wdn_ref, g_ref, b_ref, o_ref, xb, us, hs, *, tm, rb):
    i = pl.program_id(1)
    _fill_xb(xb, x_ref, xh_ref, i)
    pc = FFN_PIECE
    n_pieces = D_FF // pc

    def up(p):
        x = xb[...]
        us[p % 2, :, 0:pc] = _dot(x, wup_ref[:, pc * p:pc * (p + 1)])
        us[p % 2, :, pc:2 * pc] = _dot(x, wup_ref[:, D_FF + pc * p:D_FF + pc * (p + 1)])

    def gate_piece(p):
        u = us.at[p % 2]
        for r in range(tm // rb):
            rows_in = pl.ds(r * rb + HALO - 8, rb + 8)
            for h in range(pc // LANES):
                cg = slice(h * LANES, (h + 1) * LANES)
                cv = slice(pc + h * LANES, pc + (h + 1) * LANES)
                wg = slice(pc * p + h * LANES, pc * p + (h + 1) * LANES)
                wv = slice(D_FF + pc * p + h * LANES, D_FF + pc * p + (h + 1) * LANES)
                tg = [cw_ref[j:j + 1, wg] for j in range(FFN_CONV)]
                tv = [cw_ref[j:j + 1, wv] for j in range(FFN_CONV)]
                gate = _conv_taps(u[rows_in, cg], tg) + cw_ref[FFN_CONV:FFN_CONV + 1, wg]
                val = _conv_taps(u[rows_in, cv], tv) + cw_ref[FFN_CONV:FFN_CONV + 1, wv]
                hs[r * rb:(r + 1) * rb, wg] = (gate * _sigmoid(gate) * val).astype(BF16)

    up(0)
    for p in range(1, n_pieces):
        up(p)
        gate_piece(p - 1)
    gate_piece(n_pieces - 1)
    ffn = _dot(hs[...], wdn_ref[...])
    o_ref[0] = _layernorm_rows(DN_ALPHA * x_ref[0] + ffn, g_ref[...], b_ref[...])


def _ffn(x3, wup_r, cw_r, wdn_r, g, b, tm):
    bsz, s, d = x3.shape
    hb = tm // HALO
    const = lambda a: pl.BlockSpec(a.shape, lambda bi, i: (0, 0), pipeline_mode=pl.Buffered(1))
    return pl.pallas_call(
        functools.partial(_ffn_kernel, tm=tm, rb=128),
        grid=(bsz, s // tm),
        in_specs=[pl.BlockSpec((1, tm, d), lambda bi, i: (bi, i, 0)),
                  pl.BlockSpec((1, HALO, d), lambda bi, i: (bi, jnp.maximum(i * hb - 1, 0), 0)),
                  const(wup_r), const(cw_r), const(wdn_r), const(g), const(b)],
        out_specs=pl.BlockSpec((1, tm, d), lambda bi, i: (bi, i, 0)),
        out_shape=jax.ShapeDtypeStruct((bsz, s, d), F32),
        scratch_shapes=[pltpu.VMEM((HALO + tm, d), BF16),
                        pltpu.VMEM((2, HALO + tm, 2 * FFN_PIECE), F32),
                        pltpu.VMEM((tm, D_FF), BF16)],
        compiler_params=_cparams(2),
        name="conv_ffn_ln",
    )(x3, x3, wup_r, cw_r, wdn_r, g, b)


def _ffn_weights(w_up, conv_w, conv_b, w_down):
    cw = jnp.concatenate([conv_w, conv_b[None, :], jnp.zeros((8 - FFN_CONV - 1, 2 * D_FF), F32)], 0)
    return w_up.astype(BF16), cw, w_down.astype(BF16)


ODD_PIECE = 512


def _odd_in_kernel(x_ref, xh_ref, w_ref, cw_ref, ws_ref, o_ref, os_ref, xb, us, *, tm, conv_cols, rb):
    i = pl.program_id(1)
    _fill_xb(xb, x_ref, xh_ref, i)
    pc = ODD_PIECE
    n_pieces = w_ref.shape[1] // pc

    def up(p):
        us[p % 2] = _dot(xb[...], w_ref[:, pc * p:pc * (p + 1)])

    def finish(p):
        u = us.at[p % 2]
        if pc * p >= conv_cols:
            o_ref[0, :, pc * p:pc * (p + 1)] = u[HALO:, :].astype(o_ref.dtype)
            return
        for r in range(tm // rb):
            rows_in = pl.ds(r * rb + HALO - 8, rb + 8)
            for h in range(pc // LANES):
                cs = slice(h * LANES, (h + 1) * LANES)
                ws = slice(pc * p + h * LANES, pc * p + (h + 1) * LANES)
                taps = [cw_ref[j:j + 1, ws] for j in range(GDN_CONV)]
                y = _conv_taps(u[rows_in, cs], taps)
                y = y * _sigmoid(y)
                if pc * p + h * LANES < 2 * GDN_W:
                    y = y * lax.rsqrt(jnp.sum(y * y, -1, keepdims=True) + 1e-6)
                    if pc * p + h * LANES < GDN_W:
                        y = y * (GDN_DK ** -0.5)
                o_ref[0, r * rb:(r + 1) * rb, ws] = y.astype(o_ref.dtype)

    up(0)
    os_ref[0] = _dot(xb[HALO:, :], ws_ref[...])
    for p in range(1, n_pieces):
        up(p)
        finish(p - 1)
    finish(n_pieces - 1)


def _odd_in(x3, w_main, cw, w_small, tm):
    bsz, s, d = x3.shape
    n = w_main.shape[1]
    hb = tm // HALO
    const = lambda a: pl.BlockSpec(a.shape, lambda bi, i: (0, 0), pipeline_mode=pl.Buffered(1))
    return pl.pallas_call(
        functools.partial(_odd_in_kernel, tm=tm, conv_cols=cw.shape[1], rb=128),
        grid=(bsz, s // tm),
        in_specs=[pl.BlockSpec((1, tm, d), lambda bi, i: (bi, i, 0)),
                  pl.BlockSpec((1, HALO, d), lambda bi, i: (bi, jnp.maximum(i * hb - 1, 0), 0)),
                  const(w_main), const(cw), const(w_small)],
        out_specs=[pl.BlockSpec((1, tm, n), lambda bi, i: (bi, i, 0)),
                   pl.BlockSpec((1, tm, LANES), lambda bi, i: (bi, i, 0))],
        out_shape=[jax.ShapeDtypeStruct((bsz, s, n), BF16),
                   jax.ShapeDtypeStruct((bsz, s, LANES), F32)],
        scratch_shapes=[pltpu.VMEM((HALO + tm, d), BF16),
                        pltpu.VMEM((2, HALO + tm, ODD_PIECE), F32)],
        compiler_params=_cparams(2),
        name="od_in_proj_conv",
    )(x3, x3, w_main, cw, w_small)


GDN_WIDE = 4
GDN_BLK = GDN_CHUNK * GDN_WIDE
GDN_HG = 4


def _split_bf16(a):
    hi = a.astype(BF16)
    return hi, (a - hi.astype(F32)).astype(BF16)


def _gdn_kernel(prm_ref, q_ref, k_ref, v_ref, gt_ref, sm_ref, nw_ref, o_ref,
                mq_s, nn_s, op_s, eg_s, st_s, rhs_s, kd_s, qd_s, at_s, *, seq):
    hg = pl.program_id(1)
    c, wd, blk = GDN_CHUNK, GDN_WIDE, GDN_BLK
    ww = c * wd
    n_blocks = seq // blk
    n_chunks = seq // c
    shift = int(math.log2(c))

    r_w = lax.broadcasted_iota(jnp.int32, (c, ww), 0)
    l_w = lax.broadcasted_iota(jnp.int32, (c, ww), 1)
    j_w = l_w & (c - 1)
    chunk_of_lane = l_w >> shift
    tri_w = r_w >= j_w
    strict_w = r_w > j_w
    upper_w = jnp.where(r_w <= j_w, 1.0, 0.0).astype(F32)
    bmask = (lax.broadcasted_iota(jnp.int32, (ww, ww), 0) >> shift) == (lax.broadcasted_iota(jnp.int32, (ww, ww), 1) >> shift)
    ltri = jnp.where(lax.broadcasted_iota(jnp.int32, (c, c), 0) >= lax.broadcasted_iota(jnp.int32, (c, c), 1),
                     1.0, 0.0).astype(BF16)
    ones_cc = jnp.ones((c, c), BF16)
    lane_blk = lax.broadcasted_iota(jnp.int32, (blk, LANES), 1)
    first_half = lax.broadcasted_iota(jnp.int32, (c, LANES), 1) < c
    a_neg = -jnp.exp(prm_ref[0:1, :])
    dtb = prm_ref[1:2, :]

    def bdiag(xw):
        return jnp.where(bmask, jnp.concatenate([xw] * wd, 0), 0.0).astype(BF16)

    def prep(i, carry):
        rows = pl.ds(pl.multiple_of(i * blk, blk), blk)
        sm = sm_ref[0, rows, :]
        beta_all = _sigmoid(sm)
        sp_in = sm + dtb
        g_all = a_neg * (jnp.maximum(sp_in, 0.0) + jnp.log1p(jnp.exp(-jnp.abs(sp_in))))
        pws = []
        for hh in range(GDN_HG):
            head = hg * GDN_HG + hh
            cols = slice(hh * GDN_DK, (hh + 1) * GDN_DK)
            qn = q_ref[0, rows, cols].astype(F32)
            kn = k_ref[0, rows, cols].astype(F32)
            v = v_ref[0, rows, cols].astype(F32)
            beta = jnp.sum(jnp.where(lane_blk == head, beta_all, 0.0), -1, keepdims=True)
            g = jnp.sum(jnp.where(lane_blk == GDN_HEADS + head, g_all, 0.0), -1, keepdims=True)
            kb = kn * beta

            g_w = jnp.broadcast_to(g[0:c], (c, ww))
            for ch in range(1, wd):
                g_w = jnp.where(chunk_of_lane == ch, jnp.broadcast_to(g[ch * c:(ch + 1) * c], (c, ww)), g_w)
            g_hi, g_lo = _split_bf16(g_w)
            gc_col = _dot(ltri, g_hi) + _dot(ltri, g_lo)
            gu_hi, gu_lo = _split_bf16(g_w * upper_w)
            gc_row = _dot(ones_cc, gu_hi) + _dot(ones_cc, gu_lo)
            decay = jnp.where(tri_w, jnp.exp(jnp.where(tri_w, gc_col - gc_row, 0.0)), 0.0)

            a_parts, qk_parts = [], []
            for p in range(wd // 2):
                pr = slice(2 * p * c, (2 * p + 2) * c)
                gram = _dot_nt(jnp.concatenate([kb[pr], qn[pr]], 0).astype(BF16), kn[pr].astype(BF16))
                a_parts.append(jnp.where(first_half, gram[0:c], gram[c:2 * c]))
                qk_parts.append(jnp.where(first_half, gram[2 * c:3 * c], gram[3 * c:4 * c]))
            pws.append(-jnp.where(strict_w, jnp.concatenate(a_parts, 1) * decay, 0.0))
            at_s[hh] = jnp.where(tri_w, jnp.concatenate(qk_parts, 1) * decay, 0.0)

            gc_parts, gl_parts = [], []
            for p in range(wd // 2):
                x = gc_col[:, 2 * p * c:(2 * p + 2) * c]
                xr = pltpu.roll(x, c, 1)
                for part in (jnp.where(first_half, x, xr), jnp.where(first_half, xr, x)):
                    gc_parts.append(part)
                    gl_parts.append(jnp.broadcast_to(part[c - 1:c, :], (c, LANES)))
            gc_t = jnp.concatenate(gc_parts, 0)
            gl_t = jnp.concatenate(gl_parts, 0)
            egc = jnp.exp(gc_t)
            rhs_s[hh, :, 0:GDN_DK] = kb * egc
            rhs_s[hh, :, GDN_DK:] = v * beta
            kd_s[hh] = (kn * jnp.exp(gl_t - gc_t)).astype(BF16)
            qd_s[hh] = qn * egc
            for ch in range(wd):
                eg_s[hh, pl.ds(pl.multiple_of((i * wd + ch) * 8, 8), 8), :] = jnp.broadcast_to(
                    jnp.exp(gl_t[ch * c:ch * c + 1, :]), (8, LANES))

        ews = list(pws)
        for _ in range(shift - 1):
            pws = [_dot(pw.astype(BF16), bdiag(pw)) for pw in pws]
            ews = [e_w + pw + _dot(e_w.astype(BF16), bdiag(pw)) for e_w, pw in zip(ews, pws)]

        for hh in range(GDN_HG):
            rhs = rhs_s[hh]
            sol = rhs + _dot(bdiag(ews[hh]), rhs.astype(BF16))
            solb = sol.astype(BF16)
            asol = _dot(bdiag(at_s[hh]), solb)
            qp = (qd_s[hh] - asol[:, :GDN_DK]).astype(BF16)
            op_s[hh, rows, :] = asol[:, GDN_DK:]
            kd = kd_s[hh]
            for ch in range(wd):
                cr = slice(ch * c, (ch + 1) * c)
                ci = i * wd + ch
                mn = _dot_tn(kd[cr], solb[cr])
                mq_s[hh, ci, 0:GDN_DK, :] = mn[:, :GDN_DK].astype(BF16)
                mq_s[hh, ci, GDN_DK:GDN_DK + c, :] = qp[cr]
                nn_s[hh, ci] = mn[:, GDN_DK:]
        return carry

    lax.fori_loop(0, n_blocks, prep, 0)
    st_s[...] = jnp.zeros_like(st_s)

    def scan(i, carry):
        rows = pl.ds(pl.multiple_of(i * c, c), c)
        for hh in range(GDN_HG):
            cols = slice(hh * GDN_DK, (hh + 1) * GDN_DK)
            state = st_s[hh]
            x = _dot(mq_s[hh, i], state.astype(BF16))
            st_s[hh] = state * eg_s[hh, pl.ds(pl.multiple_of(i * 8, 8), 1), :] - x[:GDN_DK] + nn_s[hh, i]
            o = x[GDN_DK:] + op_s[hh, rows, :]
            y = o * lax.rsqrt(jnp.mean(o * o, -1, keepdims=True) + EPS) * nw_ref[...]
            gt = gt_ref[0, rows, cols].astype(F32)
            o_ref[0, rows, cols] = (y * (gt * _sigmoid(gt))).astype(o_ref.dtype)
        return carry

    lax.fori_loop(0, n_chunks, scan, 0)


def _gdn(qkvg, small, a_log, dt_bias, norm_w):
    b, s, _ = qkvg.shape
    gw = GDN_HG * GDN_DK
    groups = GDN_HEADS // GDN_HG
    n_chunks = s // GDN_CHUNK
    blk = lambda off: pl.BlockSpec((1, s, gw), lambda i, j: (i, 0, off + j))
    prm = jnp.zeros((8, LANES), F32)
    prm = prm.at[0, GDN_HEADS:2 * GDN_HEADS].set(a_log).at[1, GDN_HEADS:2 * GDN_HEADS].set(dt_bias)
    return pl.pallas_call(
        functools.partial(_gdn_kernel, seq=s),
        grid=(b, groups),
        in_specs=[pl.BlockSpec((8, LANES), lambda i, j: (0, 0)),
                  blk(0), blk(groups), blk(2 * groups), blk(3 * groups),
                  pl.BlockSpec((1, s, LANES), lambda i, j: (i, 0, 0)),
                  pl.BlockSpec((1, GDN_DV), lambda i, j: (0, 0))],
        out_specs=pl.BlockSpec((1, s, gw), lambda i, j: (i, 0, j)),
        out_shape=jax.ShapeDtypeStruct((b, s, GDN_W), BF16),
        scratch_shapes=[pltpu.VMEM((GDN_HG, n_chunks, GDN_DK + GDN_CHUNK, GDN_DK), BF16),
                        pltpu.VMEM((GDN_HG, n_chunks, GDN_DK, GDN_DV), F32),
                        pltpu.VMEM((GDN_HG, s, GDN_DV), F32),
                        pltpu.VMEM((GDN_HG, n_chunks * 8, LANES), F32),
                        pltpu.VMEM((GDN_HG, GDN_DK, GDN_DV), F32),
                        pltpu.VMEM((GDN_HG, GDN_BLK, GDN_DK + GDN_DV), F32),
                        pltpu.VMEM((GDN_HG, GDN_BLK, GDN_DK), BF16),
                        pltpu.VMEM((GDN_HG, GDN_BLK, GDN_DK), F32),
                        pltpu.VMEM((GDN_HG, GDN_CHUNK, GDN_BLK), F32)],
        compiler_params=_cparams(2),
        name="gated_delta_rule",
    )(prm, qkvg, qkvg, qkvg, qkvg, small, norm_w.reshape(1, GDN_DV))


def kernel(x, positions, ev_w_in, ev_w_out, od_w_in, od_conv_w, od_a_log, od_dt_bias, od_norm_w, od_w_out,
           ffn_w_up, ffn_conv_w, ffn_conv_b, ffn_w_down, ln1_g, ln1_b, ln2_g, ln2_b):
    b, s, d = x.shape
    ret_f, dil_f = _rope_freqs()
    pos3 = positions.reshape(b, s, 1)
    tab_ret = _trig_tables(pos3, ret_f, (True, False))
    tab_dil = _trig_tables(pos3, dil_f, (True, False))
    ret_consts = _ret_consts()
    row = lambda a: a.reshape(1, d)

    for layer in range(DEPTH):
        j = layer // 2
        if layer % 2 == 0:
            hproj = _proj(x, ev_w_in[j].astype(BF16), ROW_TILE, 512)
            ya = _retention(hproj, tab_ret, ret_consts)
            yb = _dilated(hproj, tab_dil)
            w_out = ev_w_out[j].astype(BF16)
            x = _out_ln(x, [ya, yb], [w_out[:RET_V_W], w_out[RET_V_W:]], row(ln1_g[layer]), row(ln1_b[layer]),
                        ROW_TILE)
        else:
            w_in = od_w_in[j]
            w_main = w_in[:, :4 * GDN_W].astype(BF16)
            w_small = jnp.pad(w_in[:, 4 * GDN_W:], ((0, 0), (0, LANES - 2 * GDN_HEADS))).astype(BF16)
            cw = jnp.pad(od_conv_w[j], ((0, 8 - GDN_CONV), (0, 0)))
            qkvg, small = _odd_in(x, w_main, cw, w_small, ROW_TILE)
            yc = _gdn(qkvg, small, od_a_log[j], od_dt_bias[j], od_norm_w[j])
            x = _out_ln(x, [yc], [od_w_out[j].astype(BF16)], row(ln1_g[layer]), row(ln1_b[layer]), ROW_TILE)
        wup_r, cw_r, wdn_r = _ffn_weights(ffn_w_up[layer], ffn_conv_w[layer], ffn_conv_b[layer], ffn_w_down[layer])
        x = _ffn(x, wup_r, cw_r, wdn_r, row(ln2_g[layer]), row(ln2_b[layer]), ROW_TILE)
    return x
```

```python
import functools
import math

import jax
import jax.numpy as jnp
import numpy as np
from jax import lax
from jax.experimental import pallas as pl
from jax.experimental.pallas import tpu as pltpu

F32 = jnp.float32
BF16 = jnp.bfloat16

D_MODEL = 1024
DEPTH = 4
RET_HEADS, RET_DK, RET_DV, RET_CHUNK, RET_THETA = 4, 128, 256, 128, 10000.0
DIL_HEADS, DIL_HD, DIL_BLOCK = 8, 64, 128
DIL_DILATIONS = (1, 4, 16)
DIL_INTERLEAVE = 4
ROPE_THETA, ROPE_DIMS = 500000.0, DIL_HD // 4
GDN_HEADS, GDN_DK, GDN_DV, GDN_CHUNK, GDN_CONV = 8, 128, 128, 64, 4
D_FF, FFN_CONV = 2816, 3
DN_ALPHA = (2.0 * DEPTH) ** 0.25
EPS = 1e-5

RET_QK_W = RET_HEADS * RET_DK
RET_V_W = RET_HEADS * RET_DV
DIL_W = DIL_HEADS * DIL_HD
EV_IN = 2 * RET_QK_W + 2 * RET_V_W + 3 * DIL_W
GDN_W = GDN_HEADS * GDN_DK

LANES = 128
HALO = 16
ROW_TILE = 1024
VMEM_LIMIT = 56 * 1024 * 1024
HI = lax.Precision.HIGHEST


def _cparams(n_axes, vmem=VMEM_LIMIT):
    return pltpu.CompilerParams(dimension_semantics=("arbitrary",) * n_axes, vmem_limit_bytes=vmem)


def _dot(a, b):
    return jnp.dot(a, b, preferred_element_type=F32)


def _dot_nt(a, b):
    return lax.dot_general(a, b, (((1,), (1,)), ((), ())), preferred_element_type=F32)


def _dot_tn(a, b):
    return lax.dot_general(a, b, (((0,), (0,)), ((), ())), preferred_element_type=F32)


def _sigmoid(x):
    return 1.0 / (1.0 + jnp.exp(-x))


def _layernorm_rows(z, g, b):
    mu = jnp.mean(z, -1, keepdims=True)
    zc = z - mu
    var = jnp.mean(zc * zc, -1, keepdims=True)
    return zc * lax.rsqrt(var + EPS) * g + b


def _proj_kernel(x_ref, w_ref, o_ref, *, tn):
    xb = x_ref[0].astype(BF16)
    for j in range(w_ref.shape[1] // tn):
        cols = slice(j * tn, (j + 1) * tn)
        o_ref[0, :, cols] = _dot(xb, w_ref[:, cols]).astype(o_ref.dtype)


def _proj(x3, w_bf16, tm, tn):
    bsz, s, k = x3.shape
    n = w_bf16.shape[1]
    return pl.pallas_call(
        functools.partial(_proj_kernel, tn=tn),
        grid=(bsz, s // tm),
        in_specs=[pl.BlockSpec((1, tm, k), lambda bi, i: (bi, i, 0)),
                  pl.BlockSpec((k, n), lambda bi, i: (0, 0), pipeline_mode=pl.Buffered(1))],
        out_specs=pl.BlockSpec((1, tm, n), lambda bi, i: (bi, i, 0)),
        out_shape=jax.ShapeDtypeStruct((bsz, s, n), BF16),
        compiler_params=_cparams(2),
        name="ev_in_proj",
    )(x3, w_bf16)


def _trig_kernel(pos_ref, f_ref, o_ref, *, n_tab, cos_mask):
    rc = 256

    def body(i, carry):
        rows = pl.ds(pl.multiple_of(i * rc, rc), rc)
        pos = pos_ref[0, rows, :].astype(F32)
        for t in range(n_tab):
            ang = pos * f_ref[t:t + 1, :]
            o_ref[0, t, rows, :] = jnp.cos(ang) if cos_mask[t] else jnp.sin(ang)
        return carry

    lax.fori_loop(0, pos_ref.shape[1] // rc, body, 0)


def _trig_tables(pos3, freqs, cos_mask):
    b, s, _ = pos3.shape
    n_tab = len(cos_mask)
    fpad = jnp.zeros((8, LANES), F32).at[:n_tab].set(freqs)
    return pl.pallas_call(
        functools.partial(_trig_kernel, n_tab=n_tab, cos_mask=cos_mask),
        grid=(b,),
        in_specs=[pl.BlockSpec((1, s, 1), lambda i: (i, 0, 0)),
                  pl.BlockSpec((8, LANES), lambda i: (0, 0))],
        out_specs=pl.BlockSpec((1, n_tab, s, LANES), lambda i: (i, 0, 0, 0)),
        out_shape=jax.ShapeDtypeStruct((b, n_tab, s, LANES), F32),
        compiler_params=_cparams(1),
        name="rope_tables",
    )(pos3, fpad)


def _rope_freqs():
    half = RET_DK // 2
    inv = jnp.power(RET_THETA, -jnp.arange(half, dtype=F32) * 2.0 / RET_DK)
    ret = jnp.stack([jnp.concatenate([inv, inv]), jnp.concatenate([-inv, inv])])
    hh = ROPE_DIMS // 2
    invd = jnp.power(ROPE_THETA, -jnp.arange(hh, dtype=F32) * 2.0 / ROPE_DIMS)
    z = jnp.zeros((DIL_HD - ROPE_DIMS,), F32)
    zh = jnp.zeros((hh,), F32)
    f_cos = jnp.concatenate([invd, invd, z])
    f_sin = jnp.concatenate([-invd, invd, z])
    dil = jnp.stack([jnp.tile(f_cos, 2), jnp.tile(f_sin, 2)])
    return ret, dil


def _ret_kernel(gch_ref, q_ref, k_ref, v_ref, g_ref, tab_ref, dm_ref, zt_ref, xi_ref, o_ref, r_ref, *, n_chunks):
    r_ref[...] = jnp.zeros_like(r_ref)
    c = RET_CHUNK
    heads = range(RET_HEADS)

    def body(ci, carry):
        rows = pl.ds(pl.multiple_of(ci * c, c), c)
        cosr = tab_ref[0, 0, rows, :]
        sinr = tab_ref[0, 1, rows, :]
        qk = lambda h: slice(h * RET_DK, (h + 1) * RET_DK)
        vv = lambda h: slice(h * RET_DV, (h + 1) * RET_DV)
        qs = [q_ref[0, rows, qk(h)].astype(F32) for h in heads]
        ks = [k_ref[0, rows, qk(h)].astype(F32) for h in heads]
        qr = [q * cosr + pltpu.roll(q, RET_DK // 2, 1) * sinr for q in qs]
        kr = [(k * cosr + pltpu.roll(k, RET_DK // 2, 1) * sinr) * (RET_DK ** -0.5) for k in ks]
        vs = [v_ref[0, rows, vv(h)] for h in heads]
        r_prev = [r_ref[h] for h in heads]
        scores = [_dot_nt(qr[h].astype(BF16), kr[h].astype(BF16)) * dm_ref[h] for h in heads]
        inter = [_dot((qr[h] * xi_ref[h]).astype(BF16), r_prev[h].astype(BF16)) for h in heads]
        kv = [_dot_tn((kr[h] * zt_ref[h]).astype(BF16), vs[h]) for h in heads]
        os_ = [_dot(scores[h].astype(BF16), vs[h]) + inter[h] for h in heads]
        for h in heads:
            r_ref[h] = r_prev[h] * gch_ref[h] + kv[h]
            o = os_[h]
            y = o * lax.rsqrt(jnp.mean(o * o, -1, keepdims=True) + EPS)
            g = g_ref[0, rows, vv(h)].astype(F32)
            o_ref[0, rows, vv(h)] = (y * (g * _sigmoid(g))).astype(o_ref.dtype)
        return carry

    lax.fori_loop(0, n_chunks, body, 0)


def _ret_consts():
    hh = np.arange(RET_HEADS, dtype=np.float64)
    lg = np.log1p(-np.power(2.0, -5.0 - hh))
    idx = np.arange(RET_CHUNK, dtype=np.float64)
    rel = idx[:, None] - idx[None, :]
    dmat = np.where(rel >= 0, np.exp(np.maximum(rel, 0.0) * lg[:, None, None]), 0.0)
    zeta = np.exp((RET_CHUNK - 1 - idx) * lg[:, None])
    xi = np.exp((idx + 1) * lg[:, None])
    gch = np.exp(RET_CHUNK * lg)
    bc = lambda a: np.broadcast_to(a[:, :, None], (RET_HEADS, RET_CHUNK, RET_DK))
    return (jnp.asarray(gch, F32), jnp.asarray(dmat, F32), jnp.asarray(bc(zeta), F32), jnp.asarray(bc(xi), F32))


def _retention(hproj, tab_ret, consts):
    b, s, _ = hproj.shape
    gch, dmat, zeta, xi = consts
    v_off = 2 * RET_QK_W // RET_V_W
    all_heads = pl.BlockSpec((RET_HEADS, RET_CHUNK, RET_DK), lambda i: (0, 0, 0))
    return pl.pallas_call(
        functools.partial(_ret_kernel, n_chunks=s // RET_CHUNK),
        grid=(b,),
        in_specs=[pl.BlockSpec(memory_space=pltpu.SMEM),
                  pl.BlockSpec((1, s, RET_QK_W), lambda i: (i, 0, 0)),
                  pl.BlockSpec((1, s, RET_QK_W), lambda i: (i, 0, 1)),
                  pl.BlockSpec((1, s, RET_V_W), lambda i: (i, 0, v_off)),
                  pl.BlockSpec((1, s, RET_V_W), lambda i: (i, 0, v_off + 1)),
                  pl.BlockSpec((1, 2, s, LANES), lambda i: (i, 0, 0, 0)),
                  all_heads, all_heads, all_heads],
        out_specs=pl.BlockSpec((1, s, RET_V_W), lambda i: (i, 0, 0)),
        out_shape=jax.ShapeDtypeStruct((b, s, RET_V_W), BF16),
        scratch_shapes=[pltpu.VMEM((RET_HEADS, RET_DK, RET_DV), F32)],
        compiler_params=_cparams(1),
        name="retention",
    )(gch, hproj, hproj, hproj, hproj, tab_ret, dmat, zeta, xi)


def _dil_kernel(q_ref, k_ref, v_ref, tab_ref, o_ref, qs, ks, vs, acc, ms, ls, *, seq):
    rc = 256
    lane = lax.broadcasted_iota(jnp.int32, (DIL_BLOCK, LANES), 1)
    first_head = lane < DIL_HD

    def rot_body(i, carry):
        rows = pl.ds(pl.multiple_of(i * rc, rc), rc)
        cosd = tab_ref[0, 0, rows, :]
        sind = tab_ref[0, 1, rows, :]
        hh = ROPE_DIMS // 2
        low_half = (lax.broadcasted_iota(jnp.int32, (rc, LANES), 1) & (DIL_HD - 1)) < hh

        def rot(x):
            return x * cosd + jnp.where(low_half, pltpu.roll(x, LANES - hh, 1), pltpu.roll(x, hh, 1)) * sind

        qs[rows, :] = rot(q_ref[0, rows, :].astype(F32)) * (DIL_HD ** -0.5)
        ks[rows, :] = rot(k_ref[0, rows, :].astype(F32))
        vs[rows, :] = v_ref[0, rows, :].astype(F32)
        return carry

    lax.fori_loop(0, seq // rc, rot_body, 0)

    def blocks(g, d, kb_rows, specs):
        def sl(start, n):
            return pl.ds(start, n) if d == 1 else pl.ds(start, n, stride=d)

        rel = (lax.broadcasted_iota(jnp.int32, (DIL_BLOCK, kb_rows), 0)
               - lax.broadcasted_iota(jnp.int32, (DIL_BLOCK, kb_rows), 1))
        qsl = [sl(q0, DIL_BLOCK) for q0, _, _ in specs]
        ksl = [sl(k0, kb_rows) for _, k0, _ in specs]
        qbs = [qs[s_, :] for s_ in qsl]
        kbs = [ks[s_, :].astype(BF16) for s_ in ksl]
        vbs = [vs[s_, :].astype(BF16) for s_ in ksl]
        valid = []
        for _, _, off in specs:
            dist = rel + off
            valid.append((dist >= 0) & (dist <= DIL_BLOCK))
        heads = (first_head, jnp.logical_not(first_head))
        sc = [[jnp.where(valid[b_], _dot_nt(jnp.where(sel, qbs[b_], 0.0).astype(BF16), kbs[b_]), -1e30)
               for sel in heads] for b_ in range(len(specs))]
        mx = [[jnp.max(s_, -1, keepdims=True) for s_ in row_] for row_ in sc]
        pr = [[jnp.exp(s_ - m_) for s_, m_ in zip(rs, rm)] for rs, rm in zip(sc, mx)]
        sm = [[jnp.sum(p_, -1, keepdims=True) for p_ in row_] for row_ in pr]
        pv = [[_dot(p_.astype(BF16), vbs[b_]) for p_ in pr[b_]] for b_ in range(len(specs))]
        for b_ in range(len(specs)):
            acc[g, qsl[b_], :] = jnp.where(first_head, pv[b_][0], pv[b_][1])
            ms[g, qsl[b_], :] = jnp.where(first_head, mx[b_][0], mx[b_][1])
            ls[g, qsl[b_], :] = jnp.where(first_head, sm[b_][0], sm[b_][1])

    nbi = DIL_INTERLEAVE
    for g, d in enumerate(DIL_DILATIONS):
        nb = seq // d // DIL_BLOCK
        span = DIL_BLOCK * d
        if nb == 1:
            def body(i, carry, g=g, d=d):
                blocks(g, d, DIL_BLOCK, [(i * nbi + u, i * nbi + u, 0) for u in range(nbi)])
                return carry
            lax.fori_loop(0, d // nbi, body, 0)
        elif d >= nbi:
            def body(i, carry, g=g, d=d, span=span):
                n = i // (d // nbi)
                r0 = (i % (d // nbi)) * nbi
                kn = jnp.maximum(n - 1, 0)
                off = (n - kn) * DIL_BLOCK
                blocks(g, d, 2 * DIL_BLOCK, [(r0 + u + n * span, r0 + u + kn * span, off) for u in range(nbi)])
                return carry
            lax.fori_loop(0, nb * d // nbi, body, 0)
        else:
            def body(i, carry, g=g, d=d, span=span):
                specs = []
                for u in range(nbi):
                    n = i * nbi + u
                    kn = jnp.maximum(n - 1, 0)
                    specs.append((n * span, kn * span, (n - kn) * DIL_BLOCK))
                blocks(g, d, 2 * DIL_BLOCK, specs)
                return carry
            lax.fori_loop(0, nb // nbi, body, 0)

    def comb_body(i, carry):
        rows = pl.ds(pl.multiple_of(i * rc, rc), rc)
        m0, m1, m2 = ms[0, rows, :], ms[1, rows, :], ms[2, rows, :]
        mmax = jnp.maximum(jnp.maximum(m0, m1), m2)
        e0, e1, e2 = jnp.exp(m0 - mmax), jnp.exp(m1 - mmax), jnp.exp(m2 - mmax)
        num = e0 * acc[0, rows, :] + e1 * acc[1, rows, :] + e2 * acc[2, rows, :]
        den = e0 * ls[0, rows, :] + e1 * ls[1, rows, :] + e2 * ls[2, rows, :]
        o_ref[0, rows, :] = (num / den).astype(o_ref.dtype)
        return carry

    lax.fori_loop(0, seq // rc, comb_body, 0)


def _dilated(hproj, tab_dil):
    b, s, _ = hproj.shape
    base = (2 * RET_QK_W + 2 * RET_V_W) // LANES
    pairs = DIL_W // LANES
    blk = lambda off: pl.BlockSpec((1, s, LANES), lambda i, j: (i, 0, base + off + j))
    n_g = len(DIL_DILATIONS)
    return pl.pallas_call(
        functools.partial(_dil_kernel, seq=s),
        grid=(b, pairs),
        in_specs=[blk(0), blk(pairs), blk(2 * pairs),
                  pl.BlockSpec((1, 2, s, LANES), lambda i, j: (i, 0, 0, 0))],
        out_specs=pl.BlockSpec((1, s, LANES), lambda i, j: (i, 0, j)),
        out_shape=jax.ShapeDtypeStruct((b, s, DIL_W), BF16),
        scratch_shapes=[pltpu.VMEM((s, LANES), F32)] * 3 + [pltpu.VMEM((n_g, s, LANES), F32)] * 3,
        compiler_params=_cparams(2),
        name="dilated_attention",
    )(hproj, hproj, hproj, tab_dil)


def _out_ln_kernel(*refs, n_in):
    x_ref = refs[0]
    ys = refs[1:1 + n_in]
    ws = refs[1 + n_in:1 + 2 * n_in]
    g_ref, b_ref, o_ref = refs[1 + 2 * n_in:]
    z = DN_ALPHA * x_ref[0]
    for y_ref, w_ref in zip(ys, ws):
        z = z + _dot(y_ref[0], w_ref[...])
    o_ref[0] = _layernorm_rows(z, g_ref[...], b_ref[...])


def _out_ln(x3, ys, ws, g, b, tm):
    bsz, s, d = x3.shape
    n_in = len(ys)
    row = lambda w: pl.BlockSpec((1, tm, w), lambda bi, i: (bi, i, 0))
    full = lambda a: pl.BlockSpec(a.shape, lambda bi, i: (0, 0))
    return pl.pallas_call(
        functools.partial(_out_ln_kernel, n_in=n_in),
        grid=(bsz, s // tm),
        in_specs=[row(d)] + [row(y.shape[2]) for y in ys] + [full(w) for w in ws] + [full(g), full(b)],
        out_specs=row(d),
        out_shape=jax.ShapeDtypeStruct((bsz, s, d), F32),
        compiler_params=_cparams(2),
        name="out_proj_ln",
    )(x3, *ys, *ws, g, b)


def _conv_taps(u_ref, row0, rows, cols, taps):
    k_w = len(taps)
    ub = u_ref[row0 - 8:row0 + rows, cols]
    out = ub[8:] * taps[k_w - 1]
    for back in range(1, k_w):
        out = out + pltpu.roll(ub, back, 0)[8:] * taps[k_w - 1 - back]
    return out


def _fill_xb(xb, x_ref, xh_ref, i):
    xb[HALO:, :] = x_ref[0].astype(BF16)
    xb[0:HALO, :] = jnp.where(i > 0, xh_ref[0], 0.0).astype(BF16)


FFN_PIECE = 256


def _ffn_kernel(x_ref, xh_ref, wup_ref, cw_ref, wdn_ref, g_ref, b_ref, o_ref, xb, us, hs, *, tm, rb):
    i = pl.program_id(1)
    _fill_xb(xb, x_ref, xh_ref, i)
    pc = FFN_PIECE
    n_pieces = D_FF // pc

    def up(p):
        x = xb[...]
        us[p % 2, :, 0:pc] = _dot(x, wup_ref[:, pc * p:pc * (p + 1)])
        us[p % 2, :, pc:2 * pc] = _dot(x, wup_ref[:, D_FF + pc * p:D_FF + pc * (p + 1)])

    def gate_piece(p):
        u = us.at[p % 2]
        for r in range(tm // rb):
            row0 = r * rb + HALO
            for h in range(pc // LANES):
                cg = slice(h * LANES, (h + 1) * LANES)
                cv = slice(pc + h * LANES, pc + (h + 1) * LANES)
                wg = slice(pc * p + h * LANES, pc * p + (h + 1) * LANES)
                wv = slice(D_FF + pc * p + h * LANES, D_FF + pc * p + (h + 1) * LANES)
                tg = [cw_ref[j:j + 1, wg] for j in range(FFN_CONV)]
                tv = [cw_ref[j:j + 1, wv] for j in range(FFN_CONV)]
                gate = _conv_taps(u, row0, rb, cg, tg) + cw_ref[FFN_CONV:FFN_CONV + 1, wg]
                val = _conv_taps(u, row0, rb, cv, tv) + cw_ref[FFN_CONV:FFN_CONV + 1, wv]
                hs[r * rb:(r + 1) * rb, wg] = (gate * _sigmoid(gate) * val).astype(BF16)

    up(0)
    for p in range(1, n_pieces):
        up(p)
        gate_piece(p - 1)
    gate_piece(n_pieces - 1)
    ffn = _dot(hs[...], wdn_ref[...])
    o_ref[0] = _layernorm_rows(DN_ALPHA * x_ref[0] + ffn, g_ref[...], b_ref[...])


def _ffn(x3, wup_r, cw_r, wdn_r, g, b, tm):
    bsz, s, d = x3.shape
    hb = tm // HALO
    const = lambda a: pl.BlockSpec(a.shape, lambda bi, i: (0, 0), pipeline_mode=pl.Buffered(1))
    return pl.pallas_call(
        functools.partial(_ffn_kernel, tm=tm, rb=128),
        grid=(bsz, s // tm),
        in_specs=[pl.BlockSpec((1, tm, d), lambda bi, i: (bi, i, 0)),
                  pl.BlockSpec((1, HALO, d), lambda bi, i: (bi, jnp.maximum(i * hb - 1, 0), 0)),
                  const(wup_r), const(cw_r), const(wdn_r), const(g), const(b)],
        out_specs=pl.BlockSpec((1, tm, d), lambda bi, i: (bi, i, 0)),
        out_shape=jax.ShapeDtypeStruct((bsz, s, d), F32),
        scratch_shapes=[pltpu.VMEM((HALO + tm, d), BF16),
                        pltpu.VMEM((2, HALO + tm, 2 * FFN_PIECE), F32),
                        pltpu.VMEM((tm, D_FF), BF16)],
        compiler_params=_cparams(2),
        name="conv_ffn_ln",
    )(x3, x3, wup_r, cw_r, wdn_r, g, b)


def _ffn_weights(w_up, conv_w, conv_b, w_down):
    cw = jnp.concatenate([conv_w, conv_b[None, :], jnp.zeros((8 - FFN_CONV - 1, 2 * D_FF), F32)], 0)
    return w_up.astype(BF16), cw, w_down.astype(BF16)


ODD_PIECE = 512


def _odd_in_kernel(x_ref, xh_ref, w_ref, cw_ref, ws_ref, o_ref, os_ref, xb, us, *, tm, conv_cols, rb):
    i = pl.program_id(1)
    _fill_xb(xb, x_ref, xh_ref, i)
    pc = ODD_PIECE
    n_pieces = w_ref.shape[1] // pc

    def up(p):
        us[p % 2] = _dot(xb[...], w_ref[:, pc * p:pc * (p + 1)])

    def finish(p):
        u = us.at[p % 2]
        if pc * p >= conv_cols:
            o_ref[0, :, pc * p:pc * (p + 1)] = u[HALO:, :].astype(o_ref.dtype)
            return
        for r in range(tm // rb):
            for h in range(pc // LANES):
                cs = slice(h * LANES, (h + 1) * LANES)
                ws = slice(pc * p + h * LANES, pc * p + (h + 1) * LANES)
                taps = [cw_ref[j:j + 1, ws] for j in range(GDN_CONV)]
                y = _conv_taps(u, r * rb + HALO, rb, cs, taps)
                y = y * _sigmoid(y)
                if pc * p + h * LANES < 2 * GDN_W:
                    y = y * lax.rsqrt(jnp.sum(y * y, -1, keepdims=True) + 1e-6)
                    if pc * p + h * LANES < GDN_W:
                        y = y * (GDN_DK ** -0.5)
                o_ref[0, r * rb:(r + 1) * rb, ws] = y.astype(o_ref.dtype)

    up(0)
    os_ref[0] = _dot(xb[HALO:, :], ws_ref[...])
    for p in range(1, n_pieces):
        up(p)
        finish(p - 1)
    finish(n_pieces - 1)


def _odd_in(x3, w_main, cw, w_small, tm):
    bsz, s, d = x3.shape
    n = w_main.shape[1]
    hb = tm // HALO
    const = lambda a: pl.BlockSpec(a.shape, lambda bi, i: (0, 0), pipeline_mode=pl.Buffered(1))
    return pl.pallas_call(
        functools.partial(_odd_in_kernel, tm=tm, conv_cols=cw.shape[1], rb=128),
        grid=(bsz, s // tm),
        in_specs=[pl.BlockSpec((1, tm, d), lambda bi, i: (bi, i, 0)),
                  pl.BlockSpec((1, HALO, d), lambda bi, i: (bi, jnp.maximum(i * hb - 1, 0), 0)),
                  const(w_main), const(cw), const(w_small)],
        out_specs=[pl.BlockSpec((1, tm, n), lambda bi, i: (bi, i, 0)),
                   pl.BlockSpec((1, tm, LANES), lambda bi, i: (bi, i, 0))],
        out_shape=[jax.ShapeDtypeStruct((bsz, s, n), BF16),
                   jax.ShapeDtypeStruct((bsz, s, LANES), F32)],
        scratch_shapes=[pltpu.VMEM((HALO + tm, d), BF16),
                        pltpu.VMEM((2, HALO + tm, ODD_PIECE), F32)],
        compiler_params=_cparams(2),
        name="od_in_proj_conv",
    )(x3, x3, w_main, cw, w_small)


GDN_WIDE = 4
GDN_BLK = GDN_CHUNK * GDN_WIDE
GDN_HG = 8
GDN_LOCKSTEP = 2


def _split_bf16(a):
    hi = a.astype(BF16)
    return hi, (a - hi.astype(F32)).astype(BF16)


def _gdn_kernel(prm_ref, q_ref, k_ref, v_ref, gt_ref, sm_ref, nw_ref, o_ref,
                mq_s, nn_s, op_s, eg_s, st_s, rhs_s, kd_s, qd_s, at_s, *, seq):
    hg = pl.program_id(1)
    c, wd, blk = GDN_CHUNK, GDN_WIDE, GDN_BLK
    ww = c * wd
    n_blocks = seq // blk
    n_chunks = seq // c
    shift = int(math.log2(c))

    r_w = lax.broadcasted_iota(jnp.int32, (c, ww), 0)
    l_w = lax.broadcasted_iota(jnp.int32, (c, ww), 1)
    j_w = l_w & (c - 1)
    chunk_of_lane = l_w >> shift
    tri_w = r_w >= j_w
    strict_w = r_w > j_w
    upper_w = jnp.where(r_w <= j_w, 1.0, 0.0).astype(F32)
    bmask = (lax.broadcasted_iota(jnp.int32, (ww, ww), 0) >> shift) == (lax.broadcasted_iota(jnp.int32, (ww, ww), 1) >> shift)
    ltri = jnp.where(lax.broadcasted_iota(jnp.int32, (c, c), 0) >= lax.broadcasted_iota(jnp.int32, (c, c), 1),
                     1.0, 0.0).astype(BF16)
    ones_cc = jnp.ones((c, c), BF16)
    lane_blk = lax.broadcasted_iota(jnp.int32, (blk, LANES), 1)
    first_half = lax.broadcasted_iota(jnp.int32, (c, LANES), 1) < c
    a_neg = -jnp.exp(prm_ref[0:1, :])
    dtb = prm_ref[1:2, :]

    def bdiag(xw):
        return jnp.where(bmask, jnp.concatenate([xw] * wd, 0), 0.0).astype(BF16)

    hs_all = list(range(GDN_HG))

    def widen(col):
        out = jnp.broadcast_to(col[0:c], (c, ww))
        for ch in range(1, wd):
            out = jnp.where(chunk_of_lane == ch, jnp.broadcast_to(col[ch * c:(ch + 1) * c], (c, ww)), out)
        return out

    def prep_steps(i, slot, hs_):
        rows = pl.ds(pl.multiple_of(i * blk, blk), blk)
        sm = sm_ref[0, rows, :]
        beta_all = _sigmoid(sm)
        sp_in = sm + dtb
        g_all = a_neg * (jnp.maximum(sp_in, 0.0) + jnp.log1p(jnp.exp(-jnp.abs(sp_in))))
        cols = {hh: slice(hh * GDN_DK, (hh + 1) * GDN_DK) for hh in hs_}
        head = {hh: hg * GDN_HG + hh for hh in hs_}
        qn = {hh: q_ref[0, rows, cols[hh]].astype(F32) for hh in hs_}
        kn = {hh: k_ref[0, rows, cols[hh]].astype(F32) for hh in hs_}
        beta = {hh: jnp.sum(jnp.where(lane_blk == head[hh], beta_all, 0.0), -1, keepdims=True) for hh in hs_}
        g = {hh: jnp.sum(jnp.where(lane_blk == GDN_HEADS + head[hh], g_all, 0.0), -1, keepdims=True) for hh in hs_}
        kb = {hh: kn[hh] * beta[hh] for hh in hs_}
        g_w = {hh: widen(g[hh]) for hh in hs_}
        g_sp = {hh: _split_bf16(g_w[hh]) for hh in hs_}
        gu_sp = {hh: _split_bf16(g_w[hh] * upper_w) for hh in hs_}
        gc_col = {hh: _dot(ltri, g_sp[hh][0]) + _dot(ltri, g_sp[hh][1]) for hh in hs_}
        gc_row = {hh: _dot(ones_cc, gu_sp[hh][0]) + _dot(ones_cc, gu_sp[hh][1]) for hh in hs_}
        yield
        grams = {hh: [_dot_nt(jnp.concatenate([kb[hh][2 * p * c:(2 * p + 2) * c], qn[hh][2 * p * c:(2 * p + 2) * c]], 0)
                              .astype(BF16), kn[hh][2 * p * c:(2 * p + 2) * c].astype(BF16))
                      for p in range(wd // 2)] for hh in hs_}
        yield
        pws = {}
        for hh in hs_:
            decay = jnp.where(tri_w, jnp.exp(jnp.where(tri_w, gc_col[hh] - gc_row[hh], 0.0)), 0.0)
            a_w = jnp.concatenate([jnp.where(first_half, gm[0:c], gm[c:2 * c]) for gm in grams[hh]], 1)
            qk_w = jnp.concatenate([jnp.where(first_half, gm[2 * c:3 * c], gm[3 * c:4 * c]) for gm in grams[hh]], 1)
            pws[hh] = -jnp.where(strict_w, a_w * decay, 0.0)
            at_s[hh] = jnp.where(tri_w, qk_w * decay, 0.0)
        ews = dict(pws)
        pws = {hh: _dot(pws[hh].astype(BF16), bdiag(pws[hh])) for hh in hs_}
        for hh in hs_:
            gc_parts, gl_parts = [], []
            for p in range(wd // 2):
                x = gc_col[hh][:, 2 * p * c:(2 * p + 2) * c]
                xr = pltpu.roll(x, c, 1)
                for part in (jnp.where(first_half, x, xr), jnp.where(first_half, xr, x)):
                    gc_parts.append(part)
                    gl_parts.append(jnp.broadcast_to(part[c - 1:c, :], (c, LANES)))
            gc_t = jnp.concatenate(gc_parts, 0)
            gl_t = jnp.concatenate(gl_parts, 0)
            egc = jnp.exp(gc_t)
            rhs_s[hh, :, 0:GDN_DK] = kb[hh] * egc
            rhs_s[hh, :, GDN_DK:] = v_ref[0, rows, cols[hh]].astype(F32) * beta[hh]
            kd_s[hh] = (kn[hh] * jnp.exp(gl_t - gc_t)).astype(BF16)
            qd_s[hh] = qn[hh] * egc
            for ch in range(wd):
                eg_s[slot, hh, ch * 8:(ch + 1) * 8, :] = jnp.broadcast_to(
                    jnp.exp(gl_t[ch * c:ch * c + 1, :]), (8, LANES))
        yield
        for _ in range(shift - 1):
            both = {hh: _dot(jnp.concatenate([ews[hh], pws[hh]], 0).astype(BF16), bdiag(pws[hh])) for hh in hs_}
            ews = {hh: ews[hh] + pws[hh] + both[hh][0:c] for hh in hs_}
            pws = {hh: both[hh][c:2 * c] for hh in hs_}
            yield
        rhss = {hh: rhs_s[hh] for hh in hs_}
        solb = {hh: (rhss[hh] + _dot(bdiag(ews[hh]), rhss[hh].astype(BF16))).astype(BF16) for hh in hs_}
        yield
        asol = {hh: _dot(bdiag(at_s[hh]), solb[hh]) for hh in hs_}
        for hh in hs_:
            op_s[slot, hh] = asol[hh][:, GDN_DK:]
            qp = (qd_s[hh] - asol[hh][:, :GDN_DK]).astype(BF16)
            for ch in range(wd):
                mq_s[slot, hh, ch, GDN_DK:GDN_DK + c, :] = qp[ch * c:(ch + 1) * c]
        yield
        for ch in range(wd):
            cr = slice(ch * c, (ch + 1) * c)
            mns = {hh: _dot_tn(kd_s[hh, cr, :], solb[hh][cr]) for hh in hs_}
            for hh in hs_:
                mq_s[slot, hh, ch, 0:GDN_DK, :] = mns[hh][:, :GDN_DK].astype(BF16)
                nn_s[slot, hh, ch] = mns[hh][:, GDN_DK:]
            if ch % 2 == 1:
                yield

    def scan_steps(i, slot):
        for ch in range(wd):
            rows = pl.ds(pl.multiple_of(i * blk + ch * c, c), c)
            for hh in hs_all:
                cols = slice(hh * GDN_DK, (hh + 1) * GDN_DK)
                state = st_s[hh]
                x = _dot(mq_s[slot, hh, ch], state.astype(BF16))
                st_s[hh] = state * eg_s[slot, hh, ch * 8:ch * 8 + 1, :] - x[:GDN_DK] + nn_s[slot, hh, ch]
                o = x[GDN_DK:] + op_s[slot, hh, ch * c:(ch + 1) * c, :]
                y = o * lax.rsqrt(jnp.mean(o * o, -1, keepdims=True) + EPS) * nw_ref[...]
                gt = gt_ref[0, rows, cols].astype(F32)
                o_ref[0, rows, cols] = (y * (gt * _sigmoid(gt))).astype(o_ref.dtype)
            yield

    def interleave(gens):
        gens = list(gens)
        while gens:
            for gen in list(gens):
                try:
                    next(gen)
                except StopIteration:
                    gens.remove(gen)

    def stage(prep_i, prep_slot, scan_i, scan_slot):
        gens = []
        if prep_i is not None:
            gens += [prep_steps(prep_i, prep_slot, hs_all[g0:g0 + GDN_LOCKSTEP])
                     for g0 in range(0, GDN_HG, GDN_LOCKSTEP)]
        if scan_i is not None:
            gens.append(scan_steps(scan_i, scan_slot))
        interleave(gens)

    assert n_blocks >= 2 and n_blocks % 2 == 0
    st_s[...] = jnp.zeros_like(st_s)
    stage(0, 0, None, None)

    def body(j, carry):
        stage(2 * j + 1, 1, 2 * j, 0)
        stage(2 * j + 2, 0, 2 * j + 1, 1)
        return carry

    lax.fori_loop(0, (n_blocks - 2) // 2, body, 0)
    stage(n_blocks - 1, 1, n_blocks - 2, 0)
    stage(None, None, n_blocks - 1, 1)


def _gdn(qkvg, small, a_log, dt_bias, norm_w):
    b, s, _ = qkvg.shape
    gw = GDN_HG * GDN_DK
    groups = GDN_HEADS // GDN_HG
    blk = lambda off: pl.BlockSpec((1, s, gw), lambda i, j: (i, 0, off + j), pipeline_mode=pl.Buffered(1))
    prm = jnp.zeros((8, LANES), F32)
    prm = prm.at[0, GDN_HEADS:2 * GDN_HEADS].set(a_log).at[1, GDN_HEADS:2 * GDN_HEADS].set(dt_bias)
    return pl.pallas_call(
        functools.partial(_gdn_kernel, seq=s),
        grid=(b, groups),
        in_specs=[pl.BlockSpec((8, LANES), lambda i, j: (0, 0)),
                  blk(0), blk(groups), blk(2 * groups), blk(3 * groups),
                  pl.BlockSpec((1, s, LANES), lambda i, j: (i, 0, 0)),
                  pl.BlockSpec((1, GDN_DV), lambda i, j: (0, 0))],
        out_specs=pl.BlockSpec((1, s, gw), lambda i, j: (i, 0, j)),
        out_shape=jax.ShapeDtypeStruct((b, s, GDN_W), BF16),
        scratch_shapes=[pltpu.VMEM((2, GDN_HG, GDN_WIDE, GDN_DK + GDN_CHUNK, GDN_DK), BF16),
                        pltpu.VMEM((2, GDN_HG, GDN_WIDE, GDN_DK, GDN_DV), F32),
                        pltpu.VMEM((2, GDN_HG, GDN_BLK, GDN_DV), F32),
                        pltpu.VMEM((2, GDN_HG, GDN_WIDE * 8, LANES), F32),
                        pltpu.VMEM((GDN_HG, GDN_DK, GDN_DV), F32),
                        pltpu.VMEM((GDN_HG, GDN_BLK, GDN_DK + GDN_DV), F32),
                        pltpu.VMEM((GDN_HG, GDN_BLK, GDN_DK), BF16),
                        pltpu.VMEM((GDN_HG, GDN_BLK, GDN_DK), F32),
                        pltpu.VMEM((GDN_HG, GDN_CHUNK, GDN_BLK), F32)],
        compiler_params=_cparams(2),
        name="gated_delta_rule",
    )(prm, qkvg, qkvg, qkvg, qkvg, small, norm_w.reshape(1, GDN_DV))


def kernel(x, positions, ev_w_in, ev_w_out, od_w_in, od_conv_w, od_a_log, od_dt_bias, od_norm_w, od_w_out,
           ffn_w_up, ffn_conv_w, ffn_conv_b, ffn_w_down, ln1_g, ln1_b, ln2_g, ln2_b):
    b, s, d = x.shape
    ret_f, dil_f = _rope_freqs()
    pos3 = positions.reshape(b, s, 1)
    tab_ret = _trig_tables(pos3, ret_f, (True, False))
    tab_dil = _trig_tables(pos3, dil_f, (True, False))
    ret_consts = _ret_consts()
    row = lambda a: a.reshape(1, d)

    for layer in range(DEPTH):
        j = layer // 2
        if layer % 2 == 0:
            hproj = _proj(x, ev_w_in[j].astype(BF16), ROW_TILE, 512)
            ya = _retention(hproj, tab_ret, ret_consts)
            yb = _dilated(hproj, tab_dil)
            w_out = ev_w_out[j].astype(BF16)
            x = _out_ln(x, [ya, yb], [w_out[:RET_V_W], w_out[RET_V_W:]], row(ln1_g[layer]), row(ln1_b[layer]),
                        ROW_TILE)
        else:
            w_in = od_w_in[j]
            w_main = w_in[:, :4 * GDN_W].astype(BF16)
            w_small = jnp.pad(w_in[:, 4 * GDN_W:], ((0, 0), (0, LANES - 2 * GDN_HEADS))).astype(BF16)
            cw = jnp.pad(od_conv_w[j], ((0, 8 - GDN_CONV), (0, 0)))
            qkvg, small = _odd_in(x, w_main, cw, w_small, ROW_TILE)
            yc = _gdn(qkvg, small, od_a_log[j], od_dt_bias[j], od_norm_w[j])
            x = _out_ln(x, [yc], [od_w_out[j].astype(BF16)], row(ln1_g[layer]), row(ln1_b[layer]), ROW_TILE)
        wup_r, cw_r, wdn_r = _ffn_weights(ffn_w_up[layer], ffn_conv_w[layer], ffn_conv_b[layer], ffn_w_down[layer])
        x = _ffn(x, wup_r, cw_r, wdn_r, row(ln2_g[layer]), row(ln2_b[layer]), ROW_TILE)
    return x
```

```python
import functools
import math

import jax
import jax.numpy as jnp
import numpy as np
from jax import lax
from jax.experimental import pallas as pl
from jax.experimental.pallas import tpu as pltpu

F32 = jnp.float32
BF16 = jnp.bfloat16

D_MODEL = 1024
DEPTH = 4
RET_HEADS, RET_DK, RET_DV, RET_CHUNK, RET_THETA = 4, 128, 256, 128, 10000.0
DIL_HEADS, DIL_HD, DIL_BLOCK = 8, 64, 128
DIL_DILATIONS = (1, 4, 16)
DIL_INTERLEAVE = 4
ROPE_THETA, ROPE_DIMS = 500000.0, DIL_HD // 4
GDN_HEADS, GDN_DK, GDN_DV, GDN_CHUNK, GDN_CONV = 8, 128, 128, 64, 4
D_FF, FFN_CONV = 2816, 3
DN_ALPHA = (2.0 * DEPTH) ** 0.25
EPS = 1e-5

RET_QK_W = RET_HEADS * RET_DK
RET_V_W = RET_HEADS * RET_DV
DIL_W = DIL_HEADS * DIL_HD
EV_IN = 2 * RET_QK_W + 2 * RET_V_W + 3 * DIL_W
GDN_W = GDN_HEADS * GDN_DK

LANES = 128
HALO = 16
ROW_TILE = 1024
VMEM_LIMIT = 56 * 1024 * 1024
HI = lax.Precision.HIGHEST


def _cparams(n_axes, vmem=VMEM_LIMIT):
    return pltpu.CompilerParams(dimension_semantics=("arbitrary",) * n_axes, vmem_limit_bytes=vmem)


def _dot(a, b):
    return jnp.dot(a, b, preferred_element_type=F32)


def _dot_nt(a, b):
    return lax.dot_general(a, b, (((1,), (1,)), ((), ())), preferred_element_type=F32)


def _dot_tn(a, b):
    return lax.dot_general(a, b, (((0,), (0,)), ((), ())), preferred_element_type=F32)


def _sigmoid(x):
    return 1.0 / (1.0 + jnp.exp(-x))


def _silu(x):
    return x * _sigmoid(x)


def _layernorm_rows(z, g, b):
    mu = jnp.mean(z, -1, keepdims=True)
    zc = z - mu
    var = jnp.mean(zc * zc, -1, keepdims=True)
    return zc * lax.rsqrt(var + EPS) * g + b


def _proj_kernel(x_ref, w_ref, o_ref, *, tn):
    xb = x_ref[0].astype(BF16)
    for j in range(w_ref.shape[1] // tn):
        cols = slice(j * tn, (j + 1) * tn)
        o_ref[0, :, cols] = _dot(xb, w_ref[:, cols]).astype(o_ref.dtype)


def _proj(x3, w_bf16, tm, tn):
    bsz, s, k = x3.shape
    n = w_bf16.shape[1]
    return pl.pallas_call(
        functools.partial(_proj_kernel, tn=tn),
        grid=(bsz, s // tm),
        in_specs=[pl.BlockSpec((1, tm, k), lambda bi, i: (bi, i, 0)),
                  pl.BlockSpec((k, n), lambda bi, i: (0, 0), pipeline_mode=pl.Buffered(1))],
        out_specs=pl.BlockSpec((1, tm, n), lambda bi, i: (bi, i, 0)),
        out_shape=jax.ShapeDtypeStruct((bsz, s, n), BF16),
        compiler_params=_cparams(2),
        name="ev_in_proj",
    )(x3, w_bf16)


def _trig_kernel(pos_ref, f_ref, o_ref, *, n_tab, cos_mask):
    rc = 256

    def body(i, carry):
        rows = pl.ds(pl.multiple_of(i * rc, rc), rc)
        pos = pos_ref[0, rows, :].astype(F32)
        for t in range(n_tab):
            ang = pos * f_ref[t:t + 1, :]
            o_ref[0, t, rows, :] = jnp.cos(ang) if cos_mask[t] else jnp.sin(ang)
        return carry

    lax.fori_loop(0, pos_ref.shape[1] // rc, body, 0)


def _trig_tables(pos3, freqs, cos_mask):
    b, s, _ = pos3.shape
    n_tab = len(cos_mask)
    fpad = jnp.zeros((8, LANES), F32).at[:n_tab].set(freqs)
    return pl.pallas_call(
        functools.partial(_trig_kernel, n_tab=n_tab, cos_mask=cos_mask),
        grid=(b,),
        in_specs=[pl.BlockSpec((1, s, 1), lambda i: (i, 0, 0)),
                  pl.BlockSpec((8, LANES), lambda i: (0, 0))],
        out_specs=pl.BlockSpec((1, n_tab, s, LANES), lambda i: (i, 0, 0, 0)),
        out_shape=jax.ShapeDtypeStruct((b, n_tab, s, LANES), F32),
        compiler_params=_cparams(1),
        name="rope_tables",
    )(pos3, fpad)


def _rope_freqs():
    half = RET_DK // 2
    inv = jnp.power(RET_THETA, -jnp.arange(half, dtype=F32) * 2.0 / RET_DK)
    ret = jnp.stack([jnp.concatenate([inv, inv]), jnp.concatenate([-inv, inv])])
    hh = ROPE_DIMS // 2
    invd = jnp.power(ROPE_THETA, -jnp.arange(hh, dtype=F32) * 2.0 / ROPE_DIMS)
    z = jnp.zeros((DIL_HD - ROPE_DIMS,), F32)
    zh = jnp.zeros((hh,), F32)
    f_cos = jnp.concatenate([invd, invd, z])
    f_sin = jnp.concatenate([-invd, invd, z])
    dil = jnp.stack([jnp.tile(f_cos, 2), jnp.tile(f_sin, 2)])
    return ret, dil


def _ret_kernel(gch_ref, q_ref, k_ref, v_ref, g_ref, tab_ref, dm_ref, zt_ref, xi_ref, o_ref, r_ref, *, n_chunks):
    r_ref[...] = jnp.zeros_like(r_ref)
    c = RET_CHUNK
    heads = range(RET_HEADS)

    def body(ci, carry):
        rows = pl.ds(pl.multiple_of(ci * c, c), c)
        cosr = tab_ref[0, 0, rows, :]
        sinr = tab_ref[0, 1, rows, :]
        qk = lambda h: slice(h * RET_DK, (h + 1) * RET_DK)
        vv = lambda h: slice(h * RET_DV, (h + 1) * RET_DV)
        qs = [q_ref[0, rows, qk(h)].astype(F32) for h in heads]
        ks = [k_ref[0, rows, qk(h)].astype(F32) for h in heads]
        qr = [q * cosr + pltpu.roll(q, RET_DK // 2, 1) * sinr for q in qs]
        kr = [(k * cosr + pltpu.roll(k, RET_DK // 2, 1) * sinr) * (RET_DK ** -0.5) for k in ks]
        vs = [v_ref[0, rows, vv(h)] for h in heads]
        r_prev = [r_ref[h] for h in heads]
        scores = [_dot_nt(qr[h].astype(BF16), kr[h].astype(BF16)) * dm_ref[h] for h in heads]
        inter = [_dot((qr[h] * xi_ref[h]).astype(BF16), r_prev[h].astype(BF16)) for h in heads]
        kv = [_dot_tn((kr[h] * zt_ref[h]).astype(BF16), vs[h]) for h in heads]
        os_ = [_dot(scores[h].astype(BF16), vs[h]) + inter[h] for h in heads]
        for h in heads:
            r_ref[h] = r_prev[h] * gch_ref[h] + kv[h]
            o = os_[h]
            y = o * lax.rsqrt(jnp.mean(o * o, -1, keepdims=True) + EPS)
            g = g_ref[0, rows, vv(h)].astype(F32)
            o_ref[0, rows, vv(h)] = (y * _silu(g)).astype(o_ref.dtype)
        return carry

    lax.fori_loop(0, n_chunks, body, 0, unroll=2)


def _ret_consts():
    hh = np.arange(RET_HEADS, dtype=np.float64)
    lg = np.log1p(-np.power(2.0, -5.0 - hh))
    idx = np.arange(RET_CHUNK, dtype=np.float64)
    rel = idx[:, None] - idx[None, :]
    dmat = np.where(rel >= 0, np.exp(np.maximum(rel, 0.0) * lg[:, None, None]), 0.0)
    zeta = np.exp((RET_CHUNK - 1 - idx) * lg[:, None])
    xi = np.exp((idx + 1) * lg[:, None])
    gch = np.exp(RET_CHUNK * lg)
    bc = lambda a: np.broadcast_to(a[:, :, None], (RET_HEADS, RET_CHUNK, RET_DK))
    return (jnp.asarray(gch, F32), jnp.asarray(dmat, F32), jnp.asarray(bc(zeta), F32), jnp.asarray(bc(xi), F32))


def _retention(hproj, tab_ret, consts):
    b, s, _ = hproj.shape
    gch, dmat, zeta, xi = consts
    v_off = 2 * RET_QK_W // RET_V_W
    all_heads = pl.BlockSpec((RET_HEADS, RET_CHUNK, RET_DK), lambda i: (0, 0, 0))
    return pl.pallas_call(
        functools.partial(_ret_kernel, n_chunks=s // RET_CHUNK),
        grid=(b,),
        in_specs=[pl.BlockSpec(memory_space=pltpu.SMEM),
                  pl.BlockSpec((1, s, RET_QK_W), lambda i: (i, 0, 0)),
                  pl.BlockSpec((1, s, RET_QK_W), lambda i: (i, 0, 1)),
                  pl.BlockSpec((1, s, RET_V_W), lambda i: (i, 0, v_off)),
                  pl.BlockSpec((1, s, RET_V_W), lambda i: (i, 0, v_off + 1)),
                  pl.BlockSpec((1, 2, s, LANES), lambda i: (i, 0, 0, 0)),
                  all_heads, all_heads, all_heads],
        out_specs=pl.BlockSpec((1, s, RET_V_W), lambda i: (i, 0, 0)),
        out_shape=jax.ShapeDtypeStruct((b, s, RET_V_W), BF16),
        scratch_shapes=[pltpu.VMEM((RET_HEADS, RET_DK, RET_DV), F32)],
        compiler_params=_cparams(1),
        name="retention",
    )(gch, hproj, hproj, hproj, hproj, tab_ret, dmat, zeta, xi)


def _dil_kernel(q_ref, k_ref, v_ref, tab_ref, o_ref, qs, ks, vs, acc, ms, ls, *, seq):
    rc = 256
    lane = lax.broadcasted_iota(jnp.int32, (DIL_BLOCK, LANES), 1)
    first_head = lane < DIL_HD

    def rot_body(i, carry):
        rows = pl.ds(pl.multiple_of(i * rc, rc), rc)
        cosd = tab_ref[0, 0, rows, :]
        sind = tab_ref[0, 1, rows, :]
        hh = ROPE_DIMS // 2
        low_half = (lax.broadcasted_iota(jnp.int32, (rc, LANES), 1) & (DIL_HD - 1)) < hh

        def rot(x):
            return x * cosd + jnp.where(low_half, pltpu.roll(x, LANES - hh, 1), pltpu.roll(x, hh, 1)) * sind

        qs[rows, :] = rot(q_ref[0, rows, :].astype(F32)) * (DIL_HD ** -0.5)
        ks[rows, :] = rot(k_ref[0, rows, :].astype(F32))
        vs[rows, :] = v_ref[0, rows, :].astype(F32)
        return carry

    lax.fori_loop(0, seq // rc, rot_body, 0)

    def blocks(g, d, kb_rows, specs):
        def sl(start, n):
            return pl.ds(start, n) if d == 1 else pl.ds(start, n, stride=d)

        qsl = [sl(q0, DIL_BLOCK) for q0, _, _ in specs]
        ksl = [sl(k0, kb_rows) for _, k0, _ in specs]
        qbs = [qs[s_, :] for s_ in qsl]
        kbs = [ks[s_, :].astype(BF16) for s_ in ksl]
        vbs = [vs[s_, :].astype(BF16) for s_ in ksl]
        nb_ = range(len(specs))
        q2 = [jnp.concatenate([jnp.where(first_head, qbs[b_], 0.0), jnp.where(first_head, 0.0, qbs[b_])], 0)
              .astype(BF16) for b_ in nb_]
        sc = [_dot_nt(q2[b_], kbs[b_]) + jnp.concatenate([specs[b_][2]] * 2, 0) for b_ in nb_]
        mx = [jnp.max(s_, -1, keepdims=True) for s_ in sc]
        pr = [jnp.exp(s_ - m_) for s_, m_ in zip(sc, mx)]
        ones_v = jnp.ones((kb_rows, LANES), BF16)
        pv = [_dot(pr[b_].astype(BF16), jnp.concatenate([vbs[b_], ones_v], 1)) for b_ in nb_]
        h0, h1 = slice(0, DIL_BLOCK), slice(DIL_BLOCK, 2 * DIL_BLOCK)
        for b_ in nb_:
            acc[g, qsl[b_], :] = jnp.where(first_head, pv[b_][h0, :LANES], pv[b_][h1, :LANES])
            ms[g, qsl[b_], :] = jnp.where(first_head, mx[b_][h0], mx[b_][h1])
            ls[g, qsl[b_], :] = jnp.where(first_head, pv[b_][h0, LANES:], pv[b_][h1, LANES:])

    def window_bias(kb_rows, off):
        dist = (lax.broadcasted_iota(jnp.int32, (DIL_BLOCK, kb_rows), 0)
                - lax.broadcasted_iota(jnp.int32, (DIL_BLOCK, kb_rows), 1)) + off
        return jnp.where((dist >= 0) & (dist <= DIL_BLOCK), 0.0, -1e30).astype(F32)

    bias_self = window_bias(DIL_BLOCK, 0)
    bias_first = window_bias(2 * DIL_BLOCK, 0)
    bias_band = window_bias(2 * DIL_BLOCK, DIL_BLOCK)

    nbi = DIL_INTERLEAVE
    for g, d in enumerate(DIL_DILATIONS):
        nb = seq // d // DIL_BLOCK
        span = DIL_BLOCK * d
        if nb == 1:
            for i in range(d // nbi):
                blocks(g, d, DIL_BLOCK, [(i * nbi + u, i * nbi + u, bias_self) for u in range(nbi)])
        elif d >= nbi:
            per_n = d // nbi
            for i0 in range(per_n):
                blocks(g, d, 2 * DIL_BLOCK, [(i0 * nbi + u, i0 * nbi + u, bias_first) for u in range(nbi)])

            for i in range(per_n, nb * per_n):
                n = i // per_n
                r0 = (i % per_n) * nbi
                blocks(g, d, 2 * DIL_BLOCK,
                       [(r0 + u + n * span, r0 + u + (n - 1) * span, bias_band) for u in range(nbi)])
        else:
            assert d == 1
            blocks(g, d, 2 * DIL_BLOCK, [(0, 0, bias_first)] + [(u * span, (u - 1) * span, bias_band)
                                                                  for u in range(1, nbi)])

            for i in range(1, nb // nbi):
                blocks(g, d, 2 * DIL_BLOCK,
                       [((i * nbi + u) * span, (i * nbi + u - 1) * span, bias_band) for u in range(nbi)])

    def comb_body(i, carry):
        rows = pl.ds(pl.multiple_of(i * rc, rc), rc)
        m0, m1, m2 = ms[0, rows, :], ms[1, rows, :], ms[2, rows, :]
        mmax = jnp.maximum(jnp.maximum(m0, m1), m2)
        e0, e1, e2 = jnp.exp(m0 - mmax), jnp.exp(m1 - mmax), jnp.exp(m2 - mmax)
        num = e0 * acc[0, rows, :] + e1 * acc[1, rows, :] + e2 * acc[2, rows, :]
        den = e0 * ls[0, rows, :] + e1 * ls[1, rows, :] + e2 * ls[2, rows, :]
        o_ref[0, rows, :] = (num / den).astype(o_ref.dtype)
        return carry

    lax.fori_loop(0, seq // rc, comb_body, 0)


def _dilated(hproj, tab_dil):
    b, s, _ = hproj.shape
    base = (2 * RET_QK_W + 2 * RET_V_W) // LANES
    pairs = DIL_W // LANES
    blk = lambda off: pl.BlockSpec((1, s, LANES), lambda i, j: (i, 0, base + off + j))
    n_g = len(DIL_DILATIONS)
    return pl.pallas_call(
        functools.partial(_dil_kernel, seq=s),
        grid=(b, pairs),
        in_specs=[blk(0), blk(pairs), blk(2 * pairs),
                  pl.BlockSpec((1, 2, s, LANES), lambda i, j: (i, 0, 0, 0))],
        out_specs=pl.BlockSpec((1, s, LANES), lambda i, j: (i, 0, j)),
        out_shape=jax.ShapeDtypeStruct((b, s, DIL_W), BF16),
        scratch_shapes=[pltpu.VMEM((s, LANES), F32)] * 3 + [pltpu.VMEM((n_g, s, LANES), F32)] * 3,
        compiler_params=_cparams(2),
        name="dilated_attention",
    )(hproj, hproj, hproj, tab_dil)


def _out_ln_kernel(*refs, n_in):
    x_ref = refs[0]
    ys = refs[1:1 + n_in]
    ws = refs[1 + n_in:1 + 2 * n_in]
    g_ref, b_ref, o_ref = refs[1 + 2 * n_in:]
    tm = x_ref.shape[1]
    halves = [slice(0, tm // 2), slice(tm // 2, tm)]
    zs = []
    for rows in halves:
        z = DN_ALPHA * x_ref[0, rows, :]
        for y_ref, w_ref in zip(ys, ws):
            z = z + _dot(y_ref[0, rows, :], w_ref[...])
        zs.append(z)
    for rows, z in zip(halves, zs):
        o_ref[0, rows, :] = _layernorm_rows(z, g_ref[...], b_ref[...])


def _out_ln(x3, ys, ws, g, b, tm):
    bsz, s, d = x3.shape
    n_in = len(ys)
    row = lambda w: pl.BlockSpec((1, tm, w), lambda bi, i: (bi, i, 0))
    full = lambda a: pl.BlockSpec(a.shape, lambda bi, i: (0, 0))
    return pl.pallas_call(
        functools.partial(_out_ln_kernel, n_in=n_in),
        grid=(bsz, s // tm),
        in_specs=[row(d)] + [row(y.shape[2]) for y in ys] + [full(w) for w in ws] + [full(g), full(b)],
        out_specs=row(d),
        out_shape=jax.ShapeDtypeStruct((bsz, s, d), F32),
        compiler_params=_cparams(2),
        name="out_proj_ln",
    )(x3, *ys, *ws, g, b)


def _conv_taps(u_ref, row0, rows, cols, taps):
    k_w = len(taps)
    ub = u_ref[row0 - 8:row0 + rows, cols]
    out = ub[8:] * taps[k_w - 1]
    for back in range(1, k_w):
        out = out + pltpu.roll(ub, back, 0)[8:] * taps[k_w - 1 - back]
    return out


def _fill_xb(xb, x_ref, xh_ref, i):
    xb[HALO:, :] = x_ref[0].astype(BF16)
    xb[0:HALO, :] = jnp.where(i > 0, xh_ref[0], 0.0).astype(BF16)


FFN_PIECE = 256


def _ffn_kernel(x_ref, xh_ref, wup_ref, cw_ref, wdn_ref, g_ref, b_ref, o_ref, xb, us, hs, *, tm, rb):
    i = pl.program_id(1)
    _fill_xb(xb, x_ref, xh_ref, i)
    pc = FFN_PIECE
    n_pieces = D_FF // pc

    def up(p):
        x = xb[...]
        us[p % 2, :, 0:pc] = _dot(x, wup_ref[:, pc * p:pc * (p + 1)])
        us[p % 2, :, pc:2 * pc] = _dot(x, wup_ref[:, D_FF + pc * p:D_FF + pc * (p + 1)])

    def gate_piece(p):
        u = us.at[p % 2]
        for r in range(tm // rb):
            row0 = r * rb + HALO
            for h in range(pc // LANES):
                cg = slice(h * LANES, (h + 1) * LANES)
                cv = slice(pc + h * LANES, pc + (h + 1) * LANES)
                wg = slice(pc * p + h * LANES, pc * p + (h + 1) * LANES)
                wv = slice(D_FF + pc * p + h * LANES, D_FF + pc * p + (h + 1) * LANES)
                tg = [cw_ref[j:j + 1, wg] for j in range(FFN_CONV)]
                tv = [cw_ref[j:j + 1, wv] for j in range(FFN_CONV)]
                gate = _conv_taps(u, row0, rb, cg, tg) + cw_ref[FFN_CONV:FFN_CONV + 1, wg]
                val = _conv_taps(u, row0, rb, cv, tv) + cw_ref[FFN_CONV:FFN_CONV + 1, wv]
                hs[r * rb:(r + 1) * rb, wg] = (_silu(gate) * val).astype(BF16)

    up(0)
    for p in range(1, n_pieces):
        up(p)
        gate_piece(p - 1)
    gate_piece(n_pieces - 1)
    ffn = _dot(hs[...], wdn_ref[...])
    o_ref[0] = _layernorm_rows(DN_ALPHA * x_ref[0] + ffn, g_ref[...], b_ref[...])


def _ffn(x3, wup_r, cw_r, wdn_r, g, b, tm):
    bsz, s, d = x3.shape
    hb = tm // HALO
    const = lambda a: pl.BlockSpec(a.shape, lambda bi, i: (0, 0), pipeline_mode=pl.Buffered(1))
    return pl.pallas_call(
        functools.partial(_ffn_kernel, tm=tm, rb=128),
        grid=(bsz, s // tm),
        in_specs=[pl.BlockSpec((1, tm, d), lambda bi, i: (bi, i, 0)),
                  pl.BlockSpec((1, HALO, d), lambda bi, i: (bi, jnp.maximum(i * hb - 1, 0), 0)),
                  const(wup_r), const(cw_r), const(wdn_r), const(g), const(b)],
        out_specs=pl.BlockSpec((1, tm, d), lambda bi, i: (bi, i, 0)),
        out_shape=jax.ShapeDtypeStruct((bsz, s, d), F32),
        scratch_shapes=[pltpu.VMEM((HALO + tm, d), BF16),
                        pltpu.VMEM((2, HALO + tm, 2 * FFN_PIECE), F32),
                        pltpu.VMEM((tm, D_FF), BF16)],
        compiler_params=_cparams(2),
        name="conv_ffn_ln",
    )(x3, x3, wup_r, cw_r, wdn_r, g, b)


def _ffn_weights(w_up, conv_w, conv_b, w_down):
    cw = jnp.concatenate([conv_w, conv_b[None, :], jnp.zeros((8 - FFN_CONV - 1, 2 * D_FF), F32)], 0)
    return w_up.astype(BF16), cw, w_down.astype(BF16)


ODD_PIECE = 512


def _odd_in_kernel(x_ref, xh_ref, w_ref, cw_ref, ws_ref, o_ref, os_ref, xb, us, *, tm, conv_cols, rb):
    i = pl.program_id(1)
    _fill_xb(xb, x_ref, xh_ref, i)
    pc = ODD_PIECE
    n_pieces = w_ref.shape[1] // pc

    def up(p):
        us[p % 2] = _dot(xb[...], w_ref[:, pc * p:pc * (p + 1)])

    def finish(p):
        u = us.at[p % 2]
        if pc * p >= conv_cols:
            o_ref[0, :, pc * p:pc * (p + 1)] = u[HALO:, :].astype(o_ref.dtype)
            return
        for r in range(tm // rb):
            for h in range(pc // LANES):
                cs = slice(h * LANES, (h + 1) * LANES)
                ws = slice(pc * p + h * LANES, pc * p + (h + 1) * LANES)
                taps = [cw_ref[j:j + 1, ws] for j in range(GDN_CONV)]
                y = _conv_taps(u, r * rb + HALO, rb, cs, taps)
                y = _silu(y)
                if pc * p + h * LANES < 2 * GDN_W:
                    y = y * lax.rsqrt(jnp.sum(y * y, -1, keepdims=True) + 1e-6)
                    if pc * p + h * LANES < GDN_W:
                        y = y * (GDN_DK ** -0.5)
                o_ref[0, r * rb:(r + 1) * rb, ws] = y.astype(o_ref.dtype)

    up(0)
    os_ref[0] = _dot(xb[HALO:, :], ws_ref[...])
    for p in range(1, n_pieces):
        up(p)
        finish(p - 1)
    finish(n_pieces - 1)


def _odd_in(x3, w_main, cw, w_small, tm):
    bsz, s, d = x3.shape
    n = w_main.shape[1]
    hb = tm // HALO
    const = lambda a: pl.BlockSpec(a.shape, lambda bi, i: (0, 0), pipeline_mode=pl.Buffered(1))
    return pl.pallas_call(
        functools.partial(_odd_in_kernel, tm=tm, conv_cols=cw.shape[1], rb=128),
        grid=(bsz, s // tm),
        in_specs=[pl.BlockSpec((1, tm, d), lambda bi, i: (bi, i, 0)),
                  pl.BlockSpec((1, HALO, d), lambda bi, i: (bi, jnp.maximum(i * hb - 1, 0), 0)),
                  const(w_main), const(cw), const(w_small)],
        out_specs=[pl.BlockSpec((1, tm, n), lambda bi, i: (bi, i, 0)),
                   pl.BlockSpec((1, tm, LANES), lambda bi, i: (bi, i, 0))],
        out_shape=[jax.ShapeDtypeStruct((bsz, s, n), BF16),
                   jax.ShapeDtypeStruct((bsz, s, LANES), F32)],
        scratch_shapes=[pltpu.VMEM((HALO + tm, d), BF16),
                        pltpu.VMEM((2, HALO + tm, ODD_PIECE), F32)],
        compiler_params=_cparams(2),
        name="od_in_proj_conv",
    )(x3, x3, w_main, cw, w_small)


GDN_WIDE = 4
GDN_BLK = GDN_CHUNK * GDN_WIDE
GDN_HG = 8
GDN_LOCKSTEP = 2


def _split_bf16(a):
    hi = a.astype(BF16)
    return hi, (a - hi.astype(F32)).astype(BF16)


def _gdn_kernel(prm_ref, q_ref, k_ref, v_ref, gt_ref, sm_ref, nw_ref, o_ref,
                mq_s, nn_s, op_s, eg_s, st_s, rhs_s, kd_s, qd_s, at_s, *, seq):
    hg = pl.program_id(1)
    c, wd, blk = GDN_CHUNK, GDN_WIDE, GDN_BLK
    ww = c * wd
    n_blocks = seq // blk
    n_chunks = seq // c
    shift = int(math.log2(c))

    r_w = lax.broadcasted_iota(jnp.int32, (c, ww), 0)
    l_w = lax.broadcasted_iota(jnp.int32, (c, ww), 1)
    j_w = l_w & (c - 1)
    chunk_of_lane = l_w >> shift
    tri_w = r_w >= j_w
    strict_w = r_w > j_w
    upper_w = jnp.where(r_w <= j_w, 1.0, 0.0).astype(F32)
    bmask = (lax.broadcasted_iota(jnp.int32, (ww, ww), 0) >> shift) == (lax.broadcasted_iota(jnp.int32, (ww, ww), 1) >> shift)
    ltri = jnp.where(lax.broadcasted_iota(jnp.int32, (c, c), 0) >= lax.broadcasted_iota(jnp.int32, (c, c), 1),
                     1.0, 0.0).astype(BF16)
    ones_cc = jnp.ones((c, c), BF16)
    lane_blk = lax.broadcasted_iota(jnp.int32, (blk, LANES), 1)
    first_half = lax.broadcasted_iota(jnp.int32, (c, LANES), 1) < c
    a_neg = -jnp.exp(prm_ref[0:1, :])
    dtb = prm_ref[1:2, :]

    def bdiag(xw):
        return jnp.where(bmask, jnp.concatenate([xw] * wd, 0), 0.0).astype(BF16)

    hs_all = list(range(GDN_HG))

    def widen(col):
        out = jnp.broadcast_to(col[0:c], (c, ww))
        for ch in range(1, wd):
            out = jnp.where(chunk_of_lane == ch, jnp.broadcast_to(col[ch * c:(ch + 1) * c], (c, ww)), out)
        return out

    def prep_steps(i, slot, hs_):
        rows = pl.ds(pl.multiple_of(i * blk, blk), blk)
        sm = sm_ref[0, rows, :]
        beta_all = _sigmoid(sm)
        sp_in = sm + dtb
        g_all = a_neg * (jnp.maximum(sp_in, 0.0) + jnp.log1p(jnp.exp(-jnp.abs(sp_in))))
        cols = {hh: slice(hh * GDN_DK, (hh + 1) * GDN_DK) for hh in hs_}
        head = {hh: hg * GDN_HG + hh for hh in hs_}
        qn = {hh: q_ref[0, rows, cols[hh]].astype(F32) for hh in hs_}
        kn = {hh: k_ref[0, rows, cols[hh]].astype(F32) for hh in hs_}
        beta = {hh: jnp.sum(jnp.where(lane_blk == head[hh], beta_all, 0.0), -1, keepdims=True) for hh in hs_}
        g = {hh: jnp.sum(jnp.where(lane_blk == GDN_HEADS + head[hh], g_all, 0.0), -1, keepdims=True) for hh in hs_}
        kb = {hh: kn[hh] * beta[hh] for hh in hs_}
        g_w = {hh: widen(g[hh]) for hh in hs_}
        g_sp = {hh: _split_bf16(g_w[hh]) for hh in hs_}
        gu_sp = {hh: _split_bf16(g_w[hh] * upper_w) for hh in hs_}
        gc_col = {hh: _dot(ltri, g_sp[hh][0]) + _dot(ltri, g_sp[hh][1]) for hh in hs_}
        gc_row = {hh: _dot(ones_cc, gu_sp[hh][0]) + _dot(ones_cc, gu_sp[hh][1]) for hh in hs_}
        yield
        grams = {hh: [_dot_nt(jnp.concatenate([kb[hh][2 * p * c:(2 * p + 2) * c], qn[hh][2 * p * c:(2 * p + 2) * c]], 0)
                              .astype(BF16), kn[hh][2 * p * c:(2 * p + 2) * c].astype(BF16))
                      for p in range(wd // 2)] for hh in hs_}
        yield
        pws = {}
        for hh in hs_:
            decay = jnp.where(tri_w, jnp.exp(jnp.where(tri_w, gc_col[hh] - gc_row[hh], 0.0)), 0.0)
            a_w = jnp.concatenate([jnp.where(first_half, gm[0:c], gm[c:2 * c]) for gm in grams[hh]], 1)
            qk_w = jnp.concatenate([jnp.where(first_half, gm[2 * c:3 * c], gm[3 * c:4 * c]) for gm in grams[hh]], 1)
            pws[hh] = -jnp.where(strict_w, a_w * decay, 0.0)
            at_s[hh] = jnp.where(tri_w, qk_w * decay, 0.0)
        ews = dict(pws)
        pws = {hh: _dot(pws[hh].astype(BF16), bdiag(pws[hh])) for hh in hs_}
        for hh in hs_:
            gc_parts, gl_parts = [], []
            for p in range(wd // 2):
                x = gc_col[hh][:, 2 * p * c:(2 * p + 2) * c]
                xr = pltpu.roll(x, c, 1)
                for part in (jnp.where(first_half, x, xr), jnp.where(first_half, xr, x)):
                    gc_parts.append(part)
                    gl_parts.append(jnp.broadcast_to(part[c - 1:c, :], (c, LANES)))
            gc_t = jnp.concatenate(gc_parts, 0)
            gl_t = jnp.concatenate(gl_parts, 0)
            egc = jnp.exp(gc_t)
            rhs_s[hh, :, 0:GDN_DK] = kb[hh] * egc
            rhs_s[hh, :, GDN_DK:] = v_ref[0, rows, cols[hh]].astype(F32) * beta[hh]
            kd_s[hh] = (kn[hh] * jnp.exp(gl_t - gc_t)).astype(BF16)
            qd_s[hh] = qn[hh] * egc
            for ch in range(wd):
                eg_s[slot, hh, ch * 8:(ch + 1) * 8, :] = jnp.broadcast_to(
                    jnp.exp(gl_t[ch * c:ch * c + 1, :]), (8, LANES))
        yield
        for _ in range(shift - 1):
            both = {hh: _dot(jnp.concatenate([ews[hh], pws[hh]], 0).astype(BF16), bdiag(pws[hh])) for hh in hs_}
            ews = {hh: ews[hh] + pws[hh] + both[hh][0:c] for hh in hs_}
            pws = {hh: both[hh][c:2 * c] for hh in hs_}
            yield
        rhss = {hh: rhs_s[hh] for hh in hs_}
        solb = {hh: (rhss[hh] + _dot(bdiag(ews[hh]), rhss[hh].astype(BF16))).astype(BF16) for hh in hs_}
        yield
        asol = {hh: _dot(bdiag(at_s[hh]), solb[hh]) for hh in hs_}
        for hh in hs_:
            op_s[slot, hh] = asol[hh][:, GDN_DK:]
            qp = (qd_s[hh] - asol[hh][:, :GDN_DK]).astype(BF16)
            for ch in range(wd):
                mq_s[slot, hh, ch, GDN_DK:GDN_DK + c, :] = qp[ch * c:(ch + 1) * c]
        yield
        for ch in range(wd):
            cr = slice(ch * c, (ch + 1) * c)
            mns = {hh: _dot_tn(kd_s[hh, cr, :], solb[hh][cr]) for hh in hs_}
            for hh in hs_:
                mq_s[slot, hh, ch, 0:GDN_DK, :] = mns[hh][:, :GDN_DK].astype(BF16)
                nn_s[slot, hh, ch] = mns[hh][:, GDN_DK:]
            if ch % 2 == 1:
                yield

    def scan_steps(i, slot):
        for ch in range(wd):
            rows = pl.ds(pl.multiple_of(i * blk + ch * c, c), c)
            for hh in hs_all:
                cols = slice(hh * GDN_DK, (hh + 1) * GDN_DK)
                state = st_s[hh]
                x = _dot(mq_s[slot, hh, ch], state.astype(BF16))
                st_s[hh] = state * eg_s[slot, hh, ch * 8:ch * 8 + 1, :] - x[:GDN_DK] + nn_s[slot, hh, ch]
                o = x[GDN_DK:] + op_s[slot, hh, ch * c:(ch + 1) * c, :]
                y = o * lax.rsqrt(jnp.mean(o * o, -1, keepdims=True) + EPS) * nw_ref[...]
                gt = gt_ref[0, rows, cols].astype(F32)
                o_ref[0, rows, cols] = (y * _silu(gt)).astype(o_ref.dtype)
            yield

    def interleave(gens):
        gens = list(gens)
        while gens:
            for gen in list(gens):
                try:
                    next(gen)
                except StopIteration:
                    gens.remove(gen)

    def stage(prep_i, prep_slot, scan_i, scan_slot):
        gens = []
        if prep_i is not None:
            gens += [prep_steps(prep_i, prep_slot, hs_all[g0:g0 + GDN_LOCKSTEP])
                     for g0 in range(0, GDN_HG, GDN_LOCKSTEP)]
        if scan_i is not None:
            gens.append(scan_steps(scan_i, scan_slot))
        interleave(gens)

    assert n_blocks >= 2 and n_blocks % 2 == 0
    st_s[...] = jnp.zeros_like(st_s)
    stage(0, 0, None, None)

    def body(j, carry):
        stage(2 * j + 1, 1, 2 * j, 0)
        stage(2 * j + 2, 0, 2 * j + 1, 1)
        return carry

    lax.fori_loop(0, (n_blocks - 2) // 2, body, 0)
    stage(n_blocks - 1, 1, n_blocks - 2, 0)
    stage(None, None, n_blocks - 1, 1)


def _gdn(qkvg, small, a_log, dt_bias, norm_w):
    b, s, _ = qkvg.shape
    gw = GDN_HG * GDN_DK
    groups = GDN_HEADS // GDN_HG
    blk = lambda off: pl.BlockSpec((1, s, gw), lambda i, j: (i, 0, off + j), pipeline_mode=pl.Buffered(1))
    prm = jnp.zeros((8, LANES), F32)
    prm = prm.at[0, GDN_HEADS:2 * GDN_HEADS].set(a_log).at[1, GDN_HEADS:2 * GDN_HEADS].set(dt_bias)
    return pl.pallas_call(
        functools.partial(_gdn_kernel, seq=s),
        grid=(b, groups),
        in_specs=[pl.BlockSpec((8, LANES), lambda i, j: (0, 0)),
                  blk(0), blk(groups), blk(2 * groups), blk(3 * groups),
                  pl.BlockSpec((1, s, LANES), lambda i, j: (i, 0, 0)),
                  pl.BlockSpec((1, GDN_DV), lambda i, j: (0, 0))],
        out_specs=pl.BlockSpec((1, s, gw), lambda i, j: (i, 0, j)),
        out_shape=jax.ShapeDtypeStruct((b, s, GDN_W), BF16),
        scratch_shapes=[pltpu.VMEM((2, GDN_HG, GDN_WIDE, GDN_DK + GDN_CHUNK, GDN_DK), BF16),
                        pltpu.VMEM((2, GDN_HG, GDN_WIDE, GDN_DK, GDN_DV), F32),
                        pltpu.VMEM((2, GDN_HG, GDN_BLK, GDN_DV), F32),
                        pltpu.VMEM((2, GDN_HG, GDN_WIDE * 8, LANES), F32),
                        pltpu.VMEM((GDN_HG, GDN_DK, GDN_DV), F32),
                        pltpu.VMEM((GDN_HG, GDN_BLK, GDN_DK + GDN_DV), F32),
                        pltpu.VMEM((GDN_HG, GDN_BLK, GDN_DK), BF16),
                        pltpu.VMEM((GDN_HG, GDN_BLK, GDN_DK), F32),
                        pltpu.VMEM((GDN_HG, GDN_CHUNK, GDN_BLK), F32)],
        compiler_params=_cparams(2),
        name="gated_delta_rule",
    )(prm, qkvg, qkvg, qkvg, qkvg, small, norm_w.reshape(1, GDN_DV))


def kernel(x, positions, ev_w_in, ev_w_out, od_w_in, od_conv_w, od_a_log, od_dt_bias, od_norm_w, od_w_out,
           ffn_w_up, ffn_conv_w, ffn_conv_b, ffn_w_down, ln1_g, ln1_b, ln2_g, ln2_b):
    b, s, d = x.shape
    ret_f, dil_f = _rope_freqs()
    pos3 = positions.reshape(b, s, 1)
    tab_ret = _trig_tables(pos3, ret_f, (True, False))
    tab_dil = _trig_tables(pos3, dil_f, (True, False))
    ret_consts = _ret_consts()
    row = lambda a: a.reshape(1, d)

    for layer in range(DEPTH):
        j = layer // 2
        if layer % 2 == 0:
            hproj = _proj(x, ev_w_in[j].astype(BF16), ROW_TILE, 512)
            ya = _retention(hproj, tab_ret, ret_consts)
            yb = _dilated(hproj, tab_dil)
            w_out = ev_w_out[j].astype(BF16)
            x = _out_ln(x, [ya, yb], [w_out[:RET_V_W], w_out[RET_V_W:]], row(ln1_g[layer]), row(ln1_b[layer]),
                        ROW_TILE)
        else:
            w_in = od_w_in[j]
            w_main = w_in[:, :4 * GDN_W].astype(BF16)
            w_small = jnp.pad(w_in[:, 4 * GDN_W:], ((0, 0), (0, LANES - 2 * GDN_HEADS))).astype(BF16)
            cw = jnp.pad(od_conv_w[j], ((0, 8 - GDN_CONV), (0, 0)))
            qkvg, small = _odd_in(x, w_main, cw, w_small, ROW_TILE)
            yc = _gdn(qkvg, small, od_a_log[j], od_dt_bias[j], od_norm_w[j])
            x = _out_ln(x, [yc], [od_w_out[j].astype(BF16)], row(ln1_g[layer]), row(ln1_b[layer]), ROW_TILE)
        wup_r, cw_r, wdn_r = _ffn_weights(ffn_w_up[layer], ffn_conv_w[layer], ffn_conv_b[layer], ffn_w_down[layer])
        x = _ffn(x, wup_r, cw_r, wdn_r, row(ln2_g[layer]), row(ln2_b[layer]), ROW_TILE)
    return x
```

```python
import functools
import math

import jax
import jax.numpy as jnp
import numpy as np
from jax import lax
from jax.experimental import pallas as pl
from jax.experimental.pallas import tpu as pltpu

F32 = jnp.float32
BF16 = jnp.bfloat16

D_MODEL = 1024
DEPTH = 4
RET_HEADS, RET_DK, RET_DV, RET_CHUNK, RET_THETA = 4, 128, 256, 128, 10000.0
DIL_HEADS, DIL_HD, DIL_BLOCK = 8, 64, 128
DIL_DILATIONS = (1, 4, 16)
DIL_INTERLEAVE = 2
ROPE_THETA, ROPE_DIMS = 500000.0, DIL_HD // 4
GDN_HEADS, GDN_DK, GDN_DV, GDN_CHUNK, GDN_CONV = 8, 128, 128, 64, 4
D_FF, FFN_CONV = 2816, 3
DN_ALPHA = (2.0 * DEPTH) ** 0.25
EPS = 1e-5

RET_QK_W = RET_HEADS * RET_DK
RET_V_W = RET_HEADS * RET_DV
DIL_W = DIL_HEADS * DIL_HD
EV_IN = 2 * RET_QK_W + 2 * RET_V_W + 3 * DIL_W
GDN_W = GDN_HEADS * GDN_DK

LANES = 128
HALO = 16
ROW_TILE = 1024
VMEM_LIMIT = 56 * 1024 * 1024
HI = lax.Precision.HIGHEST


def _cparams(n_axes, vmem=VMEM_LIMIT):
    return pltpu.CompilerParams(dimension_semantics=("arbitrary",) * n_axes, vmem_limit_bytes=vmem)


def _dot(a, b):
    return jnp.dot(a, b, preferred_element_type=F32)


def _dot_nt(a, b):
    return lax.dot_general(a, b, (((1,), (1,)), ((), ())), preferred_element_type=F32)


def _dot_tn(a, b):
    return lax.dot_general(a, b, (((0,), (0,)), ((), ())), preferred_element_type=F32)


def _sigmoid(x):
    return 1.0 / (1.0 + jnp.exp(-x))


def _silu(x):
    return x * _sigmoid(x)


def _layernorm_rows(z, g, b):
    mu = jnp.mean(z, -1, keepdims=True)
    zc = z - mu
    var = jnp.mean(zc * zc, -1, keepdims=True)
    return zc * lax.rsqrt(var + EPS) * g + b


TRIG_ROWS = 256


def _proj_kernel(x_ref, w_ref, *rest, tn, tables):
    if tables:
        pos_ref, f_ref, o_ref, tr_ref, td_ref = rest
        tm = x_ref.shape[1]
        units = [(t, r) for r in range(tm // TRIG_ROWS) for t in range(4)]
    else:
        (o_ref,) = rest
        units = []
    n_pieces = w_ref.shape[1] // tn
    per_piece = -(-len(units) // n_pieces)
    xb = x_ref[0].astype(BF16)
    for j in range(n_pieces):
        cols = slice(j * tn, (j + 1) * tn)
        o_ref[0, :, cols] = _dot(xb, w_ref[:, cols]).astype(o_ref.dtype)
        for t, r in units[j * per_piece:(j + 1) * per_piece]:
            rows = slice(r * TRIG_ROWS, (r + 1) * TRIG_ROWS)
            ang = pos_ref[0, rows, :].astype(F32) * f_ref[t:t + 1, :]
            dst = tr_ref if t < 2 else td_ref
            dst[0, t % 2, rows, :] = jnp.cos(ang) if t % 2 == 0 else jnp.sin(ang)


def _proj(x3, w_bf16, tm, tn, pos3=None, freqs=None):
    bsz, s, k = x3.shape
    n = w_bf16.shape[1]
    tables = pos3 is not None
    in_specs = [pl.BlockSpec((1, tm, k), lambda bi, i: (bi, i, 0)),
                pl.BlockSpec((k, n), lambda bi, i: (0, 0), pipeline_mode=pl.Buffered(1))]
    out_specs = [pl.BlockSpec((1, tm, n), lambda bi, i: (bi, i, 0))]
    out_shape = [jax.ShapeDtypeStruct((bsz, s, n), BF16)]
    args = [x3, w_bf16]
    if tables:
        in_specs += [pl.BlockSpec((1, tm, 1), lambda bi, i: (bi, i, 0)),
                     pl.BlockSpec((8, LANES), lambda bi, i: (0, 0))]
        tab_spec = pl.BlockSpec((1, 2, tm, LANES), lambda bi, i: (bi, 0, i, 0))
        out_specs += [tab_spec, tab_spec]
        out_shape += [jax.ShapeDtypeStruct((bsz, 2, s, LANES), F32)] * 2
        args += [pos3, freqs]
    outs = pl.pallas_call(
        functools.partial(_proj_kernel, tn=tn, tables=tables),
        grid=(bsz, s // tm),
        in_specs=in_specs,
        out_specs=out_specs,
        out_shape=out_shape,
        compiler_params=_cparams(2),
        name="ev_in_proj",
    )(*args)
    return outs if tables else outs[0]


def _rope_freqs():
    half = RET_DK // 2
    inv = jnp.power(RET_THETA, -jnp.arange(half, dtype=F32) * 2.0 / RET_DK)
    ret = jnp.stack([jnp.concatenate([inv, inv]), jnp.concatenate([-inv, inv])])
    hh = ROPE_DIMS // 2
    invd = jnp.power(ROPE_THETA, -jnp.arange(hh, dtype=F32) * 2.0 / ROPE_DIMS)
    z = jnp.zeros((DIL_HD - ROPE_DIMS,), F32)
    f_cos = jnp.concatenate([invd, invd, z])
    f_sin = jnp.concatenate([-invd, invd, z])
    dil = jnp.stack([jnp.tile(f_cos, 2), jnp.tile(f_sin, 2)])
    return jnp.concatenate([ret, dil, jnp.zeros((4, LANES), F32)], 0)


def _ret_kernel(gch_ref, q_ref, k_ref, v_ref, g_ref, tab_ref, dm_ref, zt_ref, xi_ref, o_ref, r_ref, *, n_chunks):
    r_ref[...] = jnp.zeros_like(r_ref)
    c = RET_CHUNK
    heads = range(RET_HEADS)

    def body(ci, carry):
        rows = pl.ds(pl.multiple_of(ci * c, c), c)
        cosr = tab_ref[0, 0, rows, :]
        sinr = tab_ref[0, 1, rows, :]
        qk = lambda h: slice(h * RET_DK, (h + 1) * RET_DK)
        vv = lambda h: slice(h * RET_DV, (h + 1) * RET_DV)
        qs = [q_ref[0, rows, qk(h)].astype(F32) for h in heads]
        ks = [k_ref[0, rows, qk(h)].astype(F32) for h in heads]
        qr = [q * cosr + pltpu.roll(q, RET_DK // 2, 1) * sinr for q in qs]
        kr = [(k * cosr + pltpu.roll(k, RET_DK // 2, 1) * sinr) * (RET_DK ** -0.5) for k in ks]
        vs = [v_ref[0, rows, vv(h)] for h in heads]
        r_prev = [r_ref[h] for h in heads]
        scores = [_dot_nt(qr[h].astype(BF16), kr[h].astype(BF16)) * dm_ref[h] for h in heads]
        inter = [_dot((qr[h] * xi_ref[h]).astype(BF16), r_prev[h].astype(BF16)) for h in heads]
        kv = [_dot_tn((kr[h] * zt_ref[h]).astype(BF16), vs[h]) for h in heads]
        os_ = [_dot(scores[h].astype(BF16), vs[h]) + inter[h] for h in heads]
        for h in heads:
            r_ref[h] = r_prev[h] * gch_ref[h] + kv[h]
            o = os_[h]
            y = o * lax.rsqrt(jnp.mean(o * o, -1, keepdims=True) + EPS)
            g = g_ref[0, rows, vv(h)].astype(F32)
            o_ref[0, rows, vv(h)] = (y * _silu(g)).astype(o_ref.dtype)
        return carry

    lax.fori_loop(0, n_chunks, body, 0, unroll=2)


def _ret_consts():
    hh = np.arange(RET_HEADS, dtype=np.float64)
    lg = np.log1p(-np.power(2.0, -5.0 - hh))
    idx = np.arange(RET_CHUNK, dtype=np.float64)
    rel = idx[:, None] - idx[None, :]
    dmat = np.where(rel >= 0, np.exp(np.maximum(rel, 0.0) * lg[:, None, None]), 0.0)
    zeta = np.exp((RET_CHUNK - 1 - idx) * lg[:, None])
    xi = np.exp((idx + 1) * lg[:, None])
    gch = np.exp(RET_CHUNK * lg)
    bc = lambda a: np.broadcast_to(a[:, :, None], (RET_HEADS, RET_CHUNK, RET_DK))
    return (jnp.asarray(gch, F32), jnp.asarray(dmat, F32), jnp.asarray(bc(zeta), F32), jnp.asarray(bc(xi), F32))


def _retention(hproj, tab_ret, consts):
    b, s, _ = hproj.shape
    gch, dmat, zeta, xi = consts
    v_off = 2 * RET_QK_W // RET_V_W
    all_heads = pl.BlockSpec((RET_HEADS, RET_CHUNK, RET_DK), lambda i: (0, 0, 0))
    return pl.pallas_call(
        functools.partial(_ret_kernel, n_chunks=s // RET_CHUNK),
        grid=(b,),
        in_specs=[pl.BlockSpec(memory_space=pltpu.SMEM),
                  pl.BlockSpec((1, s, RET_QK_W), lambda i: (i, 0, 0)),
                  pl.BlockSpec((1, s, RET_QK_W), lambda i: (i, 0, 1)),
                  pl.BlockSpec((1, s, RET_V_W), lambda i: (i, 0, v_off)),
                  pl.BlockSpec((1, s, RET_V_W), lambda i: (i, 0, v_off + 1)),
                  pl.BlockSpec((1, 2, s, LANES), lambda i: (i, 0, 0, 0)),
                  all_heads, all_heads, all_heads],
        out_specs=pl.BlockSpec((1, s, RET_V_W), lambda i: (i, 0, 0)),
        out_shape=jax.ShapeDtypeStruct((b, s, RET_V_W), BF16),
        scratch_shapes=[pltpu.VMEM((RET_HEADS, RET_DK, RET_DV), F32)],
        compiler_params=_cparams(1),
        name="retention",
    )(gch, hproj, hproj, hproj, hproj, tab_ret, dmat, zeta, xi)


def _dil_kernel(q_ref, k_ref, v_ref, tab_ref, o_ref, qs, ks, vs, acc, ms, ls, *, seq):
    rc = 256
    lane = lax.broadcasted_iota(jnp.int32, (DIL_BLOCK, LANES), 1)
    first_head = lane < DIL_HD

    def rot_body(i, carry):
        rows = pl.ds(pl.multiple_of(i * rc, rc), rc)
        cosd = tab_ref[0, 0, rows, :]
        sind = tab_ref[0, 1, rows, :]
        hh = ROPE_DIMS // 2
        low_half = (lax.broadcasted_iota(jnp.int32, (rc, LANES), 1) & (DIL_HD - 1)) < hh

        def rot(x):
            return x * cosd + jnp.where(low_half, pltpu.roll(x, LANES - hh, 1), pltpu.roll(x, hh, 1)) * sind

        qs[rows, :] = rot(q_ref[0, rows, :].astype(F32)) * (DIL_HD ** -0.5)
        ks[rows, :] = rot(k_ref[0, rows, :].astype(F32))
        vs[rows, :] = v_ref[0, rows, :].astype(F32)
        return carry

    lax.fori_loop(0, seq // rc, rot_body, 0)

    def blocks(g, d, kb_rows, specs):
        def sl(start, n):
            return pl.ds(start, n) if d == 1 else pl.ds(start, n, stride=d)

        qsl = [sl(q0, DIL_BLOCK) for q0, _, _ in specs]
        ksl = [sl(k0, kb_rows) for _, k0, _ in specs]
        qbs = [qs[s_, :] for s_ in qsl]
        kbs = [ks[s_, :].astype(BF16) for s_ in ksl]
        vbs = [vs[s_, :].astype(BF16) for s_ in ksl]
        nb_ = range(len(specs))
        q2 = [jnp.concatenate([jnp.where(first_head, qbs[b_], 0.0), jnp.where(first_head, 0.0, qbs[b_])], 0)
              .astype(BF16) for b_ in nb_]
        sc = [_dot_nt(q2[b_], kbs[b_]) + jnp.concatenate([specs[b_][2]] * 2, 0) for b_ in nb_]
        yield
        mx = [jnp.max(s_, -1, keepdims=True) for s_ in sc]
        pr = [jnp.exp(s_ - m_).astype(BF16) for s_, m_ in zip(sc, mx)]
        yield
        ones_v = jnp.ones((kb_rows, LANES), BF16)
        pv = [_dot(pr[b_], jnp.concatenate([vbs[b_], ones_v], 1)) for b_ in nb_]
        yield
        h0, h1 = slice(0, DIL_BLOCK), slice(DIL_BLOCK, 2 * DIL_BLOCK)
        for b_ in nb_:
            acc[g, qsl[b_], :] = jnp.where(first_head, pv[b_][h0, :LANES], pv[b_][h1, :LANES])
            ms[g, qsl[b_], :] = jnp.where(first_head, mx[b_][h0], mx[b_][h1])
            ls[g, qsl[b_], :] = jnp.where(first_head, pv[b_][h0, LANES:], pv[b_][h1, LANES:])

    def window_bias(kb_rows, off):
        dist = (lax.broadcasted_iota(jnp.int32, (DIL_BLOCK, kb_rows), 0)
                - lax.broadcasted_iota(jnp.int32, (DIL_BLOCK, kb_rows), 1)) + off
        return jnp.where((dist >= 0) & (dist <= DIL_BLOCK), 0.0, -1e30).astype(F32)

    bias_self = window_bias(DIL_BLOCK, 0)
    bias_first = window_bias(2 * DIL_BLOCK, 0)
    bias_band = window_bias(2 * DIL_BLOCK, DIL_BLOCK)

    nbi = DIL_INTERLEAVE
    groups = []
    for g, d in enumerate(DIL_DILATIONS):
        nb = seq // d // DIL_BLOCK
        span = DIL_BLOCK * d
        if nb == 1:
            specs = [(r, r, bias_self) for r in range(d)]
            kb_rows = DIL_BLOCK
        else:
            specs = [(r + n * span, r + max(n - 1, 0) * span, bias_band if n else bias_first)
                     for n in range(nb) for r in range(d)]
            kb_rows = 2 * DIL_BLOCK
        groups += [blocks(g, d, kb_rows, specs[i:i + nbi]) for i in range(0, len(specs), nbi)]

    pending, active = iter(groups), []
    while True:
        nxt = next(pending, None)
        if nxt is not None:
            active.append(nxt)
        if not active:
            break
        for gen in list(active):
            try:
                next(gen)
            except StopIteration:
                active.remove(gen)

    def comb_body(i, carry):
        rows = pl.ds(pl.multiple_of(i * rc, rc), rc)
        m0, m1, m2 = ms[0, rows, :], ms[1, rows, :], ms[2, rows, :]
        mmax = jnp.maximum(jnp.maximum(m0, m1), m2)
        e0, e1, e2 = jnp.exp(m0 - mmax), jnp.exp(m1 - mmax), jnp.exp(m2 - mmax)
        num = e0 * acc[0, rows, :] + e1 * acc[1, rows, :] + e2 * acc[2, rows, :]
        den = e0 * ls[0, rows, :] + e1 * ls[1, rows, :] + e2 * ls[2, rows, :]
        o_ref[0, rows, :] = (num / den).astype(o_ref.dtype)
        return carry

    lax.fori_loop(0, seq // rc, comb_body, 0)


def _dilated(hproj, tab_dil):
    b, s, _ = hproj.shape
    base = (2 * RET_QK_W + 2 * RET_V_W) // LANES
    pairs = DIL_W // LANES
    blk = lambda off: pl.BlockSpec((1, s, LANES), lambda i, j: (i, 0, base + off + j))
    n_g = len(DIL_DILATIONS)
    return pl.pallas_call(
        functools.partial(_dil_kernel, seq=s),
        grid=(b, pairs),
        in_specs=[blk(0), blk(pairs), blk(2 * pairs),
                  pl.BlockSpec((1, 2, s, LANES), lambda i, j: (i, 0, 0, 0))],
        out_specs=pl.BlockSpec((1, s, LANES), lambda i, j: (i, 0, j)),
        out_shape=jax.ShapeDtypeStruct((b, s, DIL_W), BF16),
        scratch_shapes=[pltpu.VMEM((s, LANES), F32)] * 3 + [pltpu.VMEM((n_g, s, LANES), F32)] * 3,
        compiler_params=_cparams(2),
        name="dilated_attention",
    )(hproj, hproj, hproj, tab_dil)


def _out_ln_kernel(*refs, n_in):
    x_ref = refs[0]
    ys = refs[1:1 + n_in]
    ws = refs[1 + n_in:1 + 2 * n_in]
    g_ref, b_ref, o_ref = refs[1 + 2 * n_in:]
    tm = x_ref.shape[1]
    halves = [slice(0, tm // 2), slice(tm // 2, tm)]
    zs = []
    for rows in halves:
        z = DN_ALPHA * x_ref[0, rows, :]
        for y_ref, w_ref in zip(ys, ws):
            z = z + _dot(y_ref[0, rows, :], w_ref[...])
        zs.append(z)
    for rows, z in zip(halves, zs):
        o_ref[0, rows, :] = _layernorm_rows(z, g_ref[...], b_ref[...])


def _out_ln(x3, ys, ws, g, b, tm):
    bsz, s, d = x3.shape
    n_in = len(ys)
    row = lambda w: pl.BlockSpec((1, tm, w), lambda bi, i: (bi, i, 0))
    full = lambda a: pl.BlockSpec(a.shape, lambda bi, i: (0, 0))
    return pl.pallas_call(
        functools.partial(_out_ln_kernel, n_in=n_in),
        grid=(bsz, s // tm),
        in_specs=[row(d)] + [row(y.shape[2]) for y in ys] + [full(w) for w in ws] + [full(g), full(b)],
        out_specs=row(d),
        out_shape=jax.ShapeDtypeStruct((bsz, s, d), F32),
        compiler_params=_cparams(2),
        name="out_proj_ln",
    )(x3, *ys, *ws, g, b)


def _conv_taps(u_ref, row0, rows, cols, taps):
    k_w = len(taps)
    ub = u_ref[row0 - 8:row0 + rows, cols]
    out = ub[8:] * taps[k_w - 1]
    for back in range(1, k_w):
        out = out + pltpu.roll(ub, back, 0)[8:] * taps[k_w - 1 - back]
    return out


def _fill_xb(xb, x_ref, xh_ref, i):
    xb[HALO:, :] = x_ref[0].astype(BF16)
    xb[0:HALO, :] = jnp.where(i > 0, xh_ref[0], 0.0).astype(BF16)


FFN_PIECE = 256


def _ffn_kernel(x_ref, xh_ref, wup_ref, cw_ref, wdn_ref, g_ref, b_ref, o_ref, xb, us, hs, *, tm, rb):
    i = pl.program_id(1)
    _fill_xb(xb, x_ref, xh_ref, i)
    pc = FFN_PIECE
    n_pieces = D_FF // pc

    def up(p):
        x = xb[...]
        us[p % 2, :, 0:pc] = _dot(x, wup_ref[:, pc * p:pc * (p + 1)])
        us[p % 2, :, pc:2 * pc] = _dot(x, wup_ref[:, D_FF + pc * p:D_FF + pc * (p + 1)])

    def gate_piece(p):
        u = us.at[p % 2]
        for r in range(tm // rb):
            row0 = r * rb + HALO
            for h in range(pc // LANES):
                cg = slice(h * LANES, (h + 1) * LANES)
                cv = slice(pc + h * LANES, pc + (h + 1) * LANES)
                wg = slice(pc * p + h * LANES, pc * p + (h + 1) * LANES)
                wv = slice(D_FF + pc * p + h * LANES, D_FF + pc * p + (h + 1) * LANES)
                tg = [cw_ref[j:j + 1, wg] for j in range(FFN_CONV)]
                tv = [cw_ref[j:j + 1, wv] for j in range(FFN_CONV)]
                gate = _conv_taps(u, row0, rb, cg, tg) + cw_ref[FFN_CONV:FFN_CONV + 1, wg]
                val = _conv_taps(u, row0, rb, cv, tv) + cw_ref[FFN_CONV:FFN_CONV + 1, wv]
                hs[r * rb:(r + 1) * rb, wg] = (_silu(gate) * val).astype(BF16)

    up(0)
    for p in range(1, n_pieces):
        up(p)
        gate_piece(p - 1)
    gate_piece(n_pieces - 1)
    ffn = _dot(hs[...], wdn_ref[...])
    o_ref[0] = _layernorm_rows(DN_ALPHA * x_ref[0] + ffn, g_ref[...], b_ref[...])


def _ffn(x3, wup_r, cw_r, wdn_r, g, b, tm):
    bsz, s, d = x3.shape
    hb = tm // HALO
    const = lambda a: pl.BlockSpec(a.shape, lambda bi, i: (0, 0), pipeline_mode=pl.Buffered(1))
    return pl.pallas_call(
        functools.partial(_ffn_kernel, tm=tm, rb=128),
        grid=(bsz, s // tm),
        in_specs=[pl.BlockSpec((1, tm, d), lambda bi, i: (bi, i, 0)),
                  pl.BlockSpec((1, HALO, d), lambda bi, i: (bi, jnp.maximum(i * hb - 1, 0), 0)),
                  const(wup_r), const(cw_r), const(wdn_r), const(g), const(b)],
        out_specs=pl.BlockSpec((1, tm, d), lambda bi, i: (bi, i, 0)),
        out_shape=jax.ShapeDtypeStruct((bsz, s, d), F32),
        scratch_shapes=[pltpu.VMEM((HALO + tm, d), BF16),
                        pltpu.VMEM((2, HALO + tm, 2 * FFN_PIECE), F32),
                        pltpu.VMEM((tm, D_FF), BF16)],
        compiler_params=_cparams(2),
        name="conv_ffn_ln",
    )(x3, x3, wup_r, cw_r, wdn_r, g, b)


def _ffn_weights(w_up, conv_w, conv_b, w_down):
    cw = jnp.concatenate([conv_w, conv_b[None, :], jnp.zeros((8 - FFN_CONV - 1, 2 * D_FF), F32)], 0)
    return w_up.astype(BF16), cw, w_down.astype(BF16)


ODD_PIECE = 512


def _odd_in_kernel(x_ref, xh_ref, w_ref, cw_ref, ws_ref, o_ref, os_ref, xb, us, *, tm, conv_cols, rb):
    i = pl.program_id(1)
    _fill_xb(xb, x_ref, xh_ref, i)
    pc = ODD_PIECE
    n_pieces = w_ref.shape[1] // pc

    def up(p):
        us[p % 2] = _dot(xb[...], w_ref[:, pc * p:pc * (p + 1)])

    def finish(p):
        u = us.at[p % 2]
        if pc * p >= conv_cols:
            o_ref[0, :, pc * p:pc * (p + 1)] = u[HALO:, :].astype(o_ref.dtype)
            return
        for r in range(tm // rb):
            for h in range(pc // LANES):
                cs = slice(h * LANES, (h + 1) * LANES)
                ws = slice(pc * p + h * LANES, pc * p + (h + 1) * LANES)
                taps = [cw_ref[j:j + 1, ws] for j in range(GDN_CONV)]
                y = _conv_taps(u, r * rb + HALO, rb, cs, taps)
                y = _silu(y)
                if pc * p + h * LANES < 2 * GDN_W:
                    y = y * lax.rsqrt(jnp.sum(y * y, -1, keepdims=True) + 1e-6)
                    if pc * p + h * LANES < GDN_W:
                        y = y * (GDN_DK ** -0.5)
                o_ref[0, r * rb:(r + 1) * rb, ws] = y.astype(o_ref.dtype)

    up(0)
    os_ref[0] = _dot(xb[HALO:, :], ws_ref[...])
    for p in range(1, n_pieces):
        up(p)
        finish(p - 1)
    finish(n_pieces - 1)


def _odd_in(x3, w_main, cw, w_small, tm):
    bsz, s, d = x3.shape
    n = w_main.shape[1]
    hb = tm // HALO
    const = lambda a: pl.BlockSpec(a.shape, lambda bi, i: (0, 0), pipeline_mode=pl.Buffered(1))
    return pl.pallas_call(
        functools.partial(_odd_in_kernel, tm=tm, conv_cols=cw.shape[1], rb=128),
        grid=(bsz, s // tm),
        in_specs=[pl.BlockSpec((1, tm, d), lambda bi, i: (bi, i, 0)),
                  pl.BlockSpec((1, HALO, d), lambda bi, i: (bi, jnp.maximum(i * hb - 1, 0), 0)),
                  const(w_main), const(cw), const(w_small)],
        out_specs=[pl.BlockSpec((1, tm, n), lambda bi, i: (bi, i, 0)),
                   pl.BlockSpec((1, tm, LANES), lambda bi, i: (bi, i, 0))],
        out_shape=[jax.ShapeDtypeStruct((bsz, s, n), BF16),
                   jax.ShapeDtypeStruct((bsz, s, LANES), F32)],
        scratch_shapes=[pltpu.VMEM((HALO + tm, d), BF16),
                        pltpu.VMEM((2, HALO + tm, ODD_PIECE), F32)],
        compiler_params=_cparams(2),
        name="od_in_proj_conv",
    )(x3, x3, w_main, cw, w_small)


GDN_WIDE = 4
GDN_BLK = GDN_CHUNK * GDN_WIDE
GDN_HG = 8
GDN_LOCKSTEP = 2


def _split_bf16(a):
    hi = a.astype(BF16)
    return hi, (a - hi.astype(F32)).astype(BF16)


def _gdn_kernel(prm_ref, q_ref, k_ref, v_ref, gt_ref, sm_ref, nw_ref, o_ref,
                mq_s, nn_s, op_s, eg_s, st_s, rhs_s, kd_s, qd_s, at_s, *, seq):
    hg = pl.program_id(1)
    c, wd, blk = GDN_CHUNK, GDN_WIDE, GDN_BLK
    ww = c * wd
    n_blocks = seq // blk
    shift = int(math.log2(c))

    r_w = lax.broadcasted_iota(jnp.int32, (c, ww), 0)
    l_w = lax.broadcasted_iota(jnp.int32, (c, ww), 1)
    j_w = l_w & (c - 1)
    chunk_of_lane = l_w >> shift
    tri_w = r_w >= j_w
    strict_w = r_w > j_w
    upper_w = jnp.where(r_w <= j_w, 1.0, 0.0).astype(F32)
    bmask = (lax.broadcasted_iota(jnp.int32, (ww, ww), 0) >> shift) == (lax.broadcasted_iota(jnp.int32, (ww, ww), 1) >> shift)
    ltri = jnp.where(lax.broadcasted_iota(jnp.int32, (c, c), 0) >= lax.broadcasted_iota(jnp.int32, (c, c), 1),
                     1.0, 0.0).astype(BF16)
    ones_cc = jnp.ones((c, c), BF16)
    lane_blk = lax.broadcasted_iota(jnp.int32, (blk, LANES), 1)
    first_half = lax.broadcasted_iota(jnp.int32, (c, LANES), 1) < c
    a_neg = -jnp.exp(prm_ref[0:1, :])
    dtb = prm_ref[1:2, :]

    def bdiag(xw):
        return jnp.where(bmask, jnp.concatenate([xw] * wd, 0), 0.0).astype(BF16)

    hs_all = list(range(GDN_HG))

    def widen(col):
        out = jnp.broadcast_to(col[0:c], (c, ww))
        for ch in range(1, wd):
            out = jnp.where(chunk_of_lane == ch, jnp.broadcast_to(col[ch * c:(ch + 1) * c], (c, ww)), out)
        return out

    def prep_steps(i, slot, hs_):
        rows = pl.ds(pl.multiple_of(i * blk, blk), blk)
        sm = sm_ref[0, rows, :]
        beta_all = _sigmoid(sm)
        sp_in = sm + dtb
        g_all = a_neg * (jnp.maximum(sp_in, 0.0) + jnp.log1p(jnp.exp(-jnp.abs(sp_in))))
        cols = {hh: slice(hh * GDN_DK, (hh + 1) * GDN_DK) for hh in hs_}
        head = {hh: hg * GDN_HG + hh for hh in hs_}
        qn = {hh: q_ref[0, rows, cols[hh]].astype(F32) for hh in hs_}
        kn = {hh: k_ref[0, rows, cols[hh]].astype(F32) for hh in hs_}
        beta = {hh: jnp.sum(jnp.where(lane_blk == head[hh], beta_all, 0.0), -1, keepdims=True) for hh in hs_}
        g = {hh: jnp.sum(jnp.where(lane_blk == GDN_HEADS + head[hh], g_all, 0.0), -1, keepdims=True) for hh in hs_}
        kb = {hh: kn[hh] * beta[hh] for hh in hs_}
        g_w = {hh: widen(g[hh]) for hh in hs_}
        g_sp = {hh: _split_bf16(g_w[hh]) for hh in hs_}
        gu_sp = {hh: _split_bf16(g_w[hh] * upper_w) for hh in hs_}
        gc_col = {hh: _dot(ltri, g_sp[hh][0]) + _dot(ltri, g_sp[hh][1]) for hh in hs_}
        gc_row = {hh: _dot(ones_cc, gu_sp[hh][0]) + _dot(ones_cc, gu_sp[hh][1]) for hh in hs_}
        yield
        grams = {hh: [_dot_nt(jnp.concatenate([kb[hh][2 * p * c:(2 * p + 2) * c], qn[hh][2 * p * c:(2 * p + 2) * c]], 0)
                              .astype(BF16), kn[hh][2 * p * c:(2 * p + 2) * c].astype(BF16))
                      for p in range(wd // 2)] for hh in hs_}
        yield
        pws = {}
        for hh in hs_:
            decay = jnp.where(tri_w, jnp.exp(jnp.where(tri_w, gc_col[hh] - gc_row[hh], 0.0)), 0.0)
            a_w = jnp.concatenate([jnp.where(first_half, gm[0:c], gm[c:2 * c]) for gm in grams[hh]], 1)
            qk_w = jnp.concatenate([jnp.where(first_half, gm[2 * c:3 * c], gm[3 * c:4 * c]) for gm in grams[hh]], 1)
            pws[hh] = -jnp.where(strict_w, a_w * decay, 0.0)
            at_s[hh] = jnp.where(tri_w, qk_w * decay, 0.0)
        ews = dict(pws)
        pws = {hh: _dot(pws[hh].astype(BF16), bdiag(pws[hh])) for hh in hs_}
        for hh in hs_:
            gc_parts, gl_parts = [], []
            for p in range(wd // 2):
                x = gc_col[hh][:, 2 * p * c:(2 * p + 2) * c]
                xr = pltpu.roll(x, c, 1)
                for part in (jnp.where(first_half, x, xr), jnp.where(first_half, xr, x)):
                    gc_parts.append(part)
                    gl_parts.append(jnp.broadcast_to(part[c - 1:c, :], (c, LANES)))
            gc_t = jnp.concatenate(gc_parts, 0)
            gl_t = jnp.concatenate(gl_parts, 0)
            egc = jnp.exp(gc_t)
            rhs_s[hh, :, 0:GDN_DK] = kb[hh] * egc
            rhs_s[hh, :, GDN_DK:] = v_ref[0, rows, cols[hh]].astype(F32) * beta[hh]
            kd_s[hh] = (kn[hh] * jnp.exp(gl_t - gc_t)).astype(BF16)
            qd_s[hh] = qn[hh] * egc
            for ch in range(wd):
                eg_s[slot, hh, ch * 8:(ch + 1) * 8, :] = jnp.broadcast_to(
                    jnp.exp(gl_t[ch * c:ch * c + 1, :]), (8, LANES))
        yield
        for _ in range(shift - 1):
            both = {hh: _dot(jnp.concatenate([ews[hh], pws[hh]], 0).astype(BF16), bdiag(pws[hh])) for hh in hs_}
            ews = {hh: ews[hh] + pws[hh] + both[hh][0:c] for hh in hs_}
            pws = {hh: both[hh][c:2 * c] for hh in hs_}
            yield
        rhss = {hh: rhs_s[hh] for hh in hs_}
        solb = {hh: (rhss[hh] + _dot(bdiag(ews[hh]), rhss[hh].astype(BF16))).astype(BF16) for hh in hs_}
        yield
        asol = {hh: _dot(bdiag(at_s[hh]), solb[hh]) for hh in hs_}
        for hh in hs_:
            op_s[slot, hh] = asol[hh][:, GDN_DK:]
            qp = (qd_s[hh] - asol[hh][:, :GDN_DK]).astype(BF16)
            for ch in range(wd):
                mq_s[slot, hh, ch, GDN_DK:GDN_DK + c, :] = qp[ch * c:(ch + 1) * c]
        yield
        for ch in range(wd):
            cr = slice(ch * c, (ch + 1) * c)
            mns = {hh: _dot_tn(kd_s[hh, cr, :], solb[hh][cr]) for hh in hs_}
            for hh in hs_:
                mq_s[slot, hh, ch, 0:GDN_DK, :] = mns[hh][:, :GDN_DK].astype(BF16)
                nn_s[slot, hh, ch] = mns[hh][:, GDN_DK:]
            if ch % 2 == 1:
                yield

    def scan_steps(i, slot):
        for ch in range(wd):
            rows = pl.ds(pl.multiple_of(i * blk + ch * c, c), c)
            for hh in hs_all:
                cols = slice(hh * GDN_DK, (hh + 1) * GDN_DK)
                state = st_s[hh]
                x = _dot(mq_s[slot, hh, ch], state.astype(BF16))
                st_s[hh] = state * eg_s[slot, hh, ch * 8:ch * 8 + 1, :] - x[:GDN_DK] + nn_s[slot, hh, ch]
                o = x[GDN_DK:] + op_s[slot, hh, ch * c:(ch + 1) * c, :]
                y = o * lax.rsqrt(jnp.mean(o * o, -1, keepdims=True) + EPS) * nw_ref[...]
                gt = gt_ref[0, rows, cols].astype(F32)
                o_ref[0, rows, cols] = (y * _silu(gt)).astype(o_ref.dtype)
            yield

    def interleave(gens):
        gens = list(gens)
        while gens:
            for gen in list(gens):
                try:
                    next(gen)
                except StopIteration:
                    gens.remove(gen)

    def stage(prep_i, prep_slot, scan_i, scan_slot):
        gens = []
        if prep_i is not None:
            gens += [prep_steps(prep_i, prep_slot, hs_all[g0:g0 + GDN_LOCKSTEP])
                     for g0 in range(0, GDN_HG, GDN_LOCKSTEP)]
        if scan_i is not None:
            gens.append(scan_steps(scan_i, scan_slot))
        interleave(gens)

    assert n_blocks >= 2 and n_blocks % 2 == 0
    st_s[...] = jnp.zeros_like(st_s)
    stage(0, 0, None, None)

    def body(j, carry):
        stage(2 * j + 1, 1, 2 * j, 0)
        stage(2 * j + 2, 0, 2 * j + 1, 1)
        return carry

    lax.fori_loop(0, (n_blocks - 2) // 2, body, 0)
    stage(n_blocks - 1, 1, n_blocks - 2, 0)
    stage(None, None, n_blocks - 1, 1)


def _gdn(qkvg, small, a_log, dt_bias, norm_w):
    b, s, _ = qkvg.shape
    gw = GDN_HG * GDN_DK
    groups = GDN_HEADS // GDN_HG
    blk = lambda off: pl.BlockSpec((1, s, gw), lambda i, j: (i, 0, off + j), pipeline_mode=pl.Buffered(1))
    prm = jnp.zeros((8, LANES), F32)
    prm = prm.at[0, GDN_HEADS:2 * GDN_HEADS].set(a_log).at[1, GDN_HEADS:2 * GDN_HEADS].set(dt_bias)
    return pl.pallas_call(
        functools.partial(_gdn_kernel, seq=s),
        grid=(b, groups),
        in_specs=[pl.BlockSpec((8, LANES), lambda i, j: (0, 0)),
                  blk(0), blk(groups), blk(2 * groups), blk(3 * groups),
                  pl.BlockSpec((1, s, LANES), lambda i, j: (i, 0, 0)),
                  pl.BlockSpec((1, GDN_DV), lambda i, j: (0, 0))],
        out_specs=pl.BlockSpec((1, s, gw), lambda i, j: (i, 0, j)),
        out_shape=jax.ShapeDtypeStruct((b, s, GDN_W), BF16),
        scratch_shapes=[pltpu.VMEM((2, GDN_HG, GDN_WIDE, GDN_DK + GDN_CHUNK, GDN_DK), BF16),
                        pltpu.VMEM((2, GDN_HG, GDN_WIDE, GDN_DK, GDN_DV), F32),
                        pltpu.VMEM((2, GDN_HG, GDN_BLK, GDN_DV), F32),
                        pltpu.VMEM((2, GDN_HG, GDN_WIDE * 8, LANES), F32),
                        pltpu.VMEM((GDN_HG, GDN_DK, GDN_DV), F32),
                        pltpu.VMEM((GDN_HG, GDN_BLK, GDN_DK + GDN_DV), F32),
                        pltpu.VMEM((GDN_HG, GDN_BLK, GDN_DK), BF16),
                        pltpu.VMEM((GDN_HG, GDN_BLK, GDN_DK), F32),
                        pltpu.VMEM((GDN_HG, GDN_CHUNK, GDN_BLK), F32)],
        compiler_params=_cparams(2),
        name="gated_delta_rule",
    )(prm, qkvg, qkvg, qkvg, qkvg, small, norm_w.reshape(1, GDN_DV))


def kernel(x, positions, ev_w_in, ev_w_out, od_w_in, od_conv_w, od_a_log, od_dt_bias, od_norm_w, od_w_out,
           ffn_w_up, ffn_conv_w, ffn_conv_b, ffn_w_down, ln1_g, ln1_b, ln2_g, ln2_b):
    b, s, d = x.shape
    pos3 = positions.reshape(b, s, 1)
    ret_consts = _ret_consts()
    row = lambda a: a.reshape(1, d)

    for layer in range(DEPTH):
        j = layer // 2
        if layer == 0:
            hproj, tab_ret, tab_dil = _proj(x, ev_w_in[j].astype(BF16), ROW_TILE, 512, pos3, _rope_freqs())
        elif layer % 2 == 0:
            hproj = _proj(x, ev_w_in[j].astype(BF16), ROW_TILE, 512)
        if layer % 2 == 0:
            ya = _retention(hproj, tab_ret, ret_consts)
            yb = _dilated(hproj, tab_dil)
            w_out = ev_w_out[j].astype(BF16)
            x = _out_ln(x, [ya, yb], [w_out[:RET_V_W], w_out[RET_V_W:]], row(ln1_g[layer]), row(ln1_b[layer]),
                        ROW_TILE)
        else:
            w_in = od_w_in[j]
            w_main = w_in[:, :4 * GDN_W].astype(BF16)
            w_small = jnp.pad(w_in[:, 4 * GDN_W:], ((0, 0), (0, LANES - 2 * GDN_HEADS))).astype(BF16)
            cw = jnp.pad(od_conv_w[j], ((0, 8 - GDN_CONV), (0, 0)))
            qkvg, small = _odd_in(x, w_main, cw, w_small, ROW_TILE)
            yc = _gdn(qkvg, small, od_a_log[j], od_dt_bias[j], od_norm_w[j])
            x = _out_ln(x, [yc], [od_w_out[j].astype(BF16)], row(ln1_g[layer]), row(ln1_b[layer]), ROW_TILE)
        wup_r, cw_r, wdn_r = _ffn_weights(ffn_w_up[layer], ffn_conv_w[layer], ffn_conv_b[layer], ffn_w_down[layer])
        x = _ffn(x, wup_r, cw_r, wdn_r, row(ln2_g[layer]), row(ln2_b[layer]), ROW_TILE)
    return x
```

```python
import functools
import math

import jax
import jax.numpy as jnp
import numpy as np
from jax import lax
from jax.experimental import pallas as pl
from jax.experimental.pallas import tpu as pltpu

F32 = jnp.float32
BF16 = jnp.bfloat16

D_MODEL = 1024
DEPTH = 4
RET_HEADS, RET_DK, RET_DV, RET_CHUNK, RET_THETA = 4, 128, 256, 128, 10000.0
DIL_HEADS, DIL_HD, DIL_BLOCK = 8, 64, 128
DIL_DILATIONS = (1, 4, 16)
DIL_INTERLEAVE = 2
ROPE_THETA, ROPE_DIMS = 500000.0, DIL_HD // 4
GDN_HEADS, GDN_DK, GDN_DV, GDN_CHUNK, GDN_CONV = 8, 128, 128, 64, 4
D_FF, FFN_CONV = 2816, 3
DN_ALPHA = (2.0 * DEPTH) ** 0.25
EPS = 1e-5

RET_QK_W = RET_HEADS * RET_DK
RET_V_W = RET_HEADS * RET_DV
DIL_W = DIL_HEADS * DIL_HD
EV_IN = 2 * RET_QK_W + 2 * RET_V_W + 3 * DIL_W
GDN_W = GDN_HEADS * GDN_DK

LANES = 128
HALO = 16
ROW_TILE = 1024
VMEM_LIMIT = 56 * 1024 * 1024


def _cparams(n_axes, vmem=VMEM_LIMIT):
    return pltpu.CompilerParams(dimension_semantics=("arbitrary",) * n_axes, vmem_limit_bytes=vmem)


def _dot(a, b):
    return jnp.dot(a, b, preferred_element_type=F32)


def _dot_nt(a, b):
    return lax.dot_general(a, b, (((1,), (1,)), ((), ())), preferred_element_type=F32)


def _dot_tn(a, b):
    return lax.dot_general(a, b, (((0,), (0,)), ((), ())), preferred_element_type=F32)


def _sigmoid(x):
    return 1.0 / (1.0 + jnp.exp(-x))


def _silu(x):
    return x * _sigmoid(x)


def _layernorm_rows(z, g, b):
    mu = jnp.mean(z, -1, keepdims=True)
    zc = z - mu
    var = jnp.mean(zc * zc, -1, keepdims=True)
    return zc * lax.rsqrt(var + EPS) * g + b


TRIG_ROWS = 256


def _proj_kernel(x_ref, w_ref, *rest, tn, tables):
    if tables:
        pos_ref, f_ref, o_ref, tr_ref, td_ref = rest
        tm = x_ref.shape[1]
        units = [(dst, r) for r in range(tm // TRIG_ROWS) for dst in (tr_ref, td_ref)]
    else:
        (o_ref,) = rest
        units = []
    n_pieces = w_ref.shape[1] // tn
    per_piece = -(-len(units) // n_pieces)
    xb = x_ref[0].astype(BF16)
    for j in range(n_pieces):
        cols = slice(j * tn, (j + 1) * tn)
        o_ref[0, :, cols] = _dot(xb, w_ref[:, cols]).astype(o_ref.dtype)
        for dst, r in units[j * per_piece:(j + 1) * per_piece]:
            rows = slice(r * TRIG_ROWS, (r + 1) * TRIG_ROWS)
            f0 = 0 if dst is tr_ref else 2
            ang = pos_ref[0, rows, :].astype(F32) * f_ref[f0:f0 + 1, :]
            dst[0, 0, rows, :] = jnp.cos(ang)
            dst[0, 1, rows, :] = jnp.sin(ang) * f_ref[f0 + 1:f0 + 2, :]


def _proj(x3, w_bf16, tm, tn, pos3=None, freqs=None):
    bsz, s, k = x3.shape
    n = w_bf16.shape[1]
    tables = pos3 is not None
    in_specs = [pl.BlockSpec((1, tm, k), lambda bi, i: (bi, i, 0)),
                pl.BlockSpec((k, n), lambda bi, i: (0, 0), pipeline_mode=pl.Buffered(1))]
    out_specs = [pl.BlockSpec((1, tm, n), lambda bi, i: (bi, i, 0))]
    out_shape = [jax.ShapeDtypeStruct((bsz, s, n), BF16)]
    args = [x3, w_bf16]
    if tables:
        in_specs += [pl.BlockSpec((1, tm, 1), lambda bi, i: (bi, i, 0)),
                     pl.BlockSpec((8, LANES), lambda bi, i: (0, 0))]
        tab_spec = pl.BlockSpec((1, 2, tm, LANES), lambda bi, i: (bi, 0, i, 0))
        out_specs += [tab_spec, tab_spec]
        out_shape += [jax.ShapeDtypeStruct((bsz, 2, s, LANES), F32)] * 2
        args += [pos3, freqs]
    outs = pl.pallas_call(
        functools.partial(_proj_kernel, tn=tn, tables=tables),
        grid=(bsz, s // tm),
        in_specs=in_specs,
        out_specs=out_specs,
        out_shape=out_shape,
        compiler_params=_cparams(2),
        name="ev_in_proj",
    )(*args)
    return outs if tables else outs[0]


def _rope_freqs():
    half = RET_DK // 2
    inv = jnp.power(RET_THETA, -jnp.arange(half, dtype=F32) * 2.0 / RET_DK)
    one = jnp.ones((half,), F32)
    ret = jnp.stack([jnp.concatenate([inv, inv]), jnp.concatenate([-one, one])])
    hh = ROPE_DIMS // 2
    invd = jnp.power(ROPE_THETA, -jnp.arange(hh, dtype=F32) * 2.0 / ROPE_DIMS)
    z = jnp.zeros((DIL_HD - ROPE_DIMS,), F32)
    oneh = jnp.ones((hh,), F32)
    f_ang = jnp.concatenate([invd, invd, z])
    f_sgn = jnp.concatenate([-oneh, oneh, z])
    dil = jnp.stack([jnp.tile(f_ang, 2), jnp.tile(f_sgn, 2)])
    return jnp.concatenate([ret, dil, jnp.zeros((4, LANES), F32)], 0)


def _ret_kernel(gch_ref, q_ref, k_ref, v_ref, g_ref, tab_ref, dm_ref, zt_ref, xi_ref, o_ref, r_ref, *, n_chunks):
    r_ref[...] = jnp.zeros_like(r_ref)
    c = RET_CHUNK
    heads = range(RET_HEADS)

    def body(ci, carry):
        rows = pl.ds(pl.multiple_of(ci * c, c), c)
        cosr = tab_ref[0, 0, rows, :]
        sinr = tab_ref[0, 1, rows, :]
        qk = lambda h: slice(h * RET_DK, (h + 1) * RET_DK)
        vv = lambda h: slice(h * RET_DV, (h + 1) * RET_DV)
        qs = [q_ref[0, rows, qk(h)].astype(F32) for h in heads]
        ks = [k_ref[0, rows, qk(h)].astype(F32) for h in heads]
        qr = [q * cosr + pltpu.roll(q, RET_DK // 2, 1) * sinr for q in qs]
        kr = [(k * cosr + pltpu.roll(k, RET_DK // 2, 1) * sinr) * (RET_DK ** -0.5) for k in ks]
        vs = [v_ref[0, rows, vv(h)] for h in heads]
        r_prev = [r_ref[h] for h in heads]
        scores = [_dot_nt(qr[h].astype(BF16), kr[h].astype(BF16)) * dm_ref[h] for h in heads]
        inter = [_dot((qr[h] * xi_ref[h]).astype(BF16), r_prev[h].astype(BF16)) for h in heads]
        kv = [_dot_tn((kr[h] * zt_ref[h]).astype(BF16), vs[h]) for h in heads]
        os_ = [_dot(scores[h].astype(BF16), vs[h]) + inter[h] for h in heads]
        for h in heads:
            r_ref[h] = r_prev[h] * gch_ref[h] + kv[h]
            o = os_[h]
            y = o * lax.rsqrt(jnp.mean(o * o, -1, keepdims=True) + EPS)
            g = g_ref[0, rows, vv(h)].astype(F32)
            o_ref[0, rows, vv(h)] = (y * _silu(g)).astype(o_ref.dtype)
        return carry

    lax.fori_loop(0, n_chunks, body, 0, unroll=2)


def _ret_consts():
    hh = np.arange(RET_HEADS, dtype=np.float64)
    lg = np.log1p(-np.power(2.0, -5.0 - hh))
    idx = np.arange(RET_CHUNK, dtype=np.float64)
    rel = idx[:, None] - idx[None, :]
    dmat = np.where(rel >= 0, np.exp(np.maximum(rel, 0.0) * lg[:, None, None]), 0.0)
    zeta = np.exp((RET_CHUNK - 1 - idx) * lg[:, None])
    xi = np.exp((idx + 1) * lg[:, None])
    gch = np.exp(RET_CHUNK * lg)
    bc = lambda a: np.broadcast_to(a[:, :, None], (RET_HEADS, RET_CHUNK, RET_DK))
    return (jnp.asarray(gch, F32), jnp.asarray(dmat, F32), jnp.asarray(bc(zeta), F32), jnp.asarray(bc(xi), F32))


def _retention(hproj, tab_ret, consts):
    b, s, _ = hproj.shape
    gch, dmat, zeta, xi = consts
    v_off = 2 * RET_QK_W // RET_V_W
    all_heads = pl.BlockSpec((RET_HEADS, RET_CHUNK, RET_DK), lambda i: (0, 0, 0))
    return pl.pallas_call(
        functools.partial(_ret_kernel, n_chunks=s // RET_CHUNK),
        grid=(b,),
        in_specs=[pl.BlockSpec(memory_space=pltpu.SMEM),
                  pl.BlockSpec((1, s, RET_QK_W), lambda i: (i, 0, 0)),
                  pl.BlockSpec((1, s, RET_QK_W), lambda i: (i, 0, 1)),
                  pl.BlockSpec((1, s, RET_V_W), lambda i: (i, 0, v_off)),
                  pl.BlockSpec((1, s, RET_V_W), lambda i: (i, 0, v_off + 1)),
                  pl.BlockSpec((1, 2, s, LANES), lambda i: (i, 0, 0, 0)),
                  all_heads, all_heads, all_heads],
        out_specs=pl.BlockSpec((1, s, RET_V_W), lambda i: (i, 0, 0)),
        out_shape=jax.ShapeDtypeStruct((b, s, RET_V_W), BF16),
        scratch_shapes=[pltpu.VMEM((RET_HEADS, RET_DK, RET_DV), F32)],
        compiler_params=_cparams(1),
        name="retention",
    )(gch, hproj, hproj, hproj, hproj, tab_ret, dmat, zeta, xi)


def _dil_kernel(q_ref, k_ref, v_ref, tab_ref, o_ref, qs, ks, vs, acc, ms, ls, *, seq):
    rc = 256
    lane = lax.broadcasted_iota(jnp.int32, (DIL_BLOCK, LANES), 1)
    first_head = lane < DIL_HD

    def rot_body(i, carry):
        rows = pl.ds(pl.multiple_of(i * rc, rc), rc)
        cosd = tab_ref[0, 0, rows, :]
        sind = tab_ref[0, 1, rows, :]
        hh = ROPE_DIMS // 2
        low_half = (lax.broadcasted_iota(jnp.int32, (rc, LANES), 1) & (DIL_HD - 1)) < hh

        def rot(x):
            return x * cosd + jnp.where(low_half, pltpu.roll(x, LANES - hh, 1), pltpu.roll(x, hh, 1)) * sind

        qs[rows, :] = rot(q_ref[0, rows, :].astype(F32)) * (DIL_HD ** -0.5)
        ks[rows, :] = rot(k_ref[0, rows, :].astype(F32))
        vs[rows, :] = v_ref[0, rows, :].astype(F32)
        return carry

    lax.fori_loop(0, seq // rc, rot_body, 0)

    def blocks(g, d, kb_rows, specs):
        def sl(start, n):
            return pl.ds(start, n) if d == 1 else pl.ds(start, n, stride=d)

        qsl = [sl(q0, DIL_BLOCK) for q0, _, _ in specs]
        ksl = [sl(k0, kb_rows) for _, k0, _ in specs]
        qbs = [qs[s_, :] for s_ in qsl]
        kbs = [ks[s_, :].astype(BF16) for s_ in ksl]
        vbs = [vs[s_, :].astype(BF16) for s_ in ksl]
        nb_ = range(len(specs))
        q2 = [jnp.concatenate([jnp.where(first_head, qbs[b_], 0.0), jnp.where(first_head, 0.0, qbs[b_])], 0)
              .astype(BF16) for b_ in nb_]
        sc = [_dot_nt(q2[b_], kbs[b_]) + jnp.concatenate([specs[b_][2]] * 2, 0) for b_ in nb_]
        yield
        mx = [jnp.max(s_, -1, keepdims=True) for s_ in sc]
        pr = [jnp.exp(s_ - m_).astype(BF16) for s_, m_ in zip(sc, mx)]
        yield
        ones_v = jnp.ones((kb_rows, LANES), BF16)
        pv = [_dot(pr[b_], jnp.concatenate([vbs[b_], ones_v], 1)) for b_ in nb_]
        yield
        h0, h1 = slice(0, DIL_BLOCK), slice(DIL_BLOCK, 2 * DIL_BLOCK)
        for b_ in nb_:
            acc[g, qsl[b_], :] = jnp.where(first_head, pv[b_][h0, :LANES], pv[b_][h1, :LANES])
            ms[g, qsl[b_], :] = jnp.where(first_head, mx[b_][h0], mx[b_][h1])
            ls[g, qsl[b_], :] = jnp.where(first_head, pv[b_][h0, LANES:], pv[b_][h1, LANES:])

    def window_bias(kb_rows, off):
        dist = (lax.broadcasted_iota(jnp.int32, (DIL_BLOCK, kb_rows), 0)
                - lax.broadcasted_iota(jnp.int32, (DIL_BLOCK, kb_rows), 1)) + off
        return jnp.where((dist >= 0) & (dist <= DIL_BLOCK), 0.0, -jnp.inf).astype(F32)

    bias_self = window_bias(DIL_BLOCK, 0)
    bias_first = window_bias(2 * DIL_BLOCK, 0)
    bias_band = window_bias(2 * DIL_BLOCK, DIL_BLOCK)

    nbi = DIL_INTERLEAVE
    groups = []
    for g, d in enumerate(DIL_DILATIONS):
        nb = seq // d // DIL_BLOCK
        span = DIL_BLOCK * d
        if nb == 1:
            specs = [(r, r, bias_self) for r in range(d)]
            kb_rows = DIL_BLOCK
        else:
            specs = [(r + n * span, r + max(n - 1, 0) * span, bias_band if n else bias_first)
                     for n in range(nb) for r in range(d)]
            kb_rows = 2 * DIL_BLOCK
        groups += [blocks(g, d, kb_rows, specs[i:i + nbi]) for i in range(0, len(specs), nbi)]

    pending, active = iter(groups), []
    while True:
        nxt = next(pending, None)
        if nxt is not None:
            active.append(nxt)
        if not active:
            break
        for gen in list(active):
            try:
                next(gen)
            except StopIteration:
                active.remove(gen)

    def comb_body(i, carry):
        rows = pl.ds(pl.multiple_of(i * rc, rc), rc)
        m0, m1, m2 = ms[0, rows, :], ms[1, rows, :], ms[2, rows, :]
        mmax = jnp.maximum(jnp.maximum(m0, m1), m2)
        e0, e1, e2 = jnp.exp(m0 - mmax), jnp.exp(m1 - mmax), jnp.exp(m2 - mmax)
        num = e0 * acc[0, rows, :] + e1 * acc[1, rows, :] + e2 * acc[2, rows, :]
        den = e0 * ls[0, rows, :] + e1 * ls[1, rows, :] + e2 * ls[2, rows, :]
        o_ref[0, rows, :] = (num / den).astype(o_ref.dtype)
        return carry

    lax.fori_loop(0, seq // rc, comb_body, 0)


def _dilated(hproj, tab_dil):
    b, s, _ = hproj.shape
    base = (2 * RET_QK_W + 2 * RET_V_W) // LANES
    pairs = DIL_W // LANES
    blk = lambda off: pl.BlockSpec((1, s, LANES), lambda i, j: (i, 0, base + off + j))
    n_g = len(DIL_DILATIONS)
    return pl.pallas_call(
        functools.partial(_dil_kernel, seq=s),
        grid=(b, pairs),
        in_specs=[blk(0), blk(pairs), blk(2 * pairs),
                  pl.BlockSpec((1, 2, s, LANES), lambda i, j: (i, 0, 0, 0))],
        out_specs=pl.BlockSpec((1, s, LANES), lambda i, j: (i, 0, j)),
        out_shape=jax.ShapeDtypeStruct((b, s, DIL_W), BF16),
        scratch_shapes=[pltpu.VMEM((s, LANES), F32)] * 3 + [pltpu.VMEM((n_g, s, LANES), F32)] * 3,
        compiler_params=_cparams(2),
        name="dilated_attention",
    )(hproj, hproj, hproj, tab_dil)


def _out_ln_kernel(*refs, n_in):
    x_ref = refs[0]
    ys = refs[1:1 + n_in]
    ws = refs[1 + n_in:1 + 2 * n_in]
    g_ref, b_ref, o_ref = refs[1 + 2 * n_in:]
    tm = x_ref.shape[1]
    halves = [slice(0, tm // 2), slice(tm // 2, tm)]
    zs = []
    for rows in halves:
        z = DN_ALPHA * x_ref[0, rows, :]
        for y_ref, w_ref in zip(ys, ws):
            z = z + _dot(y_ref[0, rows, :], w_ref[...])
        zs.append(z)
    for rows, z in zip(halves, zs):
        o_ref[0, rows, :] = _layernorm_rows(z, g_ref[...], b_ref[...])


def _out_ln(x3, ys, ws, g, b, tm):
    bsz, s, d = x3.shape
    n_in = len(ys)
    row = lambda w: pl.BlockSpec((1, tm, w), lambda bi, i: (bi, i, 0))
    full = lambda a: pl.BlockSpec(a.shape, lambda bi, i: (0, 0))
    return pl.pallas_call(
        functools.partial(_out_ln_kernel, n_in=n_in),
        grid=(bsz, s // tm),
        in_specs=[row(d)] + [row(y.shape[2]) for y in ys] + [full(w) for w in ws] + [full(g), full(b)],
        out_specs=row(d),
        out_shape=jax.ShapeDtypeStruct((bsz, s, d), F32),
        compiler_params=_cparams(2),
        name="out_proj_ln",
    )(x3, *ys, *ws, g, b)


def _conv_taps(u_ref, row0, rows, cols, taps):
    k_w = len(taps)
    ub = u_ref[row0 - 8:row0 + rows, cols]
    out = ub[8:] * taps[k_w - 1]
    for back in range(1, k_w):
        out = out + pltpu.roll(ub, back, 0)[8:] * taps[k_w - 1 - back]
    return out


def _fill_xb(xb, x_ref, xh_ref, i):
    xb[HALO:, :] = x_ref[0].astype(BF16)
    xb[0:HALO, :] = jnp.where(i > 0, xh_ref[0], 0.0).astype(BF16)


FFN_PIECE = 256


def _ffn_kernel(x_ref, xh_ref, wup_ref, cw_ref, wdn_ref, g_ref, b_ref, o_ref, xb, us, hs, *, tm, rb):
    i = pl.program_id(1)
    _fill_xb(xb, x_ref, xh_ref, i)
    pc = FFN_PIECE
    n_pieces = D_FF // pc

    def up(p):
        x = xb[...]
        us[p % 2, :, 0:pc] = _dot(x, wup_ref[:, pc * p:pc * (p + 1)])
        us[p % 2, :, pc:2 * pc] = _dot(x, wup_ref[:, D_FF + pc * p:D_FF + pc * (p + 1)])

    def gate_piece(p):
        u = us.at[p % 2]
        for r in range(tm // rb):
            row0 = r * rb + HALO
            for h in range(pc // LANES):
                cg = slice(h * LANES, (h + 1) * LANES)
                cv = slice(pc + h * LANES, pc + (h + 1) * LANES)
                wg = slice(pc * p + h * LANES, pc * p + (h + 1) * LANES)
                wv = slice(D_FF + pc * p + h * LANES, D_FF + pc * p + (h + 1) * LANES)
                tg = [cw_ref[j:j + 1, wg] for j in range(FFN_CONV)]
                tv = [cw_ref[j:j + 1, wv] for j in range(FFN_CONV)]
                gate = _conv_taps(u, row0, rb, cg, tg) + cw_ref[FFN_CONV:FFN_CONV + 1, wg]
                val = _conv_taps(u, row0, rb, cv, tv) + cw_ref[FFN_CONV:FFN_CONV + 1, wv]
                hs[r * rb:(r + 1) * rb, wg] = (_silu(gate) * val).astype(BF16)

    up(0)
    for p in range(1, n_pieces):
        up(p)
        gate_piece(p - 1)
    gate_piece(n_pieces - 1)
    halves = [slice(0, tm // 2), slice(tm // 2, tm)]
    zs = [DN_ALPHA * x_ref[0, rows, :] + _dot(hs[rows, :], wdn_ref[...]) for rows in halves]
    for rows, z in zip(halves, zs):
        o_ref[0, rows, :] = _layernorm_rows(z, g_ref[...], b_ref[...])


def _ffn(x3, wup_r, cw_r, wdn_r, g, b, tm):
    bsz, s, d = x3.shape
    hb = tm // HALO
    const = lambda a: pl.BlockSpec(a.shape, lambda bi, i: (0, 0), pipeline_mode=pl.Buffered(1))
    return pl.pallas_call(
        functools.partial(_ffn_kernel, tm=tm, rb=128),
        grid=(bsz, s // tm),
        in_specs=[pl.BlockSpec((1, tm, d), lambda bi, i: (bi, i, 0)),
                  pl.BlockSpec((1, HALO, d), lambda bi, i: (bi, jnp.maximum(i * hb - 1, 0), 0)),
                  const(wup_r), const(cw_r), const(wdn_r), const(g), const(b)],
        out_specs=pl.BlockSpec((1, tm, d), lambda bi, i: (bi, i, 0)),
        out_shape=jax.ShapeDtypeStruct((bsz, s, d), F32),
        scratch_shapes=[pltpu.VMEM((HALO + tm, d), BF16),
                        pltpu.VMEM((2, HALO + tm, 2 * FFN_PIECE), F32),
                        pltpu.VMEM((tm, D_FF), BF16)],
        compiler_params=_cparams(2),
        name="conv_ffn_ln",
    )(x3, x3, wup_r, cw_r, wdn_r, g, b)


def _ffn_weights(w_up, conv_w, conv_b, w_down):
    cw = jnp.concatenate([conv_w, conv_b[None, :], jnp.zeros((8 - FFN_CONV - 1, 2 * D_FF), F32)], 0)
    return w_up.astype(BF16), cw, w_down.astype(BF16)


ODD_PIECE = 512


def _odd_in_kernel(x_ref, xh_ref, w_ref, cw_ref, ws_ref, o_ref, os_ref, xb, us, *, tm, conv_cols, rb):
    i = pl.program_id(1)
    _fill_xb(xb, x_ref, xh_ref, i)
    pc = ODD_PIECE
    n_pieces = w_ref.shape[1] // pc

    def up(p):
        us[p % 2] = _dot(xb[...], w_ref[:, pc * p:pc * (p + 1)])

    def finish(p):
        u = us.at[p % 2]
        if pc * p >= conv_cols:
            o_ref[0, :, pc * p:pc * (p + 1)] = u[HALO:, :].astype(o_ref.dtype)
            return
        for r in range(tm // rb):
            for h in range(pc // LANES):
                cs = slice(h * LANES, (h + 1) * LANES)
                ws = slice(pc * p + h * LANES, pc * p + (h + 1) * LANES)
                taps = [cw_ref[j:j + 1, ws] for j in range(GDN_CONV)]
                y = _conv_taps(u, r * rb + HALO, rb, cs, taps)
                y = _silu(y)
                if pc * p + h * LANES < 2 * GDN_W:
                    y = y * lax.rsqrt(jnp.sum(y * y, -1, keepdims=True) + 1e-6)
                    if pc * p + h * LANES < GDN_W:
                        y = y * (GDN_DK ** -0.5)
                o_ref[0, r * rb:(r + 1) * rb, ws] = y.astype(o_ref.dtype)

    up(0)
    os_ref[0] = _dot(xb[HALO:, :], ws_ref[...])
    for p in range(1, n_pieces):
        up(p)
        finish(p - 1)
    finish(n_pieces - 1)


def _odd_in(x3, w_main, cw, w_small, tm):
    bsz, s, d = x3.shape
    n = w_main.shape[1]
    hb = tm // HALO
    const = lambda a: pl.BlockSpec(a.shape, lambda bi, i: (0, 0), pipeline_mode=pl.Buffered(1))
    return pl.pallas_call(
        functools.partial(_odd_in_kernel, tm=tm, conv_cols=cw.shape[1], rb=128),
        grid=(bsz, s // tm),
        in_specs=[pl.BlockSpec((1, tm, d), lambda bi, i: (bi, i, 0)),
                  pl.BlockSpec((1, HALO, d), lambda bi, i: (bi, jnp.maximum(i * hb - 1, 0), 0)),
                  const(w_main), const(cw), const(w_small)],
        out_specs=[pl.BlockSpec((1, tm, n), lambda bi, i: (bi, i, 0)),
                   pl.BlockSpec((1, tm, LANES), lambda bi, i: (bi, i, 0))],
        out_shape=[jax.ShapeDtypeStruct((bsz, s, n), BF16),
                   jax.ShapeDtypeStruct((bsz, s, LANES), F32)],
        scratch_shapes=[pltpu.VMEM((HALO + tm, d), BF16),
                        pltpu.VMEM((2, HALO + tm, ODD_PIECE), F32)],
        compiler_params=_cparams(2),
        name="od_in_proj_conv",
    )(x3, x3, w_main, cw, w_small)


GDN_WIDE = 4
GDN_BLK = GDN_CHUNK * GDN_WIDE
GDN_HG = 8
GDN_LOCKSTEP = 2


def _split_bf16(a):
    hi = a.astype(BF16)
    return hi, (a - hi.astype(F32)).astype(BF16)


def _gdn_kernel(prm_ref, q_ref, k_ref, v_ref, gt_ref, sm_ref, nw_ref, o_ref,
                mq_s, nn_s, op_s, eg_s, st_s, rhs_s, kd_s, qd_s, at_s, *, seq):
    hg = pl.program_id(1)
    c, wd, blk = GDN_CHUNK, GDN_WIDE, GDN_BLK
    ww = c * wd
    n_blocks = seq // blk
    shift = int(math.log2(c))

    r_w = lax.broadcasted_iota(jnp.int32, (c, ww), 0)
    l_w = lax.broadcasted_iota(jnp.int32, (c, ww), 1)
    j_w = l_w & (c - 1)
    chunk_of_lane = l_w >> shift
    tri_w = r_w >= j_w
    strict_w = r_w > j_w
    upper_w = jnp.where(r_w <= j_w, 1.0, 0.0).astype(F32)
    bmask = (lax.broadcasted_iota(jnp.int32, (ww, ww), 0) >> shift) == (lax.broadcasted_iota(jnp.int32, (ww, ww), 1) >> shift)
    ltri = jnp.where(lax.broadcasted_iota(jnp.int32, (c, c), 0) >= lax.broadcasted_iota(jnp.int32, (c, c), 1),
                     1.0, 0.0).astype(BF16)
    ones_cc = jnp.ones((c, c), BF16)
    lane_blk = lax.broadcasted_iota(jnp.int32, (blk, LANES), 1)
    first_half = lax.broadcasted_iota(jnp.int32, (c, LANES), 1) < c
    a_neg = -jnp.exp(prm_ref[0:1, :])
    dtb = prm_ref[1:2, :]

    def bdiag(xw):
        return jnp.where(bmask, jnp.concatenate([xw] * wd, 0), 0.0).astype(BF16)

    hs_all = list(range(GDN_HG))

    def widen(col):
        out = jnp.broadcast_to(col[0:c], (c, ww))
        for ch in range(1, wd):
            out = jnp.where(chunk_of_lane == ch, jnp.broadcast_to(col[ch * c:(ch + 1) * c], (c, ww)), out)
        return out

    def prep_steps(i, slot, hs_):
        rows = pl.ds(pl.multiple_of(i * blk, blk), blk)
        sm = sm_ref[0, rows, :]
        beta_all = _sigmoid(sm)
        sp_in = sm + dtb
        g_all = a_neg * (jnp.maximum(sp_in, 0.0) + jnp.log1p(jnp.exp(-jnp.abs(sp_in))))
        cols = {hh: slice(hh * GDN_DK, (hh + 1) * GDN_DK) for hh in hs_}
        head = {hh: hg * GDN_HG + hh for hh in hs_}
        qn = {hh: q_ref[0, rows, cols[hh]].astype(F32) for hh in hs_}
        kn = {hh: k_ref[0, rows, cols[hh]].astype(F32) for hh in hs_}
        beta = {hh: jnp.sum(jnp.where(lane_blk == head[hh], beta_all, 0.0), -1, keepdims=True) for hh in hs_}
        g = {hh: jnp.sum(jnp.where(lane_blk == GDN_HEADS + head[hh], g_all, 0.0), -1, keepdims=True) for hh in hs_}
        kb = {hh: kn[hh] * beta[hh] for hh in hs_}
        g_w = {hh: widen(g[hh]) for hh in hs_}
        g_sp = {hh: _split_bf16(g_w[hh]) for hh in hs_}
        gu_sp = {hh: _split_bf16(g_w[hh] * upper_w) for hh in hs_}
        gc_col = {hh: _dot(ltri, g_sp[hh][0]) + _dot(ltri, g_sp[hh][1]) for hh in hs_}
        gc_row = {hh: _dot(ones_cc, gu_sp[hh][0]) + _dot(ones_cc, gu_sp[hh][1]) for hh in hs_}
        yield
        grams = {hh: [_dot_nt(jnp.concatenate([kb[hh][2 * p * c:(2 * p + 2) * c], qn[hh][2 * p * c:(2 * p + 2) * c]], 0)
                              .astype(BF16), kn[hh][2 * p * c:(2 * p + 2) * c].astype(BF16))
                      for p in range(wd // 2)] for hh in hs_}
        yield
        pws = {}
        for hh in hs_:
            decay = jnp.where(tri_w, jnp.exp(jnp.where(tri_w, gc_col[hh] - gc_row[hh], 0.0)), 0.0)
            a_w = jnp.concatenate([jnp.where(first_half, gm[0:c], gm[c:2 * c]) for gm in grams[hh]], 1)
            qk_w = jnp.concatenate([jnp.where(first_half, gm[2 * c:3 * c], gm[3 * c:4 * c]) for gm in grams[hh]], 1)
            pws[hh] = -jnp.where(strict_w, a_w * decay, 0.0)
            at_s[hh] = jnp.where(tri_w, qk_w * decay, 0.0)
        ews = dict(pws)
        pws = {hh: _dot(pws[hh].astype(BF16), bdiag(pws[hh])) for hh in hs_}
        for hh in hs_:
            gc_parts, gl_parts = [], []
            for p in range(wd // 2):
                x = gc_col[hh][:, 2 * p * c:(2 * p + 2) * c]
                xr = pltpu.roll(x, c, 1)
                for part in (jnp.where(first_half, x, xr), jnp.where(first_half, xr, x)):
                    gc_parts.append(part)
                    gl_parts.append(jnp.broadcast_to(part[c - 1:c, :], (c, LANES)))
            gc_t = jnp.concatenate(gc_parts, 0)
            gl_t = jnp.concatenate(gl_parts, 0)
            egc = jnp.exp(gc_t)
            rhs_s[hh, :, 0:GDN_DK] = kb[hh] * egc
            rhs_s[hh, :, GDN_DK:] = v_ref[0, rows, cols[hh]].astype(F32) * beta[hh]
            kd_s[hh] = (kn[hh] * jnp.exp(gl_t - gc_t)).astype(BF16)
            qd_s[hh] = qn[hh] * egc
            for ch in range(wd):
                eg_s[slot, hh, ch * 8:(ch + 1) * 8, :] = jnp.broadcast_to(
                    jnp.exp(gl_t[ch * c:ch * c + 1, :]), (8, LANES))
        yield
        for _ in range(shift - 1):
            both = {hh: _dot(jnp.concatenate([ews[hh], pws[hh]], 0).astype(BF16), bdiag(pws[hh])) for hh in hs_}
            ews = {hh: ews[hh] + pws[hh] + both[hh][0:c] for hh in hs_}
            pws = {hh: both[hh][c:2 * c] for hh in hs_}
            yield
        rhss = {hh: rhs_s[hh] for hh in hs_}
        solb = {hh: (rhss[hh] + _dot(bdiag(ews[hh]), rhss[hh].astype(BF16))).astype(BF16) for hh in hs_}
        yield
        asol = {hh: _dot(bdiag(at_s[hh]), solb[hh]) for hh in hs_}
        for hh in hs_:
            op_s[slot, hh] = asol[hh][:, GDN_DK:]
            qp = (qd_s[hh] - asol[hh][:, :GDN_DK]).astype(BF16)
            for ch in range(wd):
                mq_s[slot, hh, ch, GDN_DK:GDN_DK + c, :] = qp[ch * c:(ch + 1) * c]
        yield
        for ch in range(wd):
            cr = slice(ch * c, (ch + 1) * c)
            mns = {hh: _dot_tn(kd_s[hh, cr, :], solb[hh][cr]) for hh in hs_}
            for hh in hs_:
                mq_s[slot, hh, ch, 0:GDN_DK, :] = mns[hh][:, :GDN_DK].astype(BF16)
                nn_s[slot, hh, ch] = mns[hh][:, GDN_DK:]
            if ch % 2 == 1:
                yield

    def scan_steps(i, slot):
        for ch in range(wd):
            rows = pl.ds(pl.multiple_of(i * blk + ch * c, c), c)
            for hh in hs_all:
                cols = slice(hh * GDN_DK, (hh + 1) * GDN_DK)
                state = st_s[hh]
                x = _dot(mq_s[slot, hh, ch], state.astype(BF16))
                st_s[hh] = state * eg_s[slot, hh, ch * 8:ch * 8 + 1, :] - x[:GDN_DK] + nn_s[slot, hh, ch]
                o = x[GDN_DK:] + op_s[slot, hh, ch * c:(ch + 1) * c, :]
                y = o * lax.rsqrt(jnp.mean(o * o, -1, keepdims=True) + EPS) * nw_ref[...]
                gt = gt_ref[0, rows, cols].astype(F32)
                o_ref[0, rows, cols] = (y * _silu(gt)).astype(o_ref.dtype)
            yield

    def interleave(gens):
        gens = list(gens)
        while gens:
            for gen in list(gens):
                try:
                    next(gen)
                except StopIteration:
                    gens.remove(gen)

    def stage(prep_i, prep_slot, scan_i, scan_slot):
        gens = []
        if prep_i is not None:
            gens += [prep_steps(prep_i, prep_slot, hs_all[g0:g0 + GDN_LOCKSTEP])
                     for g0 in range(0, GDN_HG, GDN_LOCKSTEP)]
        if scan_i is not None:
            gens.append(scan_steps(scan_i, scan_slot))
        interleave(gens)

    assert n_blocks >= 2 and n_blocks % 2 == 0
    st_s[...] = jnp.zeros_like(st_s)
    stage(0, 0, None, None)

    def body(j, carry):
        stage(2 * j + 1, 1, 2 * j, 0)
        stage(2 * j + 2, 0, 2 * j + 1, 1)
        return carry

    lax.fori_loop(0, (n_blocks - 2) // 2, body, 0)
    stage(n_blocks - 1, 1, n_blocks - 2, 0)
    stage(None, None, n_blocks - 1, 1)


def _gdn(qkvg, small, a_log, dt_bias, norm_w):
    b, s, _ = qkvg.shape
    gw = GDN_HG * GDN_DK
    groups = GDN_HEADS // GDN_HG
    blk = lambda off: pl.BlockSpec((1, s, gw), lambda i, j: (i, 0, off + j), pipeline_mode=pl.Buffered(1))
    prm = jnp.zeros((8, LANES), F32)
    prm = prm.at[0, GDN_HEADS:2 * GDN_HEADS].set(a_log).at[1, GDN_HEADS:2 * GDN_HEADS].set(dt_bias)
    return pl.pallas_call(
        functools.partial(_gdn_kernel, seq=s),
        grid=(b, groups),
        in_specs=[pl.BlockSpec((8, LANES), lambda i, j: (0, 0)),
                  blk(0), blk(groups), blk(2 * groups), blk(3 * groups),
                  pl.BlockSpec((1, s, LANES), lambda i, j: (i, 0, 0)),
                  pl.BlockSpec((1, GDN_DV), lambda i, j: (0, 0))],
        out_specs=pl.BlockSpec((1, s, gw), lambda i, j: (i, 0, j)),
        out_shape=jax.ShapeDtypeStruct((b, s, GDN_W), BF16),
        scratch_shapes=[pltpu.VMEM((2, GDN_HG, GDN_WIDE, GDN_DK + GDN_CHUNK, GDN_DK), BF16),
                        pltpu.VMEM((2, GDN_HG, GDN_WIDE, GDN_DK, GDN_DV), F32),
                        pltpu.VMEM((2, GDN_HG, GDN_BLK, GDN_DV), F32),
                        pltpu.VMEM((2, GDN_HG, GDN_WIDE * 8, LANES), F32),
                        pltpu.VMEM((GDN_HG, GDN_DK, GDN_DV), F32),
                        pltpu.VMEM((GDN_HG, GDN_BLK, GDN_DK + GDN_DV), F32),
                        pltpu.VMEM((GDN_HG, GDN_BLK, GDN_DK), BF16),
                        pltpu.VMEM((GDN_HG, GDN_BLK, GDN_DK), F32),
                        pltpu.VMEM((GDN_HG, GDN_CHUNK, GDN_BLK), F32)],
        compiler_params=_cparams(2),
        name="gated_delta_rule",
    )(prm, qkvg, qkvg, qkvg, qkvg, small, norm_w.reshape(1, GDN_DV))


def kernel(x, positions, ev_w_in, ev_w_out, od_w_in, od_conv_w, od_a_log, od_dt_bias, od_norm_w, od_w_out,
           ffn_w_up, ffn_conv_w, ffn_conv_b, ffn_w_down, ln1_g, ln1_b, ln2_g, ln2_b):
    b, s, d = x.shape
    pos3 = positions.reshape(b, s, 1)
    ret_consts = _ret_consts()
    row = lambda a: a.reshape(1, d)

    for layer in range(DEPTH):
        j = layer // 2
        if layer == 0:
            hproj, tab_ret, tab_dil = _proj(x, ev_w_in[j].astype(BF16), ROW_TILE, 512, pos3, _rope_freqs())
        elif layer % 2 == 0:
            hproj = _proj(x, ev_w_in[j].astype(BF16), ROW_TILE, 512)
        if layer % 2 == 0:
            ya = _retention(hproj, tab_ret, ret_consts)
            yb = _dilated(hproj, tab_dil)
            w_out = ev_w_out[j].astype(BF16)
            x = _out_ln(x, [ya, yb], [w_out[:RET_V_W], w_out[RET_V_W:]], row(ln1_g[layer]), row(ln1_b[layer]),
                        ROW_TILE)
        else:
            w_in = od_w_in[j]
            w_main = w_in[:, :4 * GDN_W].astype(BF16)
            w_small = jnp.pad(w_in[:, 4 * GDN_W:], ((0, 0), (0, LANES - 2 * GDN_HEADS))).astype(BF16)
            cw = jnp.pad(od_conv_w[j], ((0, 8 - GDN_CONV), (0, 0)))
            qkvg, small = _odd_in(x, w_main, cw, w_small, ROW_TILE)
            yc = _gdn(qkvg, small, od_a_log[j], od_dt_bias[j], od_norm_w[j])
            x = _out_ln(x, [yc], [od_w_out[j].astype(BF16)], row(ln1_g[layer]), row(ln1_b[layer]), ROW_TILE)
        wup_r, cw_r, wdn_r = _ffn_weights(ffn_w_up[layer], ffn_conv_w[layer], ffn_conv_b[layer], ffn_w_down[layer])
        x = _ffn(x, wup_r, cw_r, wdn_r, row(ln2_g[layer]), row(ln2_b[layer]), ROW_TILE)
    return x
```

```python
import functools
import math

import jax
import jax.numpy as jnp
import numpy as np
from jax import lax
from jax.experimental import pallas as pl
from jax.experimental.pallas import tpu as pltpu

F32 = jnp.float32
BF16 = jnp.bfloat16

D_MODEL = 1024
DEPTH = 4
RET_HEADS, RET_DK, RET_DV, RET_CHUNK, RET_THETA = 4, 128, 256, 128, 10000.0
DIL_HEADS, DIL_HD, DIL_BLOCK = 8, 64, 128
DIL_DILATIONS = (1, 4, 16)
DIL_INTERLEAVE = 2
ROPE_THETA, ROPE_DIMS = 500000.0, DIL_HD // 4
GDN_HEADS, GDN_DK, GDN_DV, GDN_CHUNK, GDN_CONV = 8, 128, 128, 64, 4
D_FF, FFN_CONV = 2816, 3
DN_ALPHA = (2.0 * DEPTH) ** 0.25
EPS = 1e-5

RET_QK_W = RET_HEADS * RET_DK
RET_V_W = RET_HEADS * RET_DV
DIL_W = DIL_HEADS * DIL_HD
EV_IN = 2 * RET_QK_W + 2 * RET_V_W + 3 * DIL_W
GDN_W = GDN_HEADS * GDN_DK

LANES = 128
HIST = 8
ROW_TILE = 1024
VMEM_LIMIT = 56 * 1024 * 1024


def _cparams(n_axes, vmem=VMEM_LIMIT):
    return pltpu.CompilerParams(dimension_semantics=("arbitrary",) * n_axes, vmem_limit_bytes=vmem)


def _dot(a, b):
    return jnp.dot(a, b, preferred_element_type=F32)


def _dot_nt(a, b):
    return lax.dot_general(a, b, (((1,), (1,)), ((), ())), preferred_element_type=F32)


def _dot_tn(a, b):
    return lax.dot_general(a, b, (((0,), (0,)), ((), ())), preferred_element_type=F32)


def _sigmoid(x):
    return 1.0 / (1.0 + jnp.exp(-x))


def _silu(x):
    return x * _sigmoid(x)


def _layernorm_rows(z, g, b):
    mu = jnp.mean(z, -1, keepdims=True)
    zc = z - mu
    var = jnp.mean(zc * zc, -1, keepdims=True)
    return zc * lax.rsqrt(var + EPS) * g + b


TRIG_ROWS = 256


def _proj_kernel(x_ref, w_ref, *rest, tn, tables):
    if tables:
        pos_ref, f_ref, o_ref, tr_ref, td_ref = rest
        tm = x_ref.shape[1]
        units = [(dst, r) for r in range(tm // TRIG_ROWS) for dst in (tr_ref, td_ref)]
    else:
        (o_ref,) = rest
        units = []
    n_pieces = w_ref.shape[1] // tn
    per_piece = -(-len(units) // n_pieces)
    xb = x_ref[0].astype(BF16)
    for j in range(n_pieces):
        cols = slice(j * tn, (j + 1) * tn)
        o_ref[0, :, cols] = _dot(xb, w_ref[:, cols]).astype(o_ref.dtype)
        for dst, r in units[j * per_piece:(j + 1) * per_piece]:
            rows = slice(r * TRIG_ROWS, (r + 1) * TRIG_ROWS)
            f0 = 0 if dst is tr_ref else 2
            ang = pos_ref[0, rows, :].astype(F32) * f_ref[f0:f0 + 1, :]
            dst[0, 0, rows, :] = jnp.cos(ang)
            dst[0, 1, rows, :] = jnp.sin(ang) * f_ref[f0 + 1:f0 + 2, :]


def _proj(x3, w_bf16, tm, tn, pos3=None, freqs=None):
    bsz, s, k = x3.shape
    n = w_bf16.shape[1]
    tables = pos3 is not None
    in_specs = [pl.BlockSpec((1, tm, k), lambda bi, i: (bi, i, 0)),
                pl.BlockSpec((k, n), lambda bi, i: (0, 0), pipeline_mode=pl.Buffered(1))]
    out_specs = [pl.BlockSpec((1, tm, n), lambda bi, i: (bi, i, 0))]
    out_shape = [jax.ShapeDtypeStruct((bsz, s, n), BF16)]
    args = [x3, w_bf16]
    if tables:
        in_specs += [pl.BlockSpec((1, tm, 1), lambda bi, i: (bi, i, 0)),
                     pl.BlockSpec((8, LANES), lambda bi, i: (0, 0))]
        tab_spec = pl.BlockSpec((1, 2, tm, LANES), lambda bi, i: (bi, 0, i, 0))
        out_specs += [tab_spec, tab_spec]
        out_shape += [jax.ShapeDtypeStruct((bsz, 2, s, LANES), F32)] * 2
        args += [pos3, freqs]
    outs = pl.pallas_call(
        functools.partial(_proj_kernel, tn=tn, tables=tables),
        grid=(bsz, s // tm),
        in_specs=in_specs,
        out_specs=out_specs,
        out_shape=out_shape,
        compiler_params=_cparams(2),
        name="ev_in_proj",
    )(*args)
    return outs if tables else outs[0]


def _rope_freqs():
    half = RET_DK // 2
    inv = jnp.power(RET_THETA, -jnp.arange(half, dtype=F32) * 2.0 / RET_DK)
    one = jnp.ones((half,), F32)
    ret = jnp.stack([jnp.concatenate([inv, inv]), jnp.concatenate([-one, one])])
    hh = ROPE_DIMS // 2
    invd = jnp.power(ROPE_THETA, -jnp.arange(hh, dtype=F32) * 2.0 / ROPE_DIMS)
    z = jnp.zeros((DIL_HD - ROPE_DIMS,), F32)
    oneh = jnp.ones((hh,), F32)
    f_ang = jnp.concatenate([invd, invd, z])
    f_sgn = jnp.concatenate([-oneh, oneh, z])
    dil = jnp.stack([jnp.tile(f_ang, 2), jnp.tile(f_sgn, 2)])
    return jnp.concatenate([ret, dil, jnp.zeros((4, LANES), F32)], 0)


def _ret_kernel(gch_ref, q_ref, k_ref, v_ref, g_ref, tab_ref, dm_ref, zt_ref, xi_ref, o_ref, r_ref, *, n_chunks):
    r_ref[...] = jnp.zeros_like(r_ref)
    c = RET_CHUNK
    heads = range(RET_HEADS)

    def body(ci, carry):
        rows = pl.ds(pl.multiple_of(ci * c, c), c)
        cosr = tab_ref[0, 0, rows, :]
        sinr = tab_ref[0, 1, rows, :]
        qk = lambda h: slice(h * RET_DK, (h + 1) * RET_DK)
        vv = lambda h: slice(h * RET_DV, (h + 1) * RET_DV)
        qs = [q_ref[0, rows, qk(h)].astype(F32) for h in heads]
        ks = [k_ref[0, rows, qk(h)].astype(F32) for h in heads]
        qr = [q * cosr + pltpu.roll(q, RET_DK // 2, 1) * sinr for q in qs]
        kr = [(k * cosr + pltpu.roll(k, RET_DK // 2, 1) * sinr) * (RET_DK ** -0.5) for k in ks]
        vs = [v_ref[0, rows, vv(h)] for h in heads]
        r_prev = [r_ref[h] for h in heads]
        scores = [_dot_nt(qr[h].astype(BF16), kr[h].astype(BF16)) * dm_ref[h] for h in heads]
        inter = [_dot((qr[h] * xi_ref[h]).astype(BF16), r_prev[h].astype(BF16)) for h in heads]
        kv = [_dot_tn((kr[h] * zt_ref[h]).astype(BF16), vs[h]) for h in heads]
        os_ = [_dot(scores[h].astype(BF16), vs[h]) + inter[h] for h in heads]
        for h in heads:
            r_ref[h] = r_prev[h] * gch_ref[h] + kv[h]
            o = os_[h]
            y = o * lax.rsqrt(jnp.mean(o * o, -1, keepdims=True) + EPS)
            g = g_ref[0, rows, vv(h)].astype(F32)
            o_ref[0, rows, vv(h)] = (y * _silu(g)).astype(o_ref.dtype)
        return carry

    lax.fori_loop(0, n_chunks, body, 0, unroll=2)


def _ret_consts():
    hh = np.arange(RET_HEADS, dtype=np.float64)
    lg = np.log1p(-np.power(2.0, -5.0 - hh))
    idx = np.arange(RET_CHUNK, dtype=np.float64)
    rel = idx[:, None] - idx[None, :]
    dmat = np.where(rel >= 0, np.exp(np.maximum(rel, 0.0) * lg[:, None, None]), 0.0)
    zeta = np.exp((RET_CHUNK - 1 - idx) * lg[:, None])
    xi = np.exp((idx + 1) * lg[:, None])
    gch = np.exp(RET_CHUNK * lg)
    bc = lambda a: np.broadcast_to(a[:, :, None], (RET_HEADS, RET_CHUNK, RET_DK))
    return (jnp.asarray(gch, F32), jnp.asarray(dmat, F32), jnp.asarray(bc(zeta), F32), jnp.asarray(bc(xi), F32))


def _retention(hproj, tab_ret, consts):
    b, s, _ = hproj.shape
    gch, dmat, zeta, xi = consts
    v_off = 2 * RET_QK_W // RET_V_W
    all_heads = pl.BlockSpec((RET_HEADS, RET_CHUNK, RET_DK), lambda i: (0, 0, 0))
    return pl.pallas_call(
        functools.partial(_ret_kernel, n_chunks=s // RET_CHUNK),
        grid=(b,),
        in_specs=[pl.BlockSpec(memory_space=pltpu.SMEM),
                  pl.BlockSpec((1, s, RET_QK_W), lambda i: (i, 0, 0)),
                  pl.BlockSpec((1, s, RET_QK_W), lambda i: (i, 0, 1)),
                  pl.BlockSpec((1, s, RET_V_W), lambda i: (i, 0, v_off)),
                  pl.BlockSpec((1, s, RET_V_W), lambda i: (i, 0, v_off + 1)),
                  pl.BlockSpec((1, 2, s, LANES), lambda i: (i, 0, 0, 0)),
                  all_heads, all_heads, all_heads],
        out_specs=pl.BlockSpec((1, s, RET_V_W), lambda i: (i, 0, 0)),
        out_shape=jax.ShapeDtypeStruct((b, s, RET_V_W), BF16),
        scratch_shapes=[pltpu.VMEM((RET_HEADS, RET_DK, RET_DV), F32)],
        compiler_params=_cparams(1),
        name="retention",
    )(gch, hproj, hproj, hproj, hproj, tab_ret, dmat, zeta, xi)


def _dil_kernel(q_ref, k_ref, v_ref, tab_ref, o_ref, qs, ks, vs, acc, ms, ls, *, seq):
    rc = 256
    lane = lax.broadcasted_iota(jnp.int32, (DIL_BLOCK, LANES), 1)
    first_head = lane < DIL_HD

    def rot_body(i, carry):
        rows = pl.ds(pl.multiple_of(i * rc, rc), rc)
        cosd = tab_ref[0, 0, rows, :]
        sind = tab_ref[0, 1, rows, :]
        hh = ROPE_DIMS // 2
        low_half = (lax.broadcasted_iota(jnp.int32, (rc, LANES), 1) & (DIL_HD - 1)) < hh

        def rot(x):
            return x * cosd + jnp.where(low_half, pltpu.roll(x, LANES - hh, 1), pltpu.roll(x, hh, 1)) * sind

        qs[rows, :] = rot(q_ref[0, rows, :].astype(F32)) * (DIL_HD ** -0.5)
        ks[rows, :] = rot(k_ref[0, rows, :].astype(F32))
        vs[rows, :] = v_ref[0, rows, :].astype(F32)
        return carry

    lax.fori_loop(0, seq // rc, rot_body, 0)

    def blocks(g, d, kb_rows, specs):
        def sl(start, n):
            return pl.ds(start, n) if d == 1 else pl.ds(start, n, stride=d)

        qsl = [sl(q0, DIL_BLOCK) for q0, _, _ in specs]
        ksl = [sl(k0, kb_rows) for _, k0, _ in specs]
        qbs = [qs[s_, :] for s_ in qsl]
        kbs = [ks[s_, :].astype(BF16) for s_ in ksl]
        vbs = [vs[s_, :].astype(BF16) for s_ in ksl]
        nb_ = range(len(specs))
        q2 = [jnp.concatenate([jnp.where(first_head, qbs[b_], 0.0), jnp.where(first_head, 0.0, qbs[b_])], 0)
              .astype(BF16) for b_ in nb_]
        sc = [_dot_nt(q2[b_], kbs[b_]) + jnp.concatenate([specs[b_][2]] * 2, 0) for b_ in nb_]
        yield
        mx = [jnp.max(s_, -1, keepdims=True) for s_ in sc]
        pr = [jnp.exp(s_ - m_).astype(BF16) for s_, m_ in zip(sc, mx)]
        yield
        ones_v = jnp.ones((kb_rows, LANES), BF16)
        pv = [_dot(pr[b_], jnp.concatenate([vbs[b_], ones_v], 1)) for b_ in nb_]
        yield
        h0, h1 = slice(0, DIL_BLOCK), slice(DIL_BLOCK, 2 * DIL_BLOCK)
        for b_ in nb_:
            acc[g, qsl[b_], :] = jnp.where(first_head, pv[b_][h0, :LANES], pv[b_][h1, :LANES])
            ms[g, qsl[b_], :] = jnp.where(first_head, mx[b_][h0], mx[b_][h1])
            ls[g, qsl[b_], :] = jnp.where(first_head, pv[b_][h0, LANES:], pv[b_][h1, LANES:])

    def window_bias(kb_rows, off):
        dist = (lax.broadcasted_iota(jnp.int32, (DIL_BLOCK, kb_rows), 0)
                - lax.broadcasted_iota(jnp.int32, (DIL_BLOCK, kb_rows), 1)) + off
        return jnp.where((dist >= 0) & (dist <= DIL_BLOCK), 0.0, -jnp.inf).astype(F32)

    bias_self = window_bias(DIL_BLOCK, 0)
    bias_first = window_bias(2 * DIL_BLOCK, 0)
    bias_band = window_bias(2 * DIL_BLOCK, DIL_BLOCK)

    nbi = DIL_INTERLEAVE
    groups = []
    for g, d in enumerate(DIL_DILATIONS):
        nb = seq // d // DIL_BLOCK
        span = DIL_BLOCK * d
        if nb == 1:
            specs = [(r, r, bias_self) for r in range(d)]
            kb_rows = DIL_BLOCK
        else:
            specs = [(r + n * span, r + max(n - 1, 0) * span, bias_band if n else bias_first)
                     for n in range(nb) for r in range(d)]
            kb_rows = 2 * DIL_BLOCK
        groups += [blocks(g, d, kb_rows, specs[i:i + nbi]) for i in range(0, len(specs), nbi)]

    pending, active = iter(groups), []
    while True:
        nxt = next(pending, None)
        if nxt is not None:
            active.append(nxt)
        if not active:
            break
        for gen in list(active):
            try:
                next(gen)
            except StopIteration:
                active.remove(gen)

    def comb_body(i, carry):
        rows = pl.ds(pl.multiple_of(i * rc, rc), rc)
        m0, m1, m2 = ms[0, rows, :], ms[1, rows, :], ms[2, rows, :]
        mmax = jnp.maximum(jnp.maximum(m0, m1), m2)
        e0, e1, e2 = jnp.exp(m0 - mmax), jnp.exp(m1 - mmax), jnp.exp(m2 - mmax)
        num = e0 * acc[0, rows, :] + e1 * acc[1, rows, :] + e2 * acc[2, rows, :]
        den = e0 * ls[0, rows, :] + e1 * ls[1, rows, :] + e2 * ls[2, rows, :]
        o_ref[0, rows, :] = (num / den).astype(o_ref.dtype)
        return carry

    lax.fori_loop(0, seq // rc, comb_body, 0)


def _dilated(hproj, tab_dil):
    b, s, _ = hproj.shape
    base = (2 * RET_QK_W + 2 * RET_V_W) // LANES
    pairs = DIL_W // LANES
    blk = lambda off: pl.BlockSpec((1, s, LANES), lambda i, j: (i, 0, base + off + j))
    n_g = len(DIL_DILATIONS)
    return pl.pallas_call(
        functools.partial(_dil_kernel, seq=s),
        grid=(b, pairs),
        in_specs=[blk(0), blk(pairs), blk(2 * pairs),
                  pl.BlockSpec((1, 2, s, LANES), lambda i, j: (i, 0, 0, 0))],
        out_specs=pl.BlockSpec((1, s, LANES), lambda i, j: (i, 0, j)),
        out_shape=jax.ShapeDtypeStruct((b, s, DIL_W), BF16),
        scratch_shapes=[pltpu.VMEM((s, LANES), F32)] * 3 + [pltpu.VMEM((n_g, s, LANES), F32)] * 3,
        compiler_params=_cparams(2),
        name="dilated_attention",
    )(hproj, hproj, hproj, tab_dil)


def _out_ln_kernel(*refs, n_in):
    x_ref = refs[0]
    ys = refs[1:1 + n_in]
    ws = refs[1 + n_in:1 + 2 * n_in]
    g_ref, b_ref, o_ref = refs[1 + 2 * n_in:]
    tm = x_ref.shape[1]
    halves = [slice(0, tm // 2), slice(tm // 2, tm)]
    zs = []
    for rows in halves:
        z = DN_ALPHA * x_ref[0, rows, :]
        for y_ref, w_ref in zip(ys, ws):
            z = z + _dot(y_ref[0, rows, :], w_ref[...])
        zs.append(z)
    for rows, z in zip(halves, zs):
        o_ref[0, rows, :] = _layernorm_rows(z, g_ref[...], b_ref[...])


def _out_ln(x3, ys, ws, g, b, tm):
    bsz, s, d = x3.shape
    n_in = len(ys)
    row = lambda w: pl.BlockSpec((1, tm, w), lambda bi, i: (bi, i, 0))
    full = lambda a: pl.BlockSpec(a.shape, lambda bi, i: (0, 0))
    return pl.pallas_call(
        functools.partial(_out_ln_kernel, n_in=n_in),
        grid=(bsz, s // tm),
        in_specs=[row(d)] + [row(y.shape[2]) for y in ys] + [full(w) for w in ws] + [full(g), full(b)],
        out_specs=row(d),
        out_shape=jax.ShapeDtypeStruct((bsz, s, d), F32),
        compiler_params=_cparams(2),
        name="out_proj_ln",
    )(x3, *ys, *ws, g, b)


def _conv_taps(u_ref, row0, rows, cols, taps):
    k_w = len(taps)
    ub = u_ref[row0 - 8:row0 + rows, cols]
    out = ub[8:] * taps[k_w - 1]
    for back in range(1, k_w):
        out = out + pltpu.roll(ub, back, 0)[8:] * taps[k_w - 1 - back]
    return out


def _project_with_history(xb, w_ref, cols_w, u_ref, cols_u, hist, tm):
    u_ref[0:HIST, cols_u] = hist[:, cols_w]
    u = _dot(xb, w_ref[:, cols_w])
    u_ref[HIST:, cols_u] = u
    hist[:, cols_w] = u[tm - HIST:, :]


FFN_PIECE = 256


def _ffn_kernel(x_ref, wup_ref, cw_ref, wdn_ref, g_ref, b_ref, o_ref, us, hs, hist, *, tm, rb):
    @pl.when(pl.program_id(1) == 0)
    def _():
        hist[...] = jnp.zeros_like(hist)

    xb = x_ref[0].astype(BF16)
    pc = FFN_PIECE
    n_pieces = D_FF // pc

    def up(p):
        u = us.at[p % 2]
        _project_with_history(xb, wup_ref, slice(pc * p, pc * (p + 1)), u, slice(0, pc), hist, tm)
        _project_with_history(xb, wup_ref, slice(D_FF + pc * p, D_FF + pc * (p + 1)), u, slice(pc, 2 * pc), hist, tm)

    def gate_piece(p):
        u = us.at[p % 2]
        for r in range(tm // rb):
            row0 = r * rb + HIST
            for h in range(pc // LANES):
                cg = slice(h * LANES, (h + 1) * LANES)
                cv = slice(pc + h * LANES, pc + (h + 1) * LANES)
                wg = slice(pc * p + h * LANES, pc * p + (h + 1) * LANES)
                wv = slice(D_FF + pc * p + h * LANES, D_FF + pc * p + (h + 1) * LANES)
                tg = [cw_ref[j:j + 1, wg] for j in range(FFN_CONV)]
                tv = [cw_ref[j:j + 1, wv] for j in range(FFN_CONV)]
                gate = _conv_taps(u, row0, rb, cg, tg) + cw_ref[FFN_CONV:FFN_CONV + 1, wg]
                val = _conv_taps(u, row0, rb, cv, tv) + cw_ref[FFN_CONV:FFN_CONV + 1, wv]
                hs[r * rb:(r + 1) * rb, wg] = (_silu(gate) * val).astype(BF16)

    up(0)
    for p in range(1, n_pieces):
        up(p)
        gate_piece(p - 1)
    gate_piece(n_pieces - 1)
    halves = [slice(0, tm // 2), slice(tm // 2, tm)]
    zs = [DN_ALPHA * x_ref[0, rows, :] + _dot(hs[rows, :], wdn_ref[...]) for rows in halves]
    for rows, z in zip(halves, zs):
        o_ref[0, rows, :] = _layernorm_rows(z, g_ref[...], b_ref[...])


def _ffn(x3, wup_r, cw_r, wdn_r, g, b, tm):
    bsz, s, d = x3.shape
    const = lambda a: pl.BlockSpec(a.shape, lambda bi, i: (0, 0), pipeline_mode=pl.Buffered(1))
    return pl.pallas_call(
        functools.partial(_ffn_kernel, tm=tm, rb=128),
        grid=(bsz, s // tm),
        in_specs=[pl.BlockSpec((1, tm, d), lambda bi, i: (bi, i, 0)),
                  const(wup_r), const(cw_r), const(wdn_r), const(g), const(b)],
        out_specs=pl.BlockSpec((1, tm, d), lambda bi, i: (bi, i, 0)),
        out_shape=jax.ShapeDtypeStruct((bsz, s, d), F32),
        scratch_shapes=[pltpu.VMEM((2, HIST + tm, 2 * FFN_PIECE), F32),
                        pltpu.VMEM((tm, D_FF), BF16),
                        pltpu.VMEM((HIST, 2 * D_FF), F32)],
        compiler_params=_cparams(2),
        name="conv_ffn_ln",
    )(x3, wup_r, cw_r, wdn_r, g, b)


def _ffn_weights(w_up, conv_w, conv_b, w_down):
    cw = jnp.concatenate([conv_w, conv_b[None, :], jnp.zeros((8 - FFN_CONV - 1, 2 * D_FF), F32)], 0)
    return w_up.astype(BF16), cw, w_down.astype(BF16)


ODD_PIECE = 512


def _odd_in_kernel(x_ref, w_ref, cw_ref, ws_ref, o_ref, os_ref, us, hist, *, tm, conv_cols, rb):
    @pl.when(pl.program_id(1) == 0)
    def _():
        hist[...] = jnp.zeros_like(hist)

    xb = x_ref[0].astype(BF16)
    pc = ODD_PIECE
    n_pieces = w_ref.shape[1] // pc

    def up(p):
        cols = slice(pc * p, pc * (p + 1))
        if pc * p >= conv_cols:
            us[p % 2, HIST:, :] = _dot(xb, w_ref[:, cols])
        else:
            _project_with_history(xb, w_ref, cols, us.at[p % 2], slice(0, pc), hist, tm)

    def finish(p):
        u = us.at[p % 2]
        if pc * p >= conv_cols:
            o_ref[0, :, pc * p:pc * (p + 1)] = u[HIST:, :].astype(o_ref.dtype)
            return
        for r in range(tm // rb):
            for h in range(pc // LANES):
                cs = slice(h * LANES, (h + 1) * LANES)
                ws = slice(pc * p + h * LANES, pc * p + (h + 1) * LANES)
                taps = [cw_ref[j:j + 1, ws] for j in range(GDN_CONV)]
                y = _conv_taps(u, r * rb + HIST, rb, cs, taps)
                y = _silu(y)
                if pc * p + h * LANES < 2 * GDN_W:
                    y = y * lax.rsqrt(jnp.sum(y * y, -1, keepdims=True) + 1e-6)
                    if pc * p + h * LANES < GDN_W:
                        y = y * (GDN_DK ** -0.5)
                o_ref[0, r * rb:(r + 1) * rb, ws] = y.astype(o_ref.dtype)

    up(0)
    os_ref[0] = _dot(xb, ws_ref[...])
    for p in range(1, n_pieces):
        up(p)
        finish(p - 1)
    finish(n_pieces - 1)


def _odd_in(x3, w_main, cw, w_small, tm):
    bsz, s, d = x3.shape
    n = w_main.shape[1]
    const = lambda a: pl.BlockSpec(a.shape, lambda bi, i: (0, 0), pipeline_mode=pl.Buffered(1))
    return pl.pallas_call(
        functools.partial(_odd_in_kernel, tm=tm, conv_cols=cw.shape[1], rb=128),
        grid=(bsz, s // tm),
        in_specs=[pl.BlockSpec((1, tm, d), lambda bi, i: (bi, i, 0)),
                  const(w_main), const(cw), const(w_small)],
        out_specs=[pl.BlockSpec((1, tm, n), lambda bi, i: (bi, i, 0)),
                   pl.BlockSpec((1, tm, LANES), lambda bi, i: (bi, i, 0))],
        out_shape=[jax.ShapeDtypeStruct((bsz, s, n), BF16),
                   jax.ShapeDtypeStruct((bsz, s, LANES), F32)],
        scratch_shapes=[pltpu.VMEM((2, HIST + tm, ODD_PIECE), F32),
                        pltpu.VMEM((HIST, cw.shape[1]), F32)],
        compiler_params=_cparams(2),
        name="od_in_proj_conv",
    )(x3, w_main, cw, w_small)


GDN_WIDE = 4
GDN_BLK = GDN_CHUNK * GDN_WIDE
GDN_HG = 8
GDN_LOCKSTEP = 2


def _split_bf16(a):
    hi = a.astype(BF16)
    return hi, (a - hi.astype(F32)).astype(BF16)


def _gdn_kernel(prm_ref, q_ref, k_ref, v_ref, gt_ref, sm_ref, nw_ref, o_ref,
                mq_s, nn_s, op_s, eg_s, st_s, rhs_s, kd_s, qd_s, at_s, *, seq):
    hg = pl.program_id(1)
    c, wd, blk = GDN_CHUNK, GDN_WIDE, GDN_BLK
    ww = c * wd
    n_blocks = seq // blk
    shift = int(math.log2(c))

    r_w = lax.broadcasted_iota(jnp.int32, (c, ww), 0)
    l_w = lax.broadcasted_iota(jnp.int32, (c, ww), 1)
    j_w = l_w & (c - 1)
    chunk_of_lane = l_w >> shift
    tri_w = r_w >= j_w
    strict_w = r_w > j_w
    upper_w = jnp.where(r_w <= j_w, 1.0, 0.0).astype(F32)
    bmask = (lax.broadcasted_iota(jnp.int32, (ww, ww), 0) >> shift) == (lax.broadcasted_iota(jnp.int32, (ww, ww), 1) >> shift)
    ltri = jnp.where(lax.broadcasted_iota(jnp.int32, (c, c), 0) >= lax.broadcasted_iota(jnp.int32, (c, c), 1),
                     1.0, 0.0).astype(BF16)
    ones_cc = jnp.ones((c, c), BF16)
    lane_blk = lax.broadcasted_iota(jnp.int32, (blk, LANES), 1)
    first_half = lax.broadcasted_iota(jnp.int32, (c, LANES), 1) < c
    a_neg = -jnp.exp(prm_ref[0:1, :])
    dtb = prm_ref[1:2, :]

    def bdiag(xw):
        return jnp.where(bmask, jnp.concatenate([xw] * wd, 0), 0.0).astype(BF16)

    hs_all = list(range(GDN_HG))

    def widen(col):
        out = jnp.broadcast_to(col[0:c], (c, ww))
        for ch in range(1, wd):
            out = jnp.where(chunk_of_lane == ch, jnp.broadcast_to(col[ch * c:(ch + 1) * c], (c, ww)), out)
        return out

    def prep_steps(i, slot, hs_):
        rows = pl.ds(pl.multiple_of(i * blk, blk), blk)
        sm = sm_ref[0, rows, :]
        beta_all = _sigmoid(sm)
        sp_in = sm + dtb
        g_all = a_neg * (jnp.maximum(sp_in, 0.0) + jnp.log1p(jnp.exp(-jnp.abs(sp_in))))
        cols = {hh: slice(hh * GDN_DK, (hh + 1) * GDN_DK) for hh in hs_}
        head = {hh: hg * GDN_HG + hh for hh in hs_}
        qn = {hh: q_ref[0, rows, cols[hh]].astype(F32) for hh in hs_}
        kn = {hh: k_ref[0, rows, cols[hh]].astype(F32) for hh in hs_}
        beta = {hh: jnp.sum(jnp.where(lane_blk == head[hh], beta_all, 0.0), -1, keepdims=True) for hh in hs_}
        g = {hh: jnp.sum(jnp.where(lane_blk == GDN_HEADS + head[hh], g_all, 0.0), -1, keepdims=True) for hh in hs_}
        kb = {hh: kn[hh] * beta[hh] for hh in hs_}
        g_w = {hh: widen(g[hh]) for hh in hs_}
        g_sp = {hh: _split_bf16(g_w[hh]) for hh in hs_}
        gu_sp = {hh: _split_bf16(g_w[hh] * upper_w) for hh in hs_}
        gc_col = {hh: _dot(ltri, g_sp[hh][0]) + _dot(ltri, g_sp[hh][1]) for hh in hs_}
        gc_row = {hh: _dot(ones_cc, gu_sp[hh][0]) + _dot(ones_cc, gu_sp[hh][1]) for hh in hs_}
        yield
        grams = {hh: [_dot_nt(jnp.concatenate([kb[hh][2 * p * c:(2 * p + 2) * c], qn[hh][2 * p * c:(2 * p + 2) * c]], 0)
                              .astype(BF16), kn[hh][2 * p * c:(2 * p + 2) * c].astype(BF16))
                      for p in range(wd // 2)] for hh in hs_}
        yield
        pws = {}
        for hh in hs_:
            decay = jnp.where(tri_w, jnp.exp(jnp.where(tri_w, gc_col[hh] - gc_row[hh], 0.0)), 0.0)
            a_w = jnp.concatenate([jnp.where(first_half, gm[0:c], gm[c:2 * c]) for gm in grams[hh]], 1)
            qk_w = jnp.concatenate([jnp.where(first_half, gm[2 * c:3 * c], gm[3 * c:4 * c]) for gm in grams[hh]], 1)
            pws[hh] = -jnp.where(strict_w, a_w * decay, 0.0)
            at_s[hh] = jnp.where(tri_w, qk_w * decay, 0.0)
        ews = dict(pws)
        pws = {hh: _dot(pws[hh].astype(BF16), bdiag(pws[hh])) for hh in hs_}
        for hh in hs_:
            gc_parts, gl_parts = [], []
            for p in range(wd // 2):
                x = gc_col[hh][:, 2 * p * c:(2 * p + 2) * c]
                xr = pltpu.roll(x, c, 1)
                for part in (jnp.where(first_half, x, xr), jnp.where(first_half, xr, x)):
                    gc_parts.append(part)
                    gl_parts.append(jnp.broadcast_to(part[c - 1:c, :], (c, LANES)))
            gc_t = jnp.concatenate(gc_parts, 0)
            gl_t = jnp.concatenate(gl_parts, 0)
            egc = jnp.exp(gc_t)
            rhs_s[hh, :, 0:GDN_DK] = kb[hh] * egc
            rhs_s[hh, :, GDN_DK:] = v_ref[0, rows, cols[hh]].astype(F32) * beta[hh]
            kd_s[hh] = (kn[hh] * jnp.exp(gl_t - gc_t)).astype(BF16)
            qd_s[hh] = qn[hh] * egc
            for ch in range(wd):
                eg_s[slot, hh, ch * 8:(ch + 1) * 8, :] = jnp.broadcast_to(
                    jnp.exp(gl_t[ch * c:ch * c + 1, :]), (8, LANES))
        yield
        for _ in range(shift - 1):
            both = {hh: _dot(jnp.concatenate([ews[hh], pws[hh]], 0).astype(BF16), bdiag(pws[hh])) for hh in hs_}
            ews = {hh: ews[hh] + pws[hh] + both[hh][0:c] for hh in hs_}
            pws = {hh: both[hh][c:2 * c] for hh in hs_}
            yield
        rhss = {hh: rhs_s[hh] for hh in hs_}
        solb = {hh: (rhss[hh] + _dot(bdiag(ews[hh]), rhss[hh].astype(BF16))).astype(BF16) for hh in hs_}
        yield
        asol = {hh: _dot(bdiag(at_s[hh]), solb[hh]) for hh in hs_}
        for hh in hs_:
            op_s[slot, hh] = asol[hh][:, GDN_DK:]
            qp = (qd_s[hh] - asol[hh][:, :GDN_DK]).astype(BF16)
            for ch in range(wd):
                mq_s[slot, hh, ch, GDN_DK:GDN_DK + c, :] = qp[ch * c:(ch + 1) * c]
        yield
        for ch in range(wd):
            cr = slice(ch * c, (ch + 1) * c)
            mns = {hh: _dot_tn(kd_s[hh, cr, :], solb[hh][cr]) for hh in hs_}
            for hh in hs_:
                mq_s[slot, hh, ch, 0:GDN_DK, :] = mns[hh][:, :GDN_DK].astype(BF16)
                nn_s[slot, hh, ch] = mns[hh][:, GDN_DK:]
            if ch % 2 == 1:
                yield

    def scan_steps(i, slot):
        for ch in range(wd):
            rows = pl.ds(pl.multiple_of(i * blk + ch * c, c), c)
            for hh in hs_all:
                cols = slice(hh * GDN_DK, (hh + 1) * GDN_DK)
                state = st_s[hh]
                x = _dot(mq_s[slot, hh, ch], state.astype(BF16))
                st_s[hh] = state * eg_s[slot, hh, ch * 8:ch * 8 + 1, :] - x[:GDN_DK] + nn_s[slot, hh, ch]
                o = x[GDN_DK:] + op_s[slot, hh, ch * c:(ch + 1) * c, :]
                y = o * lax.rsqrt(jnp.mean(o * o, -1, keepdims=True) + EPS) * nw_ref[...]
                gt = gt_ref[0, rows, cols].astype(F32)
                o_ref[0, rows, cols] = (y * _silu(gt)).astype(o_ref.dtype)
            yield

    def interleave(gens):
        gens = list(gens)
        while gens:
            for gen in list(gens):
                try:
                    next(gen)
                except StopIteration:
                    gens.remove(gen)

    def stage(prep_i, prep_slot, scan_i, scan_slot):
        gens = []
        if prep_i is not None:
            gens += [prep_steps(prep_i, prep_slot, hs_all[g0:g0 + GDN_LOCKSTEP])
                     for g0 in range(0, GDN_HG, GDN_LOCKSTEP)]
        if scan_i is not None:
            gens.append(scan_steps(scan_i, scan_slot))
        interleave(gens)

    assert n_blocks >= 2 and n_blocks % 2 == 0
    st_s[...] = jnp.zeros_like(st_s)
    stage(0, 0, None, None)

    def body(j, carry):
        stage(2 * j + 1, 1, 2 * j, 0)
        stage(2 * j + 2, 0, 2 * j + 1, 1)
        return carry

    lax.fori_loop(0, (n_blocks - 2) // 2, body, 0)
    stage(n_blocks - 1, 1, n_blocks - 2, 0)
    stage(None, None, n_blocks - 1, 1)


def _gdn(qkvg, small, a_log, dt_bias, norm_w):
    b, s, _ = qkvg.shape
    gw = GDN_HG * GDN_DK
    groups = GDN_HEADS // GDN_HG
    blk = lambda off: pl.BlockSpec((1, s, gw), lambda i, j: (i, 0, off + j), pipeline_mode=pl.Buffered(1))
    prm = jnp.zeros((8, LANES), F32)
    prm = prm.at[0, GDN_HEADS:2 * GDN_HEADS].set(a_log).at[1, GDN_HEADS:2 * GDN_HEADS].set(dt_bias)
    return pl.pallas_call(
        functools.partial(_gdn_kernel, seq=s),
        grid=(b, groups),
        in_specs=[pl.BlockSpec((8, LANES), lambda i, j: (0, 0)),
                  blk(0), blk(groups), blk(2 * groups), blk(3 * groups),
                  pl.BlockSpec((1, s, LANES), lambda i, j: (i, 0, 0)),
                  pl.BlockSpec((1, GDN_DV), lambda i, j: (0, 0))],
        out_specs=pl.BlockSpec((1, s, gw), lambda i, j: (i, 0, j)),
        out_shape=jax.ShapeDtypeStruct((b, s, GDN_W), BF16),
        scratch_shapes=[pltpu.VMEM((2, GDN_HG, GDN_WIDE, GDN_DK + GDN_CHUNK, GDN_DK), BF16),
                        pltpu.VMEM((2, GDN_HG, GDN_WIDE, GDN_DK, GDN_DV), F32),
                        pltpu.VMEM((2, GDN_HG, GDN_BLK, GDN_DV), F32),
                        pltpu.VMEM((2, GDN_HG, GDN_WIDE * 8, LANES), F32),
                        pltpu.VMEM((GDN_HG, GDN_DK, GDN_DV), F32),
                        pltpu.VMEM((GDN_HG, GDN_BLK, GDN_DK + GDN_DV), F32),
                        pltpu.VMEM((GDN_HG, GDN_BLK, GDN_DK), BF16),
                        pltpu.VMEM((GDN_HG, GDN_BLK, GDN_DK), F32),
                        pltpu.VMEM((GDN_HG, GDN_CHUNK, GDN_BLK), F32)],
        compiler_params=_cparams(2),
        name="gated_delta_rule",
    )(prm, qkvg, qkvg, qkvg, qkvg, small, norm_w.reshape(1, GDN_DV))


def kernel(x, positions, ev_w_in, ev_w_out, od_w_in, od_conv_w, od_a_log, od_dt_bias, od_norm_w, od_w_out,
           ffn_w_up, ffn_conv_w, ffn_conv_b, ffn_w_down, ln1_g, ln1_b, ln2_g, ln2_b):
    b, s, d = x.shape
    pos3 = positions.reshape(b, s, 1)
    ret_consts = _ret_consts()
    row = lambda a: a.reshape(1, d)

    for layer in range(DEPTH):
        j = layer // 2
        if layer == 0:
            hproj, tab_ret, tab_dil = _proj(x, ev_w_in[j].astype(BF16), ROW_TILE, 512, pos3, _rope_freqs())
        elif layer % 2 == 0:
            hproj = _proj(x, ev_w_in[j].astype(BF16), ROW_TILE, 512)
        if layer % 2 == 0:
            ya = _retention(hproj, tab_ret, ret_consts)
            yb = _dilated(hproj, tab_dil)
            w_out = ev_w_out[j].astype(BF16)
            x = _out_ln(x, [ya, yb], [w_out[:RET_V_W], w_out[RET_V_W:]], row(ln1_g[layer]), row(ln1_b[layer]),
                        ROW_TILE)
        else:
            w_in = od_w_in[j]
            w_main = w_in[:, :4 * GDN_W].astype(BF16)
            w_small = jnp.pad(w_in[:, 4 * GDN_W:], ((0, 0), (0, LANES - 2 * GDN_HEADS))).astype(BF16)
            cw = jnp.pad(od_conv_w[j], ((0, 8 - GDN_CONV), (0, 0)))
            qkvg, small = _odd_in(x, w_main, cw, w_small, ROW_TILE)
            yc = _gdn(qkvg, small, od_a_log[j], od_dt_bias[j], od_norm_w[j])
            x = _out_ln(x, [yc], [od_w_out[j].astype(BF16)], row(ln1_g[layer]), row(ln1_b[layer]), ROW_TILE)
        wup_r, cw_r, wdn_r = _ffn_weights(ffn_w_up[layer], ffn_conv_w[layer], ffn_conv_b[layer], ffn_w_down[layer])
        x = _ffn(x, wup_r, cw_r, wdn_r, row(ln2_g[layer]), row(ln2_b[layer]), ROW_TILE)
    return x
```

```python
import functools
import math

import jax
import jax.numpy as jnp
import numpy as np
from jax import lax
from jax.experimental import pallas as pl
from jax.experimental.pallas import tpu as pltpu

F32 = jnp.float32
BF16 = jnp.bfloat16

D_MODEL = 1024
DEPTH = 4
RET_HEADS, RET_DK, RET_DV, RET_CHUNK, RET_THETA = 4, 128, 256, 128, 10000.0
DIL_HEADS, DIL_HD, DIL_BLOCK = 8, 64, 128
DIL_DILATIONS = (1, 4, 16)
DIL_INTERLEAVE = 2
ROPE_THETA, ROPE_DIMS = 500000.0, DIL_HD // 4
GDN_HEADS, GDN_DK, GDN_DV, GDN_CHUNK, GDN_CONV = 8, 128, 128, 64, 4
D_FF, FFN_CONV = 2816, 3
DN_ALPHA = (2.0 * DEPTH) ** 0.25
EPS = 1e-5

RET_QK_W = RET_HEADS * RET_DK
RET_V_W = RET_HEADS * RET_DV
DIL_W = DIL_HEADS * DIL_HD
EV_IN = 2 * RET_QK_W + 2 * RET_V_W + 3 * DIL_W
GDN_W = GDN_HEADS * GDN_DK

LANES = 128
HALO = 16
HIST = 8
ROW_TILE = 1024
VMEM_LIMIT = 56 * 1024 * 1024


def _cparams(n_axes, vmem=VMEM_LIMIT):
    return pltpu.CompilerParams(dimension_semantics=("arbitrary",) * n_axes, vmem_limit_bytes=vmem)


def _dot(a, b):
    return jnp.dot(a, b, preferred_element_type=F32)


def _dot_nt(a, b):
    return lax.dot_general(a, b, (((1,), (1,)), ((), ())), preferred_element_type=F32)


def _dot_tn(a, b):
    return lax.dot_general(a, b, (((0,), (0,)), ((), ())), preferred_element_type=F32)


def _sigmoid(x):
    return 1.0 / (1.0 + jnp.exp(-x))


def _silu(x):
    return x * _sigmoid(x)


def _layernorm_rows(z, g, b):
    mu = jnp.mean(z, -1, keepdims=True)
    zc = z - mu
    var = jnp.mean(zc * zc, -1, keepdims=True)
    return zc * lax.rsqrt(var + EPS) * g + b


TRIG_ROWS = 256


def _proj_kernel(x_ref, w_ref, *rest, tn, tables):
    if tables:
        pos_ref, f_ref, o_ref, tr_ref, td_ref = rest
        tm = x_ref.shape[1]
        units = [(dst, r) for r in range(tm // TRIG_ROWS) for dst in (tr_ref, td_ref)]
    else:
        (o_ref,) = rest
        units = []
    n_pieces = w_ref.shape[1] // tn
    per_piece = -(-len(units) // n_pieces)
    xb = x_ref[0].astype(BF16)
    for j in range(n_pieces):
        cols = slice(j * tn, (j + 1) * tn)
        o_ref[0, :, cols] = _dot(xb, w_ref[:, cols]).astype(o_ref.dtype)
        for dst, r in units[j * per_piece:(j + 1) * per_piece]:
            rows = slice(r * TRIG_ROWS, (r + 1) * TRIG_ROWS)
            f0 = 0 if dst is tr_ref else 2
            ang = pos_ref[0, rows, :].astype(F32) * f_ref[f0:f0 + 1, :]
            dst[0, 0, rows, :] = jnp.cos(ang)
            dst[0, 1, rows, :] = jnp.sin(ang) * f_ref[f0 + 1:f0 + 2, :]


def _proj(x3, w_bf16, tm, tn, pos3=None, freqs=None):
    bsz, s, k = x3.shape
    n = w_bf16.shape[1]
    tables = pos3 is not None
    in_specs = [pl.BlockSpec((1, tm, k), lambda bi, i: (bi, i, 0)),
                pl.BlockSpec((k, n), lambda bi, i: (0, 0), pipeline_mode=pl.Buffered(1))]
    out_specs = [pl.BlockSpec((1, tm, n), lambda bi, i: (bi, i, 0))]
    out_shape = [jax.ShapeDtypeStruct((bsz, s, n), BF16)]
    args = [x3, w_bf16]
    if tables:
        in_specs += [pl.BlockSpec((1, tm, 1), lambda bi, i: (bi, i, 0)),
                     pl.BlockSpec((8, LANES), lambda bi, i: (0, 0))]
        tab_spec = pl.BlockSpec((1, 2, tm, LANES), lambda bi, i: (bi, 0, i, 0))
        out_specs += [tab_spec, tab_spec]
        out_shape += [jax.ShapeDtypeStruct((bsz, 2, s, LANES), F32)] * 2
        args += [pos3, freqs]
    outs = pl.pallas_call(
        functools.partial(_proj_kernel, tn=tn, tables=tables),
        grid=(bsz, s // tm),
        in_specs=in_specs,
        out_specs=out_specs,
        out_shape=out_shape,
        compiler_params=_cparams(2),
        name="ev_in_proj",
    )(*args)
    return outs if tables else outs[0]


def _rope_freqs():
    half = RET_DK // 2
    inv = jnp.power(RET_THETA, -jnp.arange(half, dtype=F32) * 2.0 / RET_DK)
    one = jnp.ones((half,), F32)
    ret = jnp.stack([jnp.concatenate([inv, inv]), jnp.concatenate([-one, one])])
    hh = ROPE_DIMS // 2
    invd = jnp.power(ROPE_THETA, -jnp.arange(hh, dtype=F32) * 2.0 / ROPE_DIMS)
    z = jnp.zeros((DIL_HD - ROPE_DIMS,), F32)
    oneh = jnp.ones((hh,), F32)
    f_ang = jnp.concatenate([invd, invd, z])
    f_sgn = jnp.concatenate([-oneh, oneh, z])
    dil = jnp.stack([jnp.tile(f_ang, 2), jnp.tile(f_sgn, 2)])
    return jnp.concatenate([ret, dil, jnp.zeros((4, LANES), F32)], 0)


def _ret_kernel(gch_ref, q_ref, k_ref, v_ref, g_ref, tab_ref, dm_ref, zt_ref, xi_ref, o_ref, r_ref, *, n_chunks):
    r_ref[...] = jnp.zeros_like(r_ref)
    c = RET_CHUNK
    heads = range(RET_HEADS)

    def body(ci, carry):
        rows = pl.ds(pl.multiple_of(ci * c, c), c)
        cosr = tab_ref[0, 0, rows, :]
        sinr = tab_ref[0, 1, rows, :]
        qk = lambda h: slice(h * RET_DK, (h + 1) * RET_DK)
        vv = lambda h: slice(h * RET_DV, (h + 1) * RET_DV)
        qs = [q_ref[0, rows, qk(h)].astype(F32) for h in heads]
        ks = [k_ref[0, rows, qk(h)].astype(F32) for h in heads]
        qr = [q * cosr + pltpu.roll(q, RET_DK // 2, 1) * sinr for q in qs]
        kr = [(k * cosr + pltpu.roll(k, RET_DK // 2, 1) * sinr) * (RET_DK ** -0.5) for k in ks]
        vs = [v_ref[0, rows, vv(h)] for h in heads]
        r_prev = [r_ref[h] for h in heads]
        scores = [_dot_nt(qr[h].astype(BF16), kr[h].astype(BF16)) * dm_ref[h] for h in heads]
        inter = [_dot((qr[h] * xi_ref[h]).astype(BF16), r_prev[h].astype(BF16)) for h in heads]
        kv = [_dot_tn((kr[h] * zt_ref[h]).astype(BF16), vs[h]) for h in heads]
        os_ = [_dot(scores[h].astype(BF16), vs[h]) + inter[h] for h in heads]
        for h in heads:
            r_ref[h] = r_prev[h] * gch_ref[h] + kv[h]
            o = os_[h]
            y = o * lax.rsqrt(jnp.mean(o * o, -1, keepdims=True) + EPS)
            g = g_ref[0, rows, vv(h)].astype(F32)
            o_ref[0, rows, vv(h)] = (y * _silu(g)).astype(o_ref.dtype)
        return carry

    lax.fori_loop(0, n_chunks, body, 0, unroll=2)


def _ret_consts():
    hh = np.arange(RET_HEADS, dtype=np.float64)
    lg = np.log1p(-np.power(2.0, -5.0 - hh))
    idx = np.arange(RET_CHUNK, dtype=np.float64)
    rel = idx[:, None] - idx[None, :]
    dmat = np.where(rel >= 0, np.exp(np.maximum(rel, 0.0) * lg[:, None, None]), 0.0)
    zeta = np.exp((RET_CHUNK - 1 - idx) * lg[:, None])
    xi = np.exp((idx + 1) * lg[:, None])
    gch = np.exp(RET_CHUNK * lg)
    bc = lambda a: np.broadcast_to(a[:, :, None], (RET_HEADS, RET_CHUNK, RET_DK))
    return (jnp.asarray(gch, F32), jnp.asarray(dmat, F32), jnp.asarray(bc(zeta), F32), jnp.asarray(bc(xi), F32))


def _retention(hproj, tab_ret, consts):
    b, s, _ = hproj.shape
    gch, dmat, zeta, xi = consts
    v_off = 2 * RET_QK_W // RET_V_W
    all_heads = pl.BlockSpec((RET_HEADS, RET_CHUNK, RET_DK), lambda i: (0, 0, 0))
    return pl.pallas_call(
        functools.partial(_ret_kernel, n_chunks=s // RET_CHUNK),
        grid=(b,),
        in_specs=[pl.BlockSpec(memory_space=pltpu.SMEM),
                  pl.BlockSpec((1, s, RET_QK_W), lambda i: (i, 0, 0)),
                  pl.BlockSpec((1, s, RET_QK_W), lambda i: (i, 0, 1)),
                  pl.BlockSpec((1, s, RET_V_W), lambda i: (i, 0, v_off)),
                  pl.BlockSpec((1, s, RET_V_W), lambda i: (i, 0, v_off + 1)),
                  pl.BlockSpec((1, 2, s, LANES), lambda i: (i, 0, 0, 0)),
                  all_heads, all_heads, all_heads],
        out_specs=pl.BlockSpec((1, s, RET_V_W), lambda i: (i, 0, 0)),
        out_shape=jax.ShapeDtypeStruct((b, s, RET_V_W), BF16),
        scratch_shapes=[pltpu.VMEM((RET_HEADS, RET_DK, RET_DV), F32)],
        compiler_params=_cparams(1),
        name="retention",
    )(gch, hproj, hproj, hproj, hproj, tab_ret, dmat, zeta, xi)


def _dil_kernel(q_ref, k_ref, v_ref, tab_ref, o_ref, qs, ks, vs, acc, ms, ls, *, seq):
    rc = 256
    lane = lax.broadcasted_iota(jnp.int32, (DIL_BLOCK, LANES), 1)
    first_head = lane < DIL_HD

    def rot_body(i, carry):
        rows = pl.ds(pl.multiple_of(i * rc, rc), rc)
        cosd = tab_ref[0, 0, rows, :]
        sind = tab_ref[0, 1, rows, :]
        hh = ROPE_DIMS // 2
        low_half = (lax.broadcasted_iota(jnp.int32, (rc, LANES), 1) & (DIL_HD - 1)) < hh

        def rot(x):
            return x * cosd + jnp.where(low_half, pltpu.roll(x, LANES - hh, 1), pltpu.roll(x, hh, 1)) * sind

        qs[rows, :] = rot(q_ref[0, rows, :].astype(F32)) * (DIL_HD ** -0.5)
        ks[rows, :] = rot(k_ref[0, rows, :].astype(F32))
        vs[rows, :] = v_ref[0, rows, :].astype(F32)
        return carry

    lax.fori_loop(0, seq // rc, rot_body, 0)

    def blocks(g, d, kb_rows, specs):
        def sl(start, n):
            return pl.ds(start, n) if d == 1 else pl.ds(start, n, stride=d)

        qsl = [sl(q0, DIL_BLOCK) for q0, _, _ in specs]
        ksl = [sl(k0, kb_rows) for _, k0, _ in specs]
        qbs = [qs[s_, :] for s_ in qsl]
        kbs = [ks[s_, :].astype(BF16) for s_ in ksl]
        vbs = [vs[s_, :].astype(BF16) for s_ in ksl]
        nb_ = range(len(specs))
        q2 = [jnp.concatenate([jnp.where(first_head, qbs[b_], 0.0), jnp.where(first_head, 0.0, qbs[b_])], 0)
              .astype(BF16) for b_ in nb_]
        sc = [_dot_nt(q2[b_], kbs[b_]) + jnp.concatenate([specs[b_][2]] * 2, 0) for b_ in nb_]
        yield
        mx = [jnp.max(s_, -1, keepdims=True) for s_ in sc]
        pr = [jnp.exp(s_ - m_).astype(BF16) for s_, m_ in zip(sc, mx)]
        yield
        ones_v = jnp.ones((kb_rows, LANES), BF16)
        pv = [_dot(pr[b_], jnp.concatenate([vbs[b_], ones_v], 1)) for b_ in nb_]
        yield
        h0, h1 = slice(0, DIL_BLOCK), slice(DIL_BLOCK, 2 * DIL_BLOCK)
        for b_ in nb_:
            acc[g, qsl[b_], :] = jnp.where(first_head, pv[b_][h0, :LANES], pv[b_][h1, :LANES])
            ms[g, qsl[b_], :] = jnp.where(first_head, mx[b_][h0], mx[b_][h1])
            ls[g, qsl[b_], :] = jnp.where(first_head, pv[b_][h0, LANES:], pv[b_][h1, LANES:])

    def window_bias(kb_rows, off):
        dist = (lax.broadcasted_iota(jnp.int32, (DIL_BLOCK, kb_rows), 0)
                - lax.broadcasted_iota(jnp.int32, (DIL_BLOCK, kb_rows), 1)) + off
        return jnp.where((dist >= 0) & (dist <= DIL_BLOCK), 0.0, -jnp.inf).astype(F32)

    bias_self = window_bias(DIL_BLOCK, 0)
    bias_first = window_bias(2 * DIL_BLOCK, 0)
    bias_band = window_bias(2 * DIL_BLOCK, DIL_BLOCK)

    nbi = DIL_INTERLEAVE
    groups = []
    for g, d in enumerate(DIL_DILATIONS):
        nb = seq // d // DIL_BLOCK
        span = DIL_BLOCK * d
        if nb == 1:
            specs = [(r, r, bias_self) for r in range(d)]
            kb_rows = DIL_BLOCK
        else:
            specs = [(r + n * span, r + max(n - 1, 0) * span, bias_band if n else bias_first)
                     for n in range(nb) for r in range(d)]
            kb_rows = 2 * DIL_BLOCK
        groups += [blocks(g, d, kb_rows, specs[i:i + nbi]) for i in range(0, len(specs), nbi)]

    pending, active = iter(groups), []
    while True:
        nxt = next(pending, None)
        if nxt is not None:
            active.append(nxt)
        if not active:
            break
        for gen in list(active):
            try:
                next(gen)
            except StopIteration:
                active.remove(gen)

    def comb_body(i, carry):
        rows = pl.ds(pl.multiple_of(i * rc, rc), rc)
        m0, m1, m2 = ms[0, rows, :], ms[1, rows, :], ms[2, rows, :]
        mmax = jnp.maximum(jnp.maximum(m0, m1), m2)
        e0, e1, e2 = jnp.exp(m0 - mmax), jnp.exp(m1 - mmax), jnp.exp(m2 - mmax)
        num = e0 * acc[0, rows, :] + e1 * acc[1, rows, :] + e2 * acc[2, rows, :]
        den = e0 * ls[0, rows, :] + e1 * ls[1, rows, :] + e2 * ls[2, rows, :]
        o_ref[0, rows, :] = (num / den).astype(o_ref.dtype)
        return carry

    lax.fori_loop(0, seq // rc, comb_body, 0)


def _dilated(hproj, tab_dil):
    b, s, _ = hproj.shape
    base = (2 * RET_QK_W + 2 * RET_V_W) // LANES
    pairs = DIL_W // LANES
    blk = lambda off: pl.BlockSpec((1, s, LANES), lambda i, j: (i, 0, base + off + j))
    n_g = len(DIL_DILATIONS)
    return pl.pallas_call(
        functools.partial(_dil_kernel, seq=s),
        grid=(b, pairs),
        in_specs=[blk(0), blk(pairs), blk(2 * pairs),
                  pl.BlockSpec((1, 2, s, LANES), lambda i, j: (i, 0, 0, 0))],
        out_specs=pl.BlockSpec((1, s, LANES), lambda i, j: (i, 0, j)),
        out_shape=jax.ShapeDtypeStruct((b, s, DIL_W), BF16),
        scratch_shapes=[pltpu.VMEM((s, LANES), F32)] * 3 + [pltpu.VMEM((n_g, s, LANES), F32)] * 3,
        compiler_params=_cparams(2),
        name="dilated_attention",
    )(hproj, hproj, hproj, tab_dil)


def _out_ln_kernel(*refs, n_in):
    x_ref = refs[0]
    ys = refs[1:1 + n_in]
    ws = refs[1 + n_in:1 + 2 * n_in]
    g_ref, b_ref, o_ref = refs[1 + 2 * n_in:]
    tm = x_ref.shape[1]
    halves = [slice(0, tm // 2), slice(tm // 2, tm)]
    zs = []
    for rows in halves:
        z = DN_ALPHA * x_ref[0, rows, :]
        for y_ref, w_ref in zip(ys, ws):
            z = z + _dot(y_ref[0, rows, :], w_ref[...])
        zs.append(z)
    for rows, z in zip(halves, zs):
        o_ref[0, rows, :] = _layernorm_rows(z, g_ref[...], b_ref[...])


def _out_ln(x3, ys, ws, g, b, tm):
    bsz, s, d = x3.shape
    n_in = len(ys)
    row = lambda w: pl.BlockSpec((1, tm, w), lambda bi, i: (bi, i, 0))
    full = lambda a: pl.BlockSpec(a.shape, lambda bi, i: (0, 0))
    return pl.pallas_call(
        functools.partial(_out_ln_kernel, n_in=n_in),
        grid=(bsz, s // tm),
        in_specs=[row(d)] + [row(y.shape[2]) for y in ys] + [full(w) for w in ws] + [full(g), full(b)],
        out_specs=row(d),
        out_shape=jax.ShapeDtypeStruct((bsz, s, d), F32),
        compiler_params=_cparams(2),
        name="out_proj_ln",
    )(x3, *ys, *ws, g, b)


def _conv_taps(u_ref, row0, rows, cols, taps):
    k_w = len(taps)
    ub = u_ref[row0 - 8:row0 + rows, cols]
    out = ub[8:] * taps[k_w - 1]
    for back in range(1, k_w):
        out = out + pltpu.roll(ub, back, 0)[8:] * taps[k_w - 1 - back]
    return out


def _fill_xb(xb, x_ref, xh_ref, i):
    xb[HALO:, :] = x_ref[0].astype(BF16)
    xb[0:HALO, :] = jnp.where(i > 0, xh_ref[0], 0.0).astype(BF16)


FFN_PIECE = 256


def _project_with_history(xb, w_ref, cols_w, u_ref, cols_u, hist, tm):
    u_ref[0:HIST, cols_u] = hist[:, cols_w]
    u = _dot(xb, w_ref[:, cols_w])
    u_ref[HIST:, cols_u] = u
    hist[:, cols_w] = u[tm - HIST:, :]


def _ffn_kernel(x_ref, wup_ref, cw_ref, wdn_ref, g_ref, b_ref, o_ref, us, hs, hist, *, tm, rb):
    @pl.when(pl.program_id(1) == 0)
    def _():
        hist[...] = jnp.zeros_like(hist)

    xb = x_ref[0].astype(BF16)
    pc = FFN_PIECE
    n_pieces = D_FF // pc

    def up(p):
        u = us.at[p % 2]
        _project_with_history(xb, wup_ref, slice(pc * p, pc * (p + 1)), u, slice(0, pc), hist, tm)
        _project_with_history(xb, wup_ref, slice(D_FF + pc * p, D_FF + pc * (p + 1)), u, slice(pc, 2 * pc), hist, tm)

    def gate_piece(p):
        u = us.at[p % 2]
        for r in range(tm // rb):
            row0 = r * rb + HIST
            for h in range(pc // LANES):
                cg = slice(h * LANES, (h + 1) * LANES)
                cv = slice(pc + h * LANES, pc + (h + 1) * LANES)
                wg = slice(pc * p + h * LANES, pc * p + (h + 1) * LANES)
                wv = slice(D_FF + pc * p + h * LANES, D_FF + pc * p + (h + 1) * LANES)
                tg = [cw_ref[j:j + 1, wg] for j in range(FFN_CONV)]
                tv = [cw_ref[j:j + 1, wv] for j in range(FFN_CONV)]
                gate = _conv_taps(u, row0, rb, cg, tg) + cw_ref[FFN_CONV:FFN_CONV + 1, wg]
                val = _conv_taps(u, row0, rb, cv, tv) + cw_ref[FFN_CONV:FFN_CONV + 1, wv]
                hs[r * rb:(r + 1) * rb, wg] = (_silu(gate) * val).astype(BF16)

    up(0)
    for p in range(1, n_pieces):
        up(p)
        gate_piece(p - 1)
    gate_piece(n_pieces - 1)
    halves = [slice(0, tm // 2), slice(tm // 2, tm)]
    zs = [DN_ALPHA * x_ref[0, rows, :] + _dot(hs[rows, :], wdn_ref[...]) for rows in halves]
    for rows, z in zip(halves, zs):
        o_ref[0, rows, :] = _layernorm_rows(z, g_ref[...], b_ref[...])


def _ffn(x3, wup_r, cw_r, wdn_r, g, b, tm):
    bsz, s, d = x3.shape
    const = lambda a: pl.BlockSpec(a.shape, lambda bi, i: (0, 0), pipeline_mode=pl.Buffered(1))
    return pl.pallas_call(
        functools.partial(_ffn_kernel, tm=tm, rb=128),
        grid=(bsz, s // tm),
        in_specs=[pl.BlockSpec((1, tm, d), lambda bi, i: (bi, i, 0)),
                  const(wup_r), const(cw_r), const(wdn_r), const(g), const(b)],
        out_specs=pl.BlockSpec((1, tm, d), lambda bi, i: (bi, i, 0)),
        out_shape=jax.ShapeDtypeStruct((bsz, s, d), F32),
        scratch_shapes=[pltpu.VMEM((2, HIST + tm, 2 * FFN_PIECE), F32),
                        pltpu.VMEM((tm, D_FF), BF16),
                        pltpu.VMEM((HIST, 2 * D_FF), F32)],
        compiler_params=_cparams(2),
        name="conv_ffn_ln",
    )(x3, wup_r, cw_r, wdn_r, g, b)


def _ffn_weights(w_up, conv_w, conv_b, w_down):
    cw = jnp.concatenate([conv_w, conv_b[None, :], jnp.zeros((8 - FFN_CONV - 1, 2 * D_FF), F32)], 0)
    return w_up.astype(BF16), cw, w_down.astype(BF16)


ODD_PIECE = 512


def _odd_in_kernel(x_ref, xh_ref, w_ref, cw_ref, ws_ref, o_ref, os_ref, xb, us, *, tm, conv_cols, rb):
    i = pl.program_id(1)
    _fill_xb(xb, x_ref, xh_ref, i)
    pc = ODD_PIECE
    n_pieces = w_ref.shape[1] // pc

    def up(p):
        us[p % 2] = _dot(xb[...], w_ref[:, pc * p:pc * (p + 1)])

    def finish(p):
        u = us.at[p % 2]
        if pc * p >= conv_cols:
            o_ref[0, :, pc * p:pc * (p + 1)] = u[HALO:, :].astype(o_ref.dtype)
            return
        for r in range(tm // rb):
            for h in range(pc // LANES):
                cs = slice(h * LANES, (h + 1) * LANES)
                ws = slice(pc * p + h * LANES, pc * p + (h + 1) * LANES)
                taps = [cw_ref[j:j + 1, ws] for j in range(GDN_CONV)]
                y = _conv_taps(u, r * rb + HALO, rb, cs, taps)
                y = _silu(y)
                if pc * p + h * LANES < 2 * GDN_W:
                    y = y * lax.rsqrt(jnp.sum(y * y, -1, keepdims=True) + 1e-6)
                    if pc * p + h * LANES < GDN_W:
                        y = y * (GDN_DK ** -0.5)
                o_ref[0, r * rb:(r + 1) * rb, ws] = y.astype(o_ref.dtype)

    up(0)
    os_ref[0] = _dot(xb[HALO:, :], ws_ref[...])
    for p in range(1, n_pieces):
        up(p)
        finish(p - 1)
    finish(n_pieces - 1)


def _odd_in(x3, w_main, cw, w_small, tm):
    bsz, s, d = x3.shape
    n = w_main.shape[1]
    hb = tm // HALO
    const = lambda a: pl.BlockSpec(a.shape, lambda bi, i: (0, 0), pipeline_mode=pl.Buffered(1))
    return pl.pallas_call(
        functools.partial(_odd_in_kernel, tm=tm, conv_cols=cw.shape[1], rb=128),
        grid=(bsz, s // tm),
        in_specs=[pl.BlockSpec((1, tm, d), lambda bi, i: (bi, i, 0)),
                  pl.BlockSpec((1, HALO, d), lambda bi, i: (bi, jnp.maximum(i * hb - 1, 0), 0)),
                  const(w_main), const(cw), const(w_small)],
        out_specs=[pl.BlockSpec((1, tm, n), lambda bi, i: (bi, i, 0)),
                   pl.BlockSpec((1, tm, LANES), lambda bi, i: (bi, i, 0))],
        out_shape=[jax.ShapeDtypeStruct((bsz, s, n), BF16),
                   jax.ShapeDtypeStruct((bsz, s, LANES), F32)],
        scratch_shapes=[pltpu.VMEM((HALO + tm, d), BF16),
                        pltpu.VMEM((2, HALO + tm, ODD_PIECE), F32)],
        compiler_params=_cparams(2),
        name="od_in_proj_conv",
    )(x3, x3, w_main, cw, w_small)


GDN_WIDE = 4
GDN_BLK = GDN_CHUNK * GDN_WIDE
GDN_HG = 8
GDN_LOCKSTEP = 2


def _split_bf16(a):
    hi = a.astype(BF16)
    return hi, (a - hi.astype(F32)).astype(BF16)


def _gdn_kernel(prm_ref, q_ref, k_ref, v_ref, gt_ref, sm_ref, nw_ref, o_ref,
                mq_s, nn_s, op_s, eg_s, st_s, rhs_s, kd_s, qd_s, at_s, *, seq):
    hg = pl.program_id(1)
    c, wd, blk = GDN_CHUNK, GDN_WIDE, GDN_BLK
    ww = c * wd
    n_blocks = seq // blk
    shift = int(math.log2(c))

    r_w = lax.broadcasted_iota(jnp.int32, (c, ww), 0)
    l_w = lax.broadcasted_iota(jnp.int32, (c, ww), 1)
    j_w = l_w & (c - 1)
    chunk_of_lane = l_w >> shift
    tri_w = r_w >= j_w
    strict_w = r_w > j_w
    upper_w = jnp.where(r_w <= j_w, 1.0, 0.0).astype(F32)
    bmask = (lax.broadcasted_iota(jnp.int32, (ww, ww), 0) >> shift) == (lax.broadcasted_iota(jnp.int32, (ww, ww), 1) >> shift)
    ltri = jnp.where(lax.broadcasted_iota(jnp.int32, (c, c), 0) >= lax.broadcasted_iota(jnp.int32, (c, c), 1),
                     1.0, 0.0).astype(BF16)
    ones_cc = jnp.ones((c, c), BF16)
    lane_blk = lax.broadcasted_iota(jnp.int32, (blk, LANES), 1)
    first_half = lax.broadcasted_iota(jnp.int32, (c, LANES), 1) < c
    a_neg = -jnp.exp(prm_ref[0:1, :])
    dtb = prm_ref[1:2, :]

    def bdiag(xw):
        return jnp.where(bmask, jnp.concatenate([xw] * wd, 0), 0.0).astype(BF16)

    hs_all = list(range(GDN_HG))

    def widen(col):
        out = jnp.broadcast_to(col[0:c], (c, ww))
        for ch in range(1, wd):
            out = jnp.where(chunk_of_lane == ch, jnp.broadcast_to(col[ch * c:(ch + 1) * c], (c, ww)), out)
        return out

    def prep_steps(i, slot, hs_):
        rows = pl.ds(pl.multiple_of(i * blk, blk), blk)
        sm = sm_ref[0, rows, :]
        beta_all = _sigmoid(sm)
        sp_in = sm + dtb
        g_all = a_neg * (jnp.maximum(sp_in, 0.0) + jnp.log1p(jnp.exp(-jnp.abs(sp_in))))
        cols = {hh: slice(hh * GDN_DK, (hh + 1) * GDN_DK) for hh in hs_}
        head = {hh: hg * GDN_HG + hh for hh in hs_}
        qn = {hh: q_ref[0, rows, cols[hh]].astype(F32) for hh in hs_}
        kn = {hh: k_ref[0, rows, cols[hh]].astype(F32) for hh in hs_}
        beta = {hh: jnp.sum(jnp.where(lane_blk == head[hh], beta_all, 0.0), -1, keepdims=True) for hh in hs_}
        g = {hh: jnp.sum(jnp.where(lane_blk == GDN_HEADS + head[hh], g_all, 0.0), -1, keepdims=True) for hh in hs_}
        kb = {hh: kn[hh] * beta[hh] for hh in hs_}
        g_w = {hh: widen(g[hh]) for hh in hs_}
        g_sp = {hh: _split_bf16(g_w[hh]) for hh in hs_}
        gu_sp = {hh: _split_bf16(g_w[hh] * upper_w) for hh in hs_}
        gc_col = {hh: _dot(ltri, g_sp[hh][0]) + _dot(ltri, g_sp[hh][1]) for hh in hs_}
        gc_row = {hh: _dot(ones_cc, gu_sp[hh][0]) + _dot(ones_cc, gu_sp[hh][1]) for hh in hs_}
        yield
        grams = {hh: [_dot_nt(jnp.concatenate([kb[hh][2 * p * c:(2 * p + 2) * c], qn[hh][2 * p * c:(2 * p + 2) * c]], 0)
                              .astype(BF16), kn[hh][2 * p * c:(2 * p + 2) * c].astype(BF16))
                      for p in range(wd // 2)] for hh in hs_}
        yield
        pws = {}
        for hh in hs_:
            decay = jnp.where(tri_w, jnp.exp(jnp.where(tri_w, gc_col[hh] - gc_row[hh], 0.0)), 0.0)
            a_w = jnp.concatenate([jnp.where(first_half, gm[0:c], gm[c:2 * c]) for gm in grams[hh]], 1)
            qk_w = jnp.concatenate([jnp.where(first_half, gm[2 * c:3 * c], gm[3 * c:4 * c]) for gm in grams[hh]], 1)
            pws[hh] = -jnp.where(strict_w, a_w * decay, 0.0)
            at_s[hh] = jnp.where(tri_w, qk_w * decay, 0.0)
        ews = dict(pws)
        pws = {hh: _dot(pws[hh].astype(BF16), bdiag(pws[hh])) for hh in hs_}
        for hh in hs_:
            gc_parts, gl_parts = [], []
            for p in range(wd // 2):
                x = gc_col[hh][:, 2 * p * c:(2 * p + 2) * c]
                xr = pltpu.roll(x, c, 1)
                for part in (jnp.where(first_half, x, xr), jnp.where(first_half, xr, x)):
                    gc_parts.append(part)
                    gl_parts.append(jnp.broadcast_to(part[c - 1:c, :], (c, LANES)))
            gc_t = jnp.concatenate(gc_parts, 0)
            gl_t = jnp.concatenate(gl_parts, 0)
            egc = jnp.exp(gc_t)
            rhs_s[hh, :, 0:GDN_DK] = kb[hh] * egc
            rhs_s[hh, :, GDN_DK:] = v_ref[0, rows, cols[hh]].astype(F32) * beta[hh]
            kd_s[hh] = (kn[hh] * jnp.exp(gl_t - gc_t)).astype(BF16)
            qd_s[hh] = qn[hh] * egc
            for ch in range(wd):
                eg_s[slot, hh, ch * 8:(ch + 1) * 8, :] = jnp.broadcast_to(
                    jnp.exp(gl_t[ch * c:ch * c + 1, :]), (8, LANES))
        yield
        for _ in range(shift - 1):
            both = {hh: _dot(jnp.concatenate([ews[hh], pws[hh]], 0).astype(BF16), bdiag(pws[hh])) for hh in hs_}
            ews = {hh: ews[hh] + pws[hh] + both[hh][0:c] for hh in hs_}
            pws = {hh: both[hh][c:2 * c] for hh in hs_}
            yield
        rhss = {hh: rhs_s[hh] for hh in hs_}
        solb = {hh: (rhss[hh] + _dot(bdiag(ews[hh]), rhss[hh].astype(BF16))).astype(BF16) for hh in hs_}
        yield
        asol = {hh: _dot(bdiag(at_s[hh]), solb[hh]) for hh in hs_}
        for hh in hs_:
            op_s[slot, hh] = asol[hh][:, GDN_DK:]
            qp = (qd_s[hh] - asol[hh][:, :GDN_DK]).astype(BF16)
            for ch in range(wd):
                mq_s[slot, hh, ch, GDN_DK:GDN_DK + c, :] = qp[ch * c:(ch + 1) * c]
        yield
        for ch in range(wd):
            cr = slice(ch * c, (ch + 1) * c)
            mns = {hh: _dot_tn(kd_s[hh, cr, :], solb[hh][cr]) for hh in hs_}
            for hh in hs_:
                mq_s[slot, hh, ch, 0:GDN_DK, :] = mns[hh][:, :GDN_DK].astype(BF16)
                nn_s[slot, hh, ch] = mns[hh][:, GDN_DK:]
            if ch % 2 == 1:
                yield

    def scan_steps(i, slot):
        for ch in range(wd):
            rows = pl.ds(pl.multiple_of(i * blk + ch * c, c), c)
            for hh in hs_all:
                cols = slice(hh * GDN_DK, (hh + 1) * GDN_DK)
                state = st_s[hh]
                x = _dot(mq_s[slot, hh, ch], state.astype(BF16))
                st_s[hh] = state * eg_s[slot, hh, ch * 8:ch * 8 + 1, :] - x[:GDN_DK] + nn_s[slot, hh, ch]
                o = x[GDN_DK:] + op_s[slot, hh, ch * c:(ch + 1) * c, :]
                y = o * lax.rsqrt(jnp.mean(o * o, -1, keepdims=True) + EPS) * nw_ref[...]
                gt = gt_ref[0, rows, cols].astype(F32)
                o_ref[0, rows, cols] = (y * _silu(gt)).astype(o_ref.dtype)
            yield

    def interleave(gens):
        gens = list(gens)
        while gens:
            for gen in list(gens):
                try:
                    next(gen)
                except StopIteration:
                    gens.remove(gen)

    def stage(prep_i, prep_slot, scan_i, scan_slot):
        gens = []
        if prep_i is not None:
            gens += [prep_steps(prep_i, prep_slot, hs_all[g0:g0 + GDN_LOCKSTEP])
                     for g0 in range(0, GDN_HG, GDN_LOCKSTEP)]
        if scan_i is not None:
            gens.append(scan_steps(scan_i, scan_slot))
        interleave(gens)

    assert n_blocks >= 2 and n_blocks % 2 == 0
    st_s[...] = jnp.zeros_like(st_s)
    stage(0, 0, None, None)

    def body(j, carry):
        stage(2 * j + 1, 1, 2 * j, 0)
        stage(2 * j + 2, 0, 2 * j + 1, 1)
        return carry

    lax.fori_loop(0, (n_blocks - 2) // 2, body, 0)
    stage(n_blocks - 1, 1, n_blocks - 2, 0)
    stage(None, None, n_blocks - 1, 1)


def _gdn(qkvg, small, a_log, dt_bias, norm_w):
    b, s, _ = qkvg.shape
    gw = GDN_HG * GDN_DK
    groups = GDN_HEADS // GDN_HG
    blk = lambda off: pl.BlockSpec((1, s, gw), lambda i, j: (i, 0, off + j), pipeline_mode=pl.Buffered(1))
    prm = jnp.zeros((8, LANES), F32)
    prm = prm.at[0, GDN_HEADS:2 * GDN_HEADS].set(a_log).at[1, GDN_HEADS:2 * GDN_HEADS].set(dt_bias)
    return pl.pallas_call(
        functools.partial(_gdn_kernel, seq=s),
        grid=(b, groups),
        in_specs=[pl.BlockSpec((8, LANES), lambda i, j: (0, 0)),
                  blk(0), blk(groups), blk(2 * groups), blk(3 * groups),
                  pl.BlockSpec((1, s, LANES), lambda i, j: (i, 0, 0)),
                  pl.BlockSpec((1, GDN_DV), lambda i, j: (0, 0))],
        out_specs=pl.BlockSpec((1, s, gw), lambda i, j: (i, 0, j)),
        out_shape=jax.ShapeDtypeStruct((b, s, GDN_W), BF16),
        scratch_shapes=[pltpu.VMEM((2, GDN_HG, GDN_WIDE, GDN_DK + GDN_CHUNK, GDN_DK), BF16),
                        pltpu.VMEM((2, GDN_HG, GDN_WIDE, GDN_DK, GDN_DV), F32),
                        pltpu.VMEM((2, GDN_HG, GDN_BLK, GDN_DV), F32),
                        pltpu.VMEM((2, GDN_HG, GDN_WIDE * 8, LANES), F32),
                        pltpu.VMEM((GDN_HG, GDN_DK, GDN_DV), F32),
                        pltpu.VMEM((GDN_HG, GDN_BLK, GDN_DK + GDN_DV), F32),
                        pltpu.VMEM((GDN_HG, GDN_BLK, GDN_DK), BF16),
                        pltpu.VMEM((GDN_HG, GDN_BLK, GDN_DK), F32),
                        pltpu.VMEM((GDN_HG, GDN_CHUNK, GDN_BLK), F32)],
        compiler_params=_cparams(2),
        name="gated_delta_rule",
    )(prm, qkvg, qkvg, qkvg, qkvg, small, norm_w.reshape(1, GDN_DV))


def kernel(x, positions, ev_w_in, ev_w_out, od_w_in, od_conv_w, od_a_log, od_dt_bias, od_norm_w, od_w_out,
           ffn_w_up, ffn_conv_w, ffn_conv_b, ffn_w_down, ln1_g, ln1_b, ln2_g, ln2_b):
    b, s, d = x.shape
    pos3 = positions.reshape(b, s, 1)
    ret_consts = _ret_consts()
    row = lambda a: a.reshape(1, d)

    for layer in range(DEPTH):
        j = layer // 2
        if layer == 0:
            hproj, tab_ret, tab_dil = _proj(x, ev_w_in[j].astype(BF16), ROW_TILE, 512, pos3, _rope_freqs())
        elif layer % 2 == 0:
            hproj = _proj(x, ev_w_in[j].astype(BF16), ROW_TILE, 512)
        if layer % 2 == 0:
            ya = _retention(hproj, tab_ret, ret_consts)
            yb = _dilated(hproj, tab_dil)
            w_out = ev_w_out[j].astype(BF16)
            x = _out_ln(x, [ya, yb], [w_out[:RET_V_W], w_out[RET_V_W:]], row(ln1_g[layer]), row(ln1_b[layer]),
                        ROW_TILE)
        else:
            w_in = od_w_in[j]
            w_main = w_in[:, :4 * GDN_W].astype(BF16)
            w_small = jnp.pad(w_in[:, 4 * GDN_W:], ((0, 0), (0, LANES - 2 * GDN_HEADS))).astype(BF16)
            cw = jnp.pad(od_conv_w[j], ((0, 8 - GDN_CONV), (0, 0)))
            qkvg, small = _odd_in(x, w_main, cw, w_small, ROW_TILE)
            yc = _gdn(qkvg, small, od_a_log[j], od_dt_bias[j], od_norm_w[j])
            x = _out_ln(x, [yc], [od_w_out[j].astype(BF16)], row(ln1_g[layer]), row(ln1_b[layer]), ROW_TILE)
        wup_r, cw_r, wdn_r = _ffn_weights(ffn_w_up[layer], ffn_conv_w[layer], ffn_conv_b[layer], ffn_w_down[layer])
        x = _ffn(x, wup_r, cw_r, wdn_r, row(ln2_g[layer]), row(ln2_b[layer]), ROW_TILE)
    return x
```

```python
import functools
import math

import jax
import jax.numpy as jnp
import numpy as np
from jax import lax
from jax.experimental import pallas as pl
from jax.experimental.pallas import tpu as pltpu

F32 = jnp.float32
BF16 = jnp.bfloat16

D_MODEL = 1024
DEPTH = 4
RET_HEADS, RET_DK, RET_DV, RET_CHUNK, RET_THETA = 4, 128, 256, 128, 10000.0
DIL_HEADS, DIL_HD, DIL_BLOCK = 8, 64, 128
DIL_DILATIONS = (1, 4, 16)
DIL_INTERLEAVE = 2
ROPE_THETA, ROPE_DIMS = 500000.0, DIL_HD // 4
GDN_HEADS, GDN_DK, GDN_DV, GDN_CHUNK, GDN_CONV = 8, 128, 128, 64, 4
D_FF, FFN_CONV = 2816, 3
DN_ALPHA = (2.0 * DEPTH) ** 0.25
EPS = 1e-5

RET_QK_W = RET_HEADS * RET_DK
RET_V_W = RET_HEADS * RET_DV
DIL_W = DIL_HEADS * DIL_HD
EV_IN = 2 * RET_QK_W + 2 * RET_V_W + 3 * DIL_W
GDN_W = GDN_HEADS * GDN_DK

LANES = 128
HALO = 16
HIST = 8
ROW_TILE = 1024
VMEM_LIMIT = 56 * 1024 * 1024
GDN_VMEM_LIMIT = 60 * 1024 * 1024


def _cparams(n_axes, vmem=VMEM_LIMIT):
    return pltpu.CompilerParams(dimension_semantics=("arbitrary",) * n_axes, vmem_limit_bytes=vmem)


def _dot(a, b):
    return jnp.dot(a, b, preferred_element_type=F32)


def _dot_nt(a, b):
    return lax.dot_general(a, b, (((1,), (1,)), ((), ())), preferred_element_type=F32)


def _dot_tn(a, b):
    return lax.dot_general(a, b, (((0,), (0,)), ((), ())), preferred_element_type=F32)


def _sigmoid(x):
    return 1.0 / (1.0 + jnp.exp(-x))


def _silu(x):
    return x * _sigmoid(x)


def _layernorm_rows(z, g, b):
    mu = jnp.mean(z, -1, keepdims=True)
    zc = z - mu
    var = jnp.mean(zc * zc, -1, keepdims=True)
    return zc * lax.rsqrt(var + EPS) * g + b


TRIG_ROWS = 256


def _proj_kernel(x_ref, w_ref, *rest, tn, tables):
    if tables:
        pos_ref, f_ref, o_ref, tr_ref, td_ref = rest
        tm = x_ref.shape[1]
        units = [(dst, r) for r in range(tm // TRIG_ROWS) for dst in (tr_ref, td_ref)]
    else:
        (o_ref,) = rest
        units = []
    n_pieces = w_ref.shape[1] // tn
    per_piece = -(-len(units) // n_pieces)
    xb = x_ref[0].astype(BF16)
    for j in range(n_pieces):
        cols = slice(j * tn, (j + 1) * tn)
        o_ref[0, :, cols] = _dot(xb, w_ref[:, cols]).astype(o_ref.dtype)
        for dst, r in units[j * per_piece:(j + 1) * per_piece]:
            rows = slice(r * TRIG_ROWS, (r + 1) * TRIG_ROWS)
            f0 = 0 if dst is tr_ref else 2
            ang = pos_ref[0, rows, :].astype(F32) * f_ref[f0:f0 + 1, :]
            dst[0, 0, rows, :] = jnp.cos(ang)
            dst[0, 1, rows, :] = jnp.sin(ang) * f_ref[f0 + 1:f0 + 2, :]


def _proj(x3, w_bf16, tm, tn, pos3=None, freqs=None):
    bsz, s, k = x3.shape
    n = w_bf16.shape[1]
    tables = pos3 is not None
    in_specs = [pl.BlockSpec((1, tm, k), lambda bi, i: (bi, i, 0)),
                pl.BlockSpec((k, n), lambda bi, i: (0, 0), pipeline_mode=pl.Buffered(1))]
    out_specs = [pl.BlockSpec((1, tm, n), lambda bi, i: (bi, i, 0))]
    out_shape = [jax.ShapeDtypeStruct((bsz, s, n), BF16)]
    args = [x3, w_bf16]
    if tables:
        in_specs += [pl.BlockSpec((1, tm, 1), lambda bi, i: (bi, i, 0)),
                     pl.BlockSpec((8, LANES), lambda bi, i: (0, 0))]
        tab_spec = pl.BlockSpec((1, 2, tm, LANES), lambda bi, i: (bi, 0, i, 0))
        out_specs += [tab_spec, tab_spec]
        out_shape += [jax.ShapeDtypeStruct((bsz, 2, s, LANES), F32)] * 2
        args += [pos3, freqs]
    outs = pl.pallas_call(
        functools.partial(_proj_kernel, tn=tn, tables=tables),
        grid=(bsz, s // tm),
        in_specs=in_specs,
        out_specs=out_specs,
        out_shape=out_shape,
        compiler_params=_cparams(2),
        name="ev_in_proj",
    )(*args)
    return outs if tables else outs[0]


def _rope_freqs():
    half = RET_DK // 2
    inv = jnp.power(RET_THETA, -jnp.arange(half, dtype=F32) * 2.0 / RET_DK)
    one = jnp.ones((half,), F32)
    ret = jnp.stack([jnp.concatenate([inv, inv]), jnp.concatenate([-one, one])])
    hh = ROPE_DIMS // 2
    invd = jnp.power(ROPE_THETA, -jnp.arange(hh, dtype=F32) * 2.0 / ROPE_DIMS)
    z = jnp.zeros((DIL_HD - ROPE_DIMS,), F32)
    oneh = jnp.ones((hh,), F32)
    f_ang = jnp.concatenate([invd, invd, z])
    f_sgn = jnp.concatenate([-oneh, oneh, z])
    dil = jnp.stack([jnp.tile(f_ang, 2), jnp.tile(f_sgn, 2)])
    return jnp.concatenate([ret, dil, jnp.zeros((4, LANES), F32)], 0)


def _ret_kernel(gch_ref, q_ref, k_ref, v_ref, g_ref, tab_ref, dm_ref, zt_ref, xi_ref, o_ref, r_ref, *, n_chunks):
    r_ref[...] = jnp.zeros_like(r_ref)
    c = RET_CHUNK
    heads = range(RET_HEADS)

    def body(ci, carry):
        rows = pl.ds(pl.multiple_of(ci * c, c), c)
        cosr = tab_ref[0, 0, rows, :]
        sinr = tab_ref[0, 1, rows, :]
        qk = lambda h: slice(h * RET_DK, (h + 1) * RET_DK)
        vv = lambda h: slice(h * RET_DV, (h + 1) * RET_DV)
        qs = [q_ref[0, rows, qk(h)].astype(F32) for h in heads]
        ks = [k_ref[0, rows, qk(h)].astype(F32) for h in heads]
        qr = [q * cosr + pltpu.roll(q, RET_DK // 2, 1) * sinr for q in qs]
        kr = [(k * cosr + pltpu.roll(k, RET_DK // 2, 1) * sinr) * (RET_DK ** -0.5) for k in ks]
        vs = [v_ref[0, rows, vv(h)] for h in heads]
        r_prev = [r_ref[h] for h in heads]
        scores = [_dot_nt(qr[h].astype(BF16), kr[h].astype(BF16)) * dm_ref[h] for h in heads]
        inter = [_dot((qr[h] * xi_ref[h]).astype(BF16), r_prev[h].astype(BF16)) for h in heads]
        kv = [_dot_tn((kr[h] * zt_ref[h]).astype(BF16), vs[h]) for h in heads]
        os_ = [_dot(scores[h].astype(BF16), vs[h]) + inter[h] for h in heads]
        for h in heads:
            r_ref[h] = r_prev[h] * gch_ref[h] + kv[h]
            o = os_[h]
            y = o * lax.rsqrt(jnp.mean(o * o, -1, keepdims=True) + EPS)
            g = g_ref[0, rows, vv(h)].astype(F32)
            o_ref[0, rows, vv(h)] = (y * _silu(g)).astype(o_ref.dtype)
        return carry

    lax.fori_loop(0, n_chunks, body, 0, unroll=2)


def _ret_consts():
    hh = np.arange(RET_HEADS, dtype=np.float64)
    lg = np.log1p(-np.power(2.0, -5.0 - hh))
    idx = np.arange(RET_CHUNK, dtype=np.float64)
    rel = idx[:, None] - idx[None, :]
    dmat = np.where(rel >= 0, np.exp(np.maximum(rel, 0.0) * lg[:, None, None]), 0.0)
    zeta = np.exp((RET_CHUNK - 1 - idx) * lg[:, None])
    xi = np.exp((idx + 1) * lg[:, None])
    gch = np.exp(RET_CHUNK * lg)
    bc = lambda a: np.broadcast_to(a[:, :, None], (RET_HEADS, RET_CHUNK, RET_DK))
    return (jnp.asarray(gch, F32), jnp.asarray(dmat, F32), jnp.asarray(bc(zeta), F32), jnp.asarray(bc(xi), F32))


def _retention(hproj, tab_ret, consts):
    b, s, _ = hproj.shape
    gch, dmat, zeta, xi = consts
    v_off = 2 * RET_QK_W // RET_V_W
    all_heads = pl.BlockSpec((RET_HEADS, RET_CHUNK, RET_DK), lambda i: (0, 0, 0))
    return pl.pallas_call(
        functools.partial(_ret_kernel, n_chunks=s // RET_CHUNK),
        grid=(b,),
        in_specs=[pl.BlockSpec(memory_space=pltpu.SMEM),
                  pl.BlockSpec((1, s, RET_QK_W), lambda i: (i, 0, 0)),
                  pl.BlockSpec((1, s, RET_QK_W), lambda i: (i, 0, 1)),
                  pl.BlockSpec((1, s, RET_V_W), lambda i: (i, 0, v_off)),
                  pl.BlockSpec((1, s, RET_V_W), lambda i: (i, 0, v_off + 1)),
                  pl.BlockSpec((1, 2, s, LANES), lambda i: (i, 0, 0, 0)),
                  all_heads, all_heads, all_heads],
        out_specs=pl.BlockSpec((1, s, RET_V_W), lambda i: (i, 0, 0)),
        out_shape=jax.ShapeDtypeStruct((b, s, RET_V_W), BF16),
        scratch_shapes=[pltpu.VMEM((RET_HEADS, RET_DK, RET_DV), F32)],
        compiler_params=_cparams(1),
        name="retention",
    )(gch, hproj, hproj, hproj, hproj, tab_ret, dmat, zeta, xi)


def _dil_kernel(q_ref, k_ref, v_ref, tab_ref, o_ref, qs, ks, vs, acc, ms, ls, *, seq):
    rc = 256
    lane = lax.broadcasted_iota(jnp.int32, (DIL_BLOCK, LANES), 1)
    first_head = lane < DIL_HD

    def rot_body(i, carry):
        rows = pl.ds(pl.multiple_of(i * rc, rc), rc)
        cosd = tab_ref[0, 0, rows, :]
        sind = tab_ref[0, 1, rows, :]
        hh = ROPE_DIMS // 2
        low_half = (lax.broadcasted_iota(jnp.int32, (rc, LANES), 1) & (DIL_HD - 1)) < hh

        def rot(x):
            return x * cosd + jnp.where(low_half, pltpu.roll(x, LANES - hh, 1), pltpu.roll(x, hh, 1)) * sind

        qs[rows, :] = rot(q_ref[0, rows, :].astype(F32)) * (DIL_HD ** -0.5)
        ks[rows, :] = rot(k_ref[0, rows, :].astype(F32))
        vs[rows, :] = v_ref[0, rows, :].astype(F32)
        return carry

    lax.fori_loop(0, seq // rc, rot_body, 0)

    def blocks(g, d, kb_rows, specs):
        def sl(start, n):
            return pl.ds(start, n) if d == 1 else pl.ds(start, n, stride=d)

        qsl = [sl(q0, DIL_BLOCK) for q0, _, _ in specs]
        ksl = [sl(k0, kb_rows) for _, k0, _ in specs]
        qbs = [qs[s_, :] for s_ in qsl]
        kbs = [ks[s_, :].astype(BF16) for s_ in ksl]
        vbs = [vs[s_, :].astype(BF16) for s_ in ksl]
        nb_ = range(len(specs))
        q2 = [jnp.concatenate([jnp.where(first_head, qbs[b_], 0.0), jnp.where(first_head, 0.0, qbs[b_])], 0)
              .astype(BF16) for b_ in nb_]
        sc = [_dot_nt(q2[b_], kbs[b_]) + jnp.concatenate([specs[b_][2]] * 2, 0) for b_ in nb_]
        yield
        mx = [jnp.max(s_, -1, keepdims=True) for s_ in sc]
        pr = [jnp.exp(s_ - m_).astype(BF16) for s_, m_ in zip(sc, mx)]
        yield
        ones_v = jnp.ones((kb_rows, LANES), BF16)
        pv = [_dot(pr[b_], jnp.concatenate([vbs[b_], ones_v], 1)) for b_ in nb_]
        yield
        h0, h1 = slice(0, DIL_BLOCK), slice(DIL_BLOCK, 2 * DIL_BLOCK)
        for b_ in nb_:
            acc[g, qsl[b_], :] = jnp.where(first_head, pv[b_][h0, :LANES], pv[b_][h1, :LANES])
            ms[g, qsl[b_], :] = jnp.where(first_head, mx[b_][h0], mx[b_][h1])
            ls[g, qsl[b_], :] = jnp.where(first_head, pv[b_][h0, LANES:], pv[b_][h1, LANES:])

    def window_bias(kb_rows, off):
        dist = (lax.broadcasted_iota(jnp.int32, (DIL_BLOCK, kb_rows), 0)
                - lax.broadcasted_iota(jnp.int32, (DIL_BLOCK, kb_rows), 1)) + off
        return jnp.where((dist >= 0) & (dist <= DIL_BLOCK), 0.0, -jnp.inf).astype(F32)

    bias_self = window_bias(DIL_BLOCK, 0)
    bias_first = window_bias(2 * DIL_BLOCK, 0)
    bias_band = window_bias(2 * DIL_BLOCK, DIL_BLOCK)

    nbi = DIL_INTERLEAVE
    groups = []
    for g, d in enumerate(DIL_DILATIONS):
        nb = seq // d // DIL_BLOCK
        span = DIL_BLOCK * d
        if nb == 1:
            specs = [(r, r, bias_self) for r in range(d)]
            kb_rows = DIL_BLOCK
        else:
            specs = [(r + n * span, r + max(n - 1, 0) * span, bias_band if n else bias_first)
                     for n in range(nb) for r in range(d)]
            kb_rows = 2 * DIL_BLOCK
        groups += [blocks(g, d, kb_rows, specs[i:i + nbi]) for i in range(0, len(specs), nbi)]

    pending, active = iter(groups), []
    while True:
        nxt = next(pending, None)
        if nxt is not None:
            active.append(nxt)
        if not active:
            break
        for gen in list(active):
            try:
                next(gen)
            except StopIteration:
                active.remove(gen)

    def comb_body(i, carry):
        rows = pl.ds(pl.multiple_of(i * rc, rc), rc)
        m0, m1, m2 = ms[0, rows, :], ms[1, rows, :], ms[2, rows, :]
        mmax = jnp.maximum(jnp.maximum(m0, m1), m2)
        e0, e1, e2 = jnp.exp(m0 - mmax), jnp.exp(m1 - mmax), jnp.exp(m2 - mmax)
        num = e0 * acc[0, rows, :] + e1 * acc[1, rows, :] + e2 * acc[2, rows, :]
        den = e0 * ls[0, rows, :] + e1 * ls[1, rows, :] + e2 * ls[2, rows, :]
        o_ref[0, rows, :] = (num / den).astype(o_ref.dtype)
        return carry

    lax.fori_loop(0, seq // rc, comb_body, 0)


def _dilated(hproj, tab_dil):
    b, s, _ = hproj.shape
    base = (2 * RET_QK_W + 2 * RET_V_W) // LANES
    pairs = DIL_W // LANES
    blk = lambda off: pl.BlockSpec((1, s, LANES), lambda i, j: (i, 0, base + off + j))
    n_g = len(DIL_DILATIONS)
    return pl.pallas_call(
        functools.partial(_dil_kernel, seq=s),
        grid=(b, pairs),
        in_specs=[blk(0), blk(pairs), blk(2 * pairs),
                  pl.BlockSpec((1, 2, s, LANES), lambda i, j: (i, 0, 0, 0))],
        out_specs=pl.BlockSpec((1, s, LANES), lambda i, j: (i, 0, j)),
        out_shape=jax.ShapeDtypeStruct((b, s, DIL_W), BF16),
        scratch_shapes=[pltpu.VMEM((s, LANES), F32)] * 3 + [pltpu.VMEM((n_g, s, LANES), F32)] * 3,
        compiler_params=_cparams(2),
        name="dilated_attention",
    )(hproj, hproj, hproj, tab_dil)


def _out_ln_kernel(*refs, n_in):
    x_ref = refs[0]
    ys = refs[1:1 + n_in]
    ws = refs[1 + n_in:1 + 2 * n_in]
    g_ref, b_ref, o_ref = refs[1 + 2 * n_in:]
    tm = x_ref.shape[1]
    halves = [slice(0, tm // 2), slice(tm // 2, tm)]
    zs = []
    for rows in halves:
        z = DN_ALPHA * x_ref[0, rows, :]
        for y_ref, w_ref in zip(ys, ws):
            z = z + _dot(y_ref[0, rows, :], w_ref[...])
        zs.append(z)
    for rows, z in zip(halves, zs):
        o_ref[0, rows, :] = _layernorm_rows(z, g_ref[...], b_ref[...])


def _out_ln(x3, ys, ws, g, b, tm):
    bsz, s, d = x3.shape
    n_in = len(ys)
    row = lambda w: pl.BlockSpec((1, tm, w), lambda bi, i: (bi, i, 0))
    full = lambda a: pl.BlockSpec(a.shape, lambda bi, i: (0, 0))
    return pl.pallas_call(
        functools.partial(_out_ln_kernel, n_in=n_in),
        grid=(bsz, s // tm),
        in_specs=[row(d)] + [row(y.shape[2]) for y in ys] + [full(w) for w in ws] + [full(g), full(b)],
        out_specs=row(d),
        out_shape=jax.ShapeDtypeStruct((bsz, s, d), F32),
        compiler_params=_cparams(2),
        name="out_proj_ln",
    )(x3, *ys, *ws, g, b)


def _conv_taps(u_ref, row0, rows, cols, taps):
    k_w = len(taps)
    ub = u_ref[row0 - 8:row0 + rows, cols]
    out = ub[8:] * taps[k_w - 1]
    for back in range(1, k_w):
        out = out + pltpu.roll(ub, back, 0)[8:] * taps[k_w - 1 - back]
    return out


def _fill_xb(xb, x_ref, xh_ref, i):
    xb[HALO:, :] = x_ref[0].astype(BF16)
    xb[0:HALO, :] = jnp.where(i > 0, xh_ref[0], 0.0).astype(BF16)


FFN_PIECE = 256


def _project_with_history(xb, w_ref, cols_w, u_ref, cols_u, hist, tm):
    u_ref[0:HIST, cols_u] = hist[:, cols_w]
    u = _dot(xb, w_ref[:, cols_w])
    u_ref[HIST:, cols_u] = u
    hist[:, cols_w] = u[tm - HIST:, :]


def _ffn_kernel(x_ref, wup_ref, cw_ref, wdn_ref, g_ref, b_ref, o_ref, us, hs, hist, *, tm, rb):
    @pl.when(pl.program_id(1) == 0)
    def _():
        hist[...] = jnp.zeros_like(hist)

    xb = x_ref[0].astype(BF16)
    pc = FFN_PIECE
    n_pieces = D_FF // pc

    def up(p):
        u = us.at[p % 2]
        _project_with_history(xb, wup_ref, slice(pc * p, pc * (p + 1)), u, slice(0, pc), hist, tm)
        _project_with_history(xb, wup_ref, slice(D_FF + pc * p, D_FF + pc * (p + 1)), u, slice(pc, 2 * pc), hist, tm)

    def gate_piece(p):
        u = us.at[p % 2]
        for r in range(tm // rb):
            row0 = r * rb + HIST
            for h in range(pc // LANES):
                cg = slice(h * LANES, (h + 1) * LANES)
                cv = slice(pc + h * LANES, pc + (h + 1) * LANES)
                wg = slice(pc * p + h * LANES, pc * p + (h + 1) * LANES)
                wv = slice(D_FF + pc * p + h * LANES, D_FF + pc * p + (h + 1) * LANES)
                tg = [cw_ref[j:j + 1, wg] for j in range(FFN_CONV)]
                tv = [cw_ref[j:j + 1, wv] for j in range(FFN_CONV)]
                gate = _conv_taps(u, row0, rb, cg, tg) + cw_ref[FFN_CONV:FFN_CONV + 1, wg]
                val = _conv_taps(u, row0, rb, cv, tv) + cw_ref[FFN_CONV:FFN_CONV + 1, wv]
                hs[r * rb:(r + 1) * rb, wg] = (_silu(gate) * val).astype(BF16)

    up(0)
    for p in range(1, n_pieces):
        up(p)
        gate_piece(p - 1)
    gate_piece(n_pieces - 1)
    halves = [slice(0, tm // 2), slice(tm // 2, tm)]
    zs = [DN_ALPHA * x_ref[0, rows, :] + _dot(hs[rows, :], wdn_ref[...]) for rows in halves]
    for rows, z in zip(halves, zs):
        o_ref[0, rows, :] = _layernorm_rows(z, g_ref[...], b_ref[...])


def _ffn(x3, wup_r, cw_r, wdn_r, g, b, tm):
    bsz, s, d = x3.shape
    const = lambda a: pl.BlockSpec(a.shape, lambda bi, i: (0, 0), pipeline_mode=pl.Buffered(1))
    return pl.pallas_call(
        functools.partial(_ffn_kernel, tm=tm, rb=128),
        grid=(bsz, s // tm),
        in_specs=[pl.BlockSpec((1, tm, d), lambda bi, i: (bi, i, 0)),
                  const(wup_r), const(cw_r), const(wdn_r), const(g), const(b)],
        out_specs=pl.BlockSpec((1, tm, d), lambda bi, i: (bi, i, 0)),
        out_shape=jax.ShapeDtypeStruct((bsz, s, d), F32),
        scratch_shapes=[pltpu.VMEM((2, HIST + tm, 2 * FFN_PIECE), F32),
                        pltpu.VMEM((tm, D_FF), BF16),
                        pltpu.VMEM((HIST, 2 * D_FF), F32)],
        compiler_params=_cparams(2),
        name="conv_ffn_ln",
    )(x3, wup_r, cw_r, wdn_r, g, b)


def _ffn_weights(w_up, conv_w, conv_b, w_down):
    cw = jnp.concatenate([conv_w, conv_b[None, :], jnp.zeros((8 - FFN_CONV - 1, 2 * D_FF), F32)], 0)
    return w_up.astype(BF16), cw, w_down.astype(BF16)


ODD_PIECE = 512


def _odd_in_kernel(x_ref, xh_ref, w_ref, cw_ref, ws_ref, o_ref, os_ref, xb, us, *, tm, conv_cols, rb):
    i = pl.program_id(1)
    _fill_xb(xb, x_ref, xh_ref, i)
    pc = ODD_PIECE
    n_pieces = w_ref.shape[1] // pc

    def up(p):
        us[p % 2] = _dot(xb[...], w_ref[:, pc * p:pc * (p + 1)])

    def finish(p):
        u = us.at[p % 2]
        if pc * p >= conv_cols:
            o_ref[0, :, pc * p:pc * (p + 1)] = u[HALO:, :].astype(o_ref.dtype)
            return
        for r in range(tm // rb):
            for h in range(pc // LANES):
                cs = slice(h * LANES, (h + 1) * LANES)
                ws = slice(pc * p + h * LANES, pc * p + (h + 1) * LANES)
                taps = [cw_ref[j:j + 1, ws] for j in range(GDN_CONV)]
                y = _conv_taps(u, r * rb + HALO, rb, cs, taps)
                y = _silu(y)
                if pc * p + h * LANES < 2 * GDN_W:
                    y = y * lax.rsqrt(jnp.sum(y * y, -1, keepdims=True) + 1e-6)
                    if pc * p + h * LANES < GDN_W:
                        y = y * (GDN_DK ** -0.5)
                o_ref[0, r * rb:(r + 1) * rb, ws] = y.astype(o_ref.dtype)

    up(0)
    os_ref[0] = _dot(xb[HALO:, :], ws_ref[...])
    for p in range(1, n_pieces):
        up(p)
        finish(p - 1)
    finish(n_pieces - 1)


def _odd_in(x3, w_main, cw, w_small, tm):
    bsz, s, d = x3.shape
    n = w_main.shape[1]
    hb = tm // HALO
    const = lambda a: pl.BlockSpec(a.shape, lambda bi, i: (0, 0), pipeline_mode=pl.Buffered(1))
    return pl.pallas_call(
        functools.partial(_odd_in_kernel, tm=tm, conv_cols=cw.shape[1], rb=128),
        grid=(bsz, s // tm),
        in_specs=[pl.BlockSpec((1, tm, d), lambda bi, i: (bi, i, 0)),
                  pl.BlockSpec((1, HALO, d), lambda bi, i: (bi, jnp.maximum(i * hb - 1, 0), 0)),
                  const(w_main), const(cw), const(w_small)],
        out_specs=[pl.BlockSpec((1, tm, n), lambda bi, i: (bi, i, 0)),
                   pl.BlockSpec((1, tm, LANES), lambda bi, i: (bi, i, 0))],
        out_shape=[jax.ShapeDtypeStruct((bsz, s, n), BF16),
                   jax.ShapeDtypeStruct((bsz, s, LANES), F32)],
        scratch_shapes=[pltpu.VMEM((HALO + tm, d), BF16),
                        pltpu.VMEM((2, HALO + tm, ODD_PIECE), F32)],
        compiler_params=_cparams(2),
        name="od_in_proj_conv",
    )(x3, x3, w_main, cw, w_small)


GDN_WIDE = 4
GDN_BLK = GDN_CHUNK * GDN_WIDE
GDN_HG = 8
GDN_LOCKSTEP = 2


def _split_bf16(a):
    hi = a.astype(BF16)
    return hi, (a - hi.astype(F32)).astype(BF16)


def _gdn_kernel(prm_ref, q_ref, k_ref, v_ref, gt_ref, sm_ref, nw_ref, o_ref,
                mq_s, nn_s, op_s, eg_s, st_s, rhs_s, kd_s, qd_s, at_s, *, seq):
    hg = pl.program_id(1)
    c, wd, blk = GDN_CHUNK, GDN_WIDE, GDN_BLK
    ww = c * wd
    n_blocks = seq // blk
    shift = int(math.log2(c))

    r_w = lax.broadcasted_iota(jnp.int32, (c, ww), 0)
    l_w = lax.broadcasted_iota(jnp.int32, (c, ww), 1)
    j_w = l_w & (c - 1)
    chunk_of_lane = l_w >> shift
    tri_w = r_w >= j_w
    strict_w = r_w > j_w
    upper_w = jnp.where(r_w <= j_w, 1.0, 0.0).astype(F32)
    bmask = (lax.broadcasted_iota(jnp.int32, (ww, ww), 0) >> shift) == (lax.broadcasted_iota(jnp.int32, (ww, ww), 1) >> shift)
    ltri = jnp.where(lax.broadcasted_iota(jnp.int32, (c, c), 0) >= lax.broadcasted_iota(jnp.int32, (c, c), 1),
                     1.0, 0.0).astype(BF16)
    ones_cc = jnp.ones((c, c), BF16)
    lane_blk = lax.broadcasted_iota(jnp.int32, (blk, LANES), 1)
    first_half = lax.broadcasted_iota(jnp.int32, (c, LANES), 1) < c
    a_neg = -jnp.exp(prm_ref[0:1, :])
    dtb = prm_ref[1:2, :]

    def bdiag(xw):
        return jnp.where(bmask, jnp.concatenate([xw] * wd, 0), 0.0).astype(BF16)

    hs_all = list(range(GDN_HG))

    def widen(col):
        out = jnp.broadcast_to(col[0:c], (c, ww))
        for ch in range(1, wd):
            out = jnp.where(chunk_of_lane == ch, jnp.broadcast_to(col[ch * c:(ch + 1) * c], (c, ww)), out)
        return out

    def prep_steps(i, slot, hs_):
        rows = pl.ds(pl.multiple_of(i * blk, blk), blk)
        sm = sm_ref[0, rows, :]
        beta_all = _sigmoid(sm)
        sp_in = sm + dtb
        g_all = a_neg * (jnp.maximum(sp_in, 0.0) + jnp.log1p(jnp.exp(-jnp.abs(sp_in))))
        cols = {hh: slice(hh * GDN_DK, (hh + 1) * GDN_DK) for hh in hs_}
        head = {hh: hg * GDN_HG + hh for hh in hs_}
        qn = {hh: q_ref[0, rows, cols[hh]].astype(F32) for hh in hs_}
        kn = {hh: k_ref[0, rows, cols[hh]].astype(F32) for hh in hs_}
        beta = {hh: jnp.sum(jnp.where(lane_blk == head[hh], beta_all, 0.0), -1, keepdims=True) for hh in hs_}
        g = {hh: jnp.sum(jnp.where(lane_blk == GDN_HEADS + head[hh], g_all, 0.0), -1, keepdims=True) for hh in hs_}
        kb = {hh: kn[hh] * beta[hh] for hh in hs_}
        g_w = {hh: widen(g[hh]) for hh in hs_}
        g_sp = {hh: _split_bf16(g_w[hh]) for hh in hs_}
        gu_sp = {hh: _split_bf16(g_w[hh] * upper_w) for hh in hs_}
        gc_col = {hh: _dot(ltri, g_sp[hh][0]) + _dot(ltri, g_sp[hh][1]) for hh in hs_}
        gc_row = {hh: _dot(ones_cc, gu_sp[hh][0]) + _dot(ones_cc, gu_sp[hh][1]) for hh in hs_}
        yield
        grams = {hh: [_dot_nt(jnp.concatenate([kb[hh][2 * p * c:(2 * p + 2) * c], qn[hh][2 * p * c:(2 * p + 2) * c]], 0)
                              .astype(BF16), kn[hh][2 * p * c:(2 * p + 2) * c].astype(BF16))
                      for p in range(wd // 2)] for hh in hs_}
        yield
        pws = {}
        for hh in hs_:
            decay = jnp.where(tri_w, jnp.exp(jnp.where(tri_w, gc_col[hh] - gc_row[hh], 0.0)), 0.0)
            a_w = jnp.concatenate([jnp.where(first_half, gm[0:c], gm[c:2 * c]) for gm in grams[hh]], 1)
            qk_w = jnp.concatenate([jnp.where(first_half, gm[2 * c:3 * c], gm[3 * c:4 * c]) for gm in grams[hh]], 1)
            pws[hh] = -jnp.where(strict_w, a_w * decay, 0.0)
            at_s[hh] = jnp.where(tri_w, qk_w * decay, 0.0)
        ews = dict(pws)
        pws = {hh: _dot(pws[hh].astype(BF16), bdiag(pws[hh])) for hh in hs_}
        for hh in hs_:
            gc_parts, gl_parts = [], []
            for p in range(wd // 2):
                x = gc_col[hh][:, 2 * p * c:(2 * p + 2) * c]
                xr = pltpu.roll(x, c, 1)
                for part in (jnp.where(first_half, x, xr), jnp.where(first_half, xr, x)):
                    gc_parts.append(part)
                    gl_parts.append(jnp.broadcast_to(part[c - 1:c, :], (c, LANES)))
            gc_t = jnp.concatenate(gc_parts, 0)
            gl_t = jnp.concatenate(gl_parts, 0)
            egc = jnp.exp(gc_t)
            rhs_s[hh, :, 0:GDN_DK] = kb[hh] * egc
            rhs_s[hh, :, GDN_DK:] = v_ref[0, rows, cols[hh]].astype(F32) * beta[hh]
            kd_s[hh] = (kn[hh] * jnp.exp(gl_t - gc_t)).astype(BF16)
            qd_s[hh] = qn[hh] * egc
            for ch in range(wd):
                eg_s[slot, hh, ch * 8:(ch + 1) * 8, :] = jnp.broadcast_to(
                    jnp.exp(gl_t[ch * c:ch * c + 1, :]), (8, LANES))
        yield
        for _ in range(shift - 1):
            both = {hh: _dot(jnp.concatenate([ews[hh], pws[hh]], 0).astype(BF16), bdiag(pws[hh])) for hh in hs_}
            ews = {hh: ews[hh] + pws[hh] + both[hh][0:c] for hh in hs_}
            pws = {hh: both[hh][c:2 * c] for hh in hs_}
            yield
        rhss = {hh: rhs_s[hh] for hh in hs_}
        solb = {hh: (rhss[hh] + _dot(bdiag(ews[hh]), rhss[hh].astype(BF16))).astype(BF16) for hh in hs_}
        yield
        asol = {hh: _dot(bdiag(at_s[hh]), solb[hh]) for hh in hs_}
        for hh in hs_:
            op_s[slot, hh] = asol[hh][:, GDN_DK:]
            qp = (qd_s[hh] - asol[hh][:, :GDN_DK]).astype(BF16)
            for ch in range(wd):
                mq_s[slot, hh, ch, GDN_DK:GDN_DK + c, :] = qp[ch * c:(ch + 1) * c]
        yield
        for ch in range(wd):
            cr = slice(ch * c, (ch + 1) * c)
            mns = {hh: _dot_tn(kd_s[hh, cr, :], solb[hh][cr]) for hh in hs_}
            for hh in hs_:
                mq_s[slot, hh, ch, 0:GDN_DK, :] = mns[hh][:, :GDN_DK].astype(BF16)
                nn_s[slot, hh, ch] = mns[hh][:, GDN_DK:]
            if ch % 2 == 1:
                yield

    def scan_steps(i, slot):
        for ch in range(wd):
            rows = pl.ds(pl.multiple_of(i * blk + ch * c, c), c)
            for hh in hs_all:
                cols = slice(hh * GDN_DK, (hh + 1) * GDN_DK)
                state = st_s[hh]
                x = _dot(mq_s[slot, hh, ch], state.astype(BF16))
                st_s[hh] = state * eg_s[slot, hh, ch * 8:ch * 8 + 1, :] - x[:GDN_DK] + nn_s[slot, hh, ch]
                o = x[GDN_DK:] + op_s[slot, hh, ch * c:(ch + 1) * c, :]
                y = o * lax.rsqrt(jnp.mean(o * o, -1, keepdims=True) + EPS) * nw_ref[...]
                gt = gt_ref[0, rows, cols].astype(F32)
                o_ref[0, rows, cols] = (y * _silu(gt)).astype(o_ref.dtype)
            yield

    def interleave(gens):
        gens = list(gens)
        while gens:
            for gen in list(gens):
                try:
                    next(gen)
                except StopIteration:
                    gens.remove(gen)

    def stage(prep_i, prep_slot, scan_i, scan_slot):
        gens = []
        if prep_i is not None:
            gens += [prep_steps(prep_i, prep_slot, hs_all[g0:g0 + GDN_LOCKSTEP])
                     for g0 in range(0, GDN_HG, GDN_LOCKSTEP)]
        if scan_i is not None:
            gens.append(scan_steps(scan_i, scan_slot))
        interleave(gens)

    assert n_blocks >= 2 and n_blocks % 2 == 0
    st_s[...] = jnp.zeros_like(st_s)
    stage(0, 0, None, None)

    def body(j, carry):
        stage(2 * j + 1, 1, 2 * j, 0)
        stage(2 * j + 2, 0, 2 * j + 1, 1)
        return carry

    lax.fori_loop(0, (n_blocks - 2) // 2, body, 0)
    stage(n_blocks - 1, 1, n_blocks - 2, 0)
    stage(None, None, n_blocks - 1, 1)


def _gdn(qkvg, small, a_log, dt_bias, norm_w):
    b, s, _ = qkvg.shape
    gw = GDN_HG * GDN_DK
    groups = GDN_HEADS // GDN_HG
    blk = lambda off, nbuf=1: pl.BlockSpec((1, s, gw), lambda i, j: (i, 0, off + j), pipeline_mode=pl.Buffered(nbuf))
    prm = jnp.zeros((8, LANES), F32)
    prm = prm.at[0, GDN_HEADS:2 * GDN_HEADS].set(a_log).at[1, GDN_HEADS:2 * GDN_HEADS].set(dt_bias)
    return pl.pallas_call(
        functools.partial(_gdn_kernel, seq=s),
        grid=(b, groups),
        in_specs=[pl.BlockSpec((8, LANES), lambda i, j: (0, 0)),
                  blk(0, 2), blk(groups, 2), blk(2 * groups), blk(3 * groups),
                  pl.BlockSpec((1, s, LANES), lambda i, j: (i, 0, 0)),
                  pl.BlockSpec((1, GDN_DV), lambda i, j: (0, 0))],
        out_specs=pl.BlockSpec((1, s, gw), lambda i, j: (i, 0, j)),
        out_shape=jax.ShapeDtypeStruct((b, s, GDN_W), BF16),
        scratch_shapes=[pltpu.VMEM((2, GDN_HG, GDN_WIDE, GDN_DK + GDN_CHUNK, GDN_DK), BF16),
                        pltpu.VMEM((2, GDN_HG, GDN_WIDE, GDN_DK, GDN_DV), F32),
                        pltpu.VMEM((2, GDN_HG, GDN_BLK, GDN_DV), F32),
                        pltpu.VMEM((2, GDN_HG, GDN_WIDE * 8, LANES), F32),
                        pltpu.VMEM((GDN_HG, GDN_DK, GDN_DV), F32),
                        pltpu.VMEM((GDN_HG, GDN_BLK, GDN_DK + GDN_DV), F32),
                        pltpu.VMEM((GDN_HG, GDN_BLK, GDN_DK), BF16),
                        pltpu.VMEM((GDN_HG, GDN_BLK, GDN_DK), F32),
                        pltpu.VMEM((GDN_HG, GDN_CHUNK, GDN_BLK), F32)],
        compiler_params=_cparams(2, GDN_VMEM_LIMIT),
        name="gated_delta_rule",
    )(prm, qkvg, qkvg, qkvg, qkvg, small, norm_w.reshape(1, GDN_DV))


def kernel(x, positions, ev_w_in, ev_w_out, od_w_in, od_conv_w, od_a_log, od_dt_bias, od_norm_w, od_w_out,
           ffn_w_up, ffn_conv_w, ffn_conv_b, ffn_w_down, ln1_g, ln1_b, ln2_g, ln2_b):
    b, s, d = x.shape
    pos3 = positions.reshape(b, s, 1)
    ret_consts = _ret_consts()
    row = lambda a: a.reshape(1, d)

    for layer in range(DEPTH):
        j = layer // 2
        if layer == 0:
            hproj, tab_ret, tab_dil = _proj(x, ev_w_in[j].astype(BF16), ROW_TILE, 512, pos3, _rope_freqs())
        elif layer % 2 == 0:
            hproj = _proj(x, ev_w_in[j].astype(BF16), ROW_TILE, 512)
        if layer % 2 == 0:
            ya = _retention(hproj, tab_ret, ret_consts)
            yb = _dilated(hproj, tab_dil)
            w_out = ev_w_out[j].astype(BF16)
            x = _out_ln(x, [ya, yb], [w_out[:RET_V_W], w_out[RET_V_W:]], row(ln1_g[layer]), row(ln1_b[layer]),
                        ROW_TILE)
        else:
            w_in = od_w_in[j]
            w_main = w_in[:, :4 * GDN_W].astype(BF16)
            w_small = jnp.pad(w_in[:, 4 * GDN_W:], ((0, 0), (0, LANES - 2 * GDN_HEADS))).astype(BF16)
            cw = jnp.pad(od_conv_w[j], ((0, 8 - GDN_CONV), (0, 0)))
            qkvg, small = _odd_in(x, w_main, cw, w_small, ROW_TILE)
            yc = _gdn(qkvg, small, od_a_log[j], od_dt_bias[j], od_norm_w[j])
            x = _out_ln(x, [yc], [od_w_out[j].astype(BF16)], row(ln1_g[layer]), row(ln1_b[layer]), ROW_TILE)
        wup_r, cw_r, wdn_r = _ffn_weights(ffn_w_up[layer], ffn_conv_w[layer], ffn_conv_b[layer], ffn_w_down[layer])
        x = _ffn(x, wup_r, cw_r, wdn_r, row(ln2_g[layer]), row(ln2_b[layer]), ROW_TILE)
    return x
```
